```python
import math
import jax, jax.numpy as jnp
from jax import lax
import numpy as np

D_MODEL = 1024
BATCH = 32
SEQ = 2048
DEPTH = 2

CTX_LEN = 256
GRID_W = 64
EPS = 1e-6

ATT_HEADS = 4
ATT_QK_DIM = 64
ATT_V_DIM = 2 * ATT_QK_DIM
ATT_WIDTH = ATT_HEADS * ATT_V_DIM
Q_BLOCK = 128
ROPE_THETA = 10000.0

REC_HEADS = 4
REC_DK = 128
REC_DV = 128
REC_KEY_WIDTH = REC_HEADS * REC_DK
REC_WIDTH = REC_HEADS * REC_DV
CHUNK = 64

D_FF = 2752
N_EXPERTS = 8
TOP_K = 2
D_FF_EXPERT = 3584
N_DENSE = (DEPTH + 1) // 2
N_MOE = DEPTH // 2

IN_SIZES = [ATT_HEADS * 2 * ATT_QK_DIM,
            ATT_HEADS * 2 * ATT_QK_DIM,
            ATT_WIDTH,
            REC_KEY_WIDTH,
            REC_WIDTH,
            REC_KEY_WIDTH,
            REC_KEY_WIDTH,
            REC_WIDTH,
            D_MODEL,
            D_MODEL]
D_IN = sum(IN_SIZES)
IN_SPLIT_POINTS = [int(v) for v in np.cumsum(IN_SIZES)[:-1]]

kernel_name = 'hybrid_diffattn_hgrn2_moe_dit'


def _rmsnorm(x, w):
    x32 = x.astype(jnp.float32)
    y = x32 * lax.rsqrt(jnp.mean(x32 * x32, axis=-1, keepdims=True) + EPS)
    return y.astype(x.dtype) * w


def _modulate(h, shift, scale):
    return h * (1 + scale) + shift


def _heads(a, dh):
    b, t, _ = a.shape
    return a.reshape(b, t, -1, dh).transpose(0, 2, 1, 3)


def _merge_heads(a):
    b, h, t, dh = a.shape
    return a.transpose(0, 2, 1, 3).reshape(b, t, h * dh)


def _rope_2d(rows):
    row = jnp.repeat(jnp.arange(rows, dtype=jnp.float32), GRID_W)
    col = jnp.tile(jnp.arange(GRID_W, dtype=jnp.float32), rows)
    n_freq = ATT_QK_DIM // 4
    inv_freq = ROPE_THETA ** (-jnp.arange(n_freq, dtype=jnp.float32) / n_freq)
    ang = jnp.concatenate([row[:, None] * inv_freq, col[:, None] * inv_freq], axis=-1)
    return jnp.cos(ang), jnp.sin(ang)


def _apply_rope(x, cos, sin):
    half = x.shape[-1] // 2
    cos = cos.astype(x.dtype)
    sin = sin.astype(x.dtype)
    xa, xb = x[..., :half], x[..., half:]
    return jnp.concatenate([xa * cos - xb * sin, xa * sin + xb * cos], axis=-1)


def _diff_attend(q, k, v, lam, scale):
    s = jnp.einsum('bhcqd,bhckd->bhcqk', q, k).astype(jnp.float32) * scale
    p = jax.nn.softmax(s, axis=-1)
    a = p[:, :, 0] - lam * p[:, :, 1]
    return jnp.einsum('bhqk,bhkd->bhqd', a.astype(v.dtype), v)


def _diff_attend_blocked(q, k, v, lam, scale):
    b, h, _, t, dq = q.shape
    nb = t // Q_BLOCK
    qb = jnp.moveaxis(q.reshape(b, h, 2, nb, Q_BLOCK, dq), 3, 0)
    out = lax.map(lambda qi: _diff_attend(qi, k, v, lam, scale), qb)
    return jnp.moveaxis(out, 0, 2).reshape(b, h, t, v.shape[-1])


def _log_forget(z, lb):
    z32 = z.astype(jnp.float32)
    lb32 = lb.astype(jnp.float32)
    return jnp.logaddexp(jnp.log(lb32), jnp.log1p(-lb32) + jax.nn.log_sigmoid(z32))


def _gla_scan(q, k, v, logf, s0):
    b, h, t, _ = q.shape
    dv = v.shape[-1]
    n = t // CHUNK

    def chunks(a):
        return jnp.moveaxis(a.astype(jnp.float32).reshape(b, h, n, CHUNK, a.shape[-1]), 2, 0)

    lower = jnp.tril(jnp.ones((CHUNK, CHUNK), dtype=bool))[:, :, None]

    def step(state, inp):
        qc, kc, vc, lfc = inp
        cum = jnp.cumsum(lfc, axis=2)
        rel = jnp.where(lower, cum[:, :, :, None, :] - cum[:, :, None, :, :], -jnp.inf)
        scores = jnp.einsum('bhtd,bhsd,bhtsd->bhts', qc, kc, jnp.exp(rel))
        o = (jnp.einsum('bhts,bhsv->bhtv', scores, vc)
             + jnp.einsum('bhtd,bhdv->bhtv', qc * jnp.exp(cum), state))
        last = cum[:, :, -1:, :]
        new_state = (jnp.exp(last[:, :, 0, :])[..., None] * state
                     + jnp.einsum('bhsd,bhsv->bhdv', kc * jnp.exp(last - cum), vc))
        return new_state, o

    s_fin, o = lax.scan(step, s0, (chunks(q), chunks(k), chunks(v), chunks(logf)))
    return jnp.moveaxis(o, 0, 2).reshape(b, h, t, dv), s_fin


def _layer_lower_bounds(lb_param):
    cs = jnp.cumsum(jax.nn.softmax(lb_param.astype(jnp.float32), axis=0), axis=0)
    return cs - cs[0:1]


def _mixer(h_lat, h_ctx, w_in, lq1, lk1, lq2, lk2, att_norm_w, rec_norm_w, lb_f, lb_b,
           w_up_att, w_up_rec, w_out, lam_init, cos, sin, need_ctx):
    dt = h_lat.dtype
    p_lat = jnp.split(h_lat @ w_in, IN_SPLIT_POINTS, axis=-1)
    p_ctx = jnp.split(h_ctx @ w_in, IN_SPLIT_POINTS, axis=-1)

    lam = (jnp.exp(jnp.sum(lq1.astype(jnp.float32) * lk1.astype(jnp.float32)))
           - jnp.exp(jnp.sum(lq2.astype(jnp.float32) * lk2.astype(jnp.float32))) + lam_init)
    scale = ATT_QK_DIM ** -0.5

    def att_qkv(parts):
        b, t, _ = parts[0].shape
        q = parts[0].reshape(b, t, ATT_HEADS, 2, ATT_QK_DIM).transpose(0, 2, 3, 1, 4)
        k = parts[1].reshape(b, t, ATT_HEADS, 2, ATT_QK_DIM).transpose(0, 2, 3, 1, 4)
        v = _heads(parts[2], ATT_V_DIM)
        return q, k, v

    q_c, k_c, v_c = att_qkv(p_ctx)
    q_l, k_l, v_l = att_qkv(p_lat)
    q_l = _apply_rope(q_l, cos, sin)
    k_l = _apply_rope(k_l, cos, sin)
    k_all = jnp.concatenate([k_c, k_l], axis=3)
    v_all = jnp.concatenate([v_c, v_l], axis=2)
    o_att_l = _diff_attend_blocked(q_l, k_all, v_all, lam, scale)

    def att_out(o):
        return _merge_heads(_rmsnorm(o, att_norm_w) * (1 - lam_init))

    def rec_inputs(parts):
        q = _heads(jax.nn.silu(parts[3]), REC_DK)
        v = _heads(parts[4], REC_DV)
        lf_f = _heads(_log_forget(parts[5], lb_f), REC_DK)
        lf_b = _heads(_log_forget(parts[6], lb_b), REC_DK)
        return q, v, lf_f, lf_b

    def flip(a):
        return jnp.flip(a, axis=2)

    qr_c, vr_c, lf_cf, lf_cb = rec_inputs(p_ctx)
    qr_l, vr_l, lf_lf, lf_lb = rec_inputs(p_lat)
    b = h_lat.shape[0]
    s0 = jnp.zeros((b, REC_HEADS, REC_DK, REC_DV), jnp.float32)
    o_cf, s_cf = _gla_scan(qr_c, -jnp.expm1(lf_cf), vr_c, lf_cf, s0)
    o_cb, s_cb = _gla_scan(flip(qr_c), flip(-jnp.expm1(lf_cb)), flip(vr_c), flip(lf_cb), s0)
    o_lf, _ = _gla_scan(qr_l, -jnp.expm1(lf_lf), vr_l, lf_lf, s_cf)
    o_lb, _ = _gla_scan(flip(qr_l), flip(-jnp.expm1(lf_lb)), flip(vr_l), flip(lf_lb), s_cb)
    o_rec_l = o_lf + flip(o_lb)

    def rec_out(o, g):
        return _merge_heads(_rmsnorm(o.astype(dt), rec_norm_w)) * jax.nn.silu(g)

    def merge(y_att, y_rec, parts):
        g_att = jax.nn.sigmoid(parts[8])
        g_rec = jax.nn.sigmoid(parts[9])
        return (g_att * (y_att @ w_up_att) + g_rec * (y_rec @ w_up_rec)) @ w_out

    y_lat = merge(att_out(o_att_l), rec_out(o_rec_l, p_lat[7]), p_lat)
    y_ctx = None
    if need_ctx:
        o_att_c = _diff_attend(q_c, k_c, v_c, lam, scale)
        y_ctx = merge(att_out(o_att_c), rec_out(o_cf + flip(o_cb), p_ctx[7]), p_ctx)
    return y_lat, y_ctx


def _swiglu(h, w1, w3, w2):
    return (jax.nn.silu(h @ w1) * (h @ w3)) @ w2


def _moe(h, w_router, w1, w3, w2):
    logits = (h @ w_router).astype(jnp.float32)
    vals, idx = lax.top_k(logits, TOP_K)
    wts = jax.nn.softmax(vals, axis=-1)
    gates = jnp.sum(jax.nn.one_hot(idx, N_EXPERTS, dtype=jnp.float32) * wts[..., None], axis=-2).astype(h.dtype)
    out = jnp.zeros_like(h)
    for e in range(N_EXPERTS):
        out = out + gates[..., e:e + 1] * _swiglu(h, w1[e], w3[e], w2[e])
    return out


def setup_inputs(seed: int = 0) -> dict:
    key = jax.random.key(seed)
    ks = iter(jax.random.split(key, 40))

    def nrm(shape, scale):
        return jax.random.normal(next(ks), shape, jnp.float32) * scale

    def gain(shape):
        return 1.0 + nrm(shape, 0.05)

    D = D_MODEL
    return {
        'x': nrm((BATCH, SEQ, D), 1.0),
        'c': nrm((BATCH, D), 1.0),
        'ctx': nrm((BATCH, CTX_LEN, D), 1.0),
        'c_ctx': nrm((D,), 1.0),
        'w_ada': nrm((DEPTH, D, 6 * D), 0.5 * D ** -0.5),
        'b_ada': nrm((DEPTH, 6 * D), 0.02),
        'norm_mix_w': gain((DEPTH, D)),
        'norm_ffn_w': gain((DEPTH, D)),
        'w_in': nrm((DEPTH, D, D_IN), D ** -0.5),
        'lambda_q1': nrm((DEPTH, ATT_QK_DIM), 0.1),
        'lambda_k1': nrm((DEPTH, ATT_QK_DIM), 0.1),
        'lambda_q2': nrm((DEPTH, ATT_QK_DIM), 0.1),
        'lambda_k2': nrm((DEPTH, ATT_QK_DIM), 0.1),
        'att_norm_w': gain((DEPTH, ATT_V_DIM)),
        'rec_norm_w': gain((DEPTH, REC_DV)),
        'lb_fwd': 1.0 + nrm((DEPTH, REC_KEY_WIDTH), 0.5),
        'lb_bwd': 1.0 + nrm((DEPTH, REC_KEY_WIDTH), 0.5),
        'w_up_att': nrm((DEPTH, ATT_WIDTH, D), ATT_WIDTH ** -0.5),
        'w_up_rec': nrm((DEPTH, REC_WIDTH, D), REC_WIDTH ** -0.5),
        'w_out': nrm((DEPTH, D, D), D ** -0.5),
        'ffn_w1': nrm((N_DENSE, D, D_FF), D ** -0.5),
        'ffn_w3': nrm((N_DENSE, D, D_FF), D ** -0.5),
        'ffn_w2': nrm((N_DENSE, D_FF, D), D_FF ** -0.5),
        'router_w': nrm((N_MOE, D, N_EXPERTS), D ** -0.5),
        'moe_w1': nrm((N_MOE, N_EXPERTS, D, D_FF_EXPERT), D ** -0.5),
        'moe_w3': nrm((N_MOE, N_EXPERTS, D, D_FF_EXPERT), D ** -0.5),
        'moe_w2': nrm((N_MOE, N_EXPERTS, D_FF_EXPERT, D), D_FF_EXPERT ** -0.5),
        'final_norm_w': gain((D,)),
    }


def reference(x, c, ctx, c_ctx, w_ada, b_ada, norm_mix_w, norm_ffn_w, w_in,
              lambda_q1, lambda_k1, lambda_q2, lambda_k2, att_norm_w, rec_norm_w,
              lb_fwd, lb_bwd, w_up_att, w_up_rec, w_out, ffn_w1, ffn_w3, ffn_w2,
              router_w, moe_w1, moe_w3, moe_w2, final_norm_w):
    rows = x.shape[1] // GRID_W
    cos, sin = _rope_2d(rows)
    lbs_f = _layer_lower_bounds(lb_fwd)
    lbs_b = _layer_lower_bounds(lb_bwd)
    silu_c = jax.nn.silu(c)
    silu_cc = jax.nn.silu(c_ctx)

    for l in range(DEPTH):
        last = l == DEPTH - 1
        mod_l = [m[:, None, :] for m in jnp.split(silu_c @ w_ada[l] + b_ada[l], 6, axis=-1)]
        mod_c = jnp.split(silu_cc @ w_ada[l] + b_ada[l], 6, axis=-1)
        lam_init = 0.8 - 0.6 * math.exp(-0.3 * l)

        h_l = _modulate(_rmsnorm(x, norm_mix_w[l]), mod_l[0], mod_l[1])
        h_c = _modulate(_rmsnorm(ctx, norm_mix_w[l]), mod_c[0], mod_c[1])
        y_l, y_c = _mixer(h_l, h_c, w_in[l], lambda_q1[l], lambda_k1[l], lambda_q2[l], lambda_k2[l],
                          att_norm_w[l], rec_norm_w[l], lbs_f[l], lbs_b[l],
                          w_up_att[l], w_up_rec[l], w_out[l], lam_init, cos, sin, not last)
        x = x + mod_l[2] * y_l
        if not last:
            ctx = ctx + mod_c[2] * y_c

        if l % 2 == 0:
            j = l // 2
            ffn = lambda h, j=j: _swiglu(h, ffn_w1[j], ffn_w3[j], ffn_w2[j])
        else:
            j = l // 2
            ffn = lambda h, j=j: _moe(h, router_w[j], moe_w1[j], moe_w3[j], moe_w2[j])
        x = x + mod_l[5] * ffn(_modulate(_rmsnorm(x, norm_ffn_w[l]), mod_l[3], mod_l[4]))
        if not last:
            ctx = ctx + mod_c[5] * ffn(_modulate(_rmsnorm(ctx, norm_ffn_w[l]), mod_c[3], mod_c[4]))

    return _rmsnorm(x, final_norm_w)
```

```python
import functools
import math

import numpy as np
import jax
import jax.numpy as jnp
from jax import lax
from jax.experimental import pallas as pl
from jax.experimental.pallas import tpu as pltpu

F32 = jnp.float32
BF16 = jnp.bfloat16
HIGHEST = lax.Precision.HIGHEST

EPS = 1e-6
GRID_W = 64
ROPE_THETA = 10000.0
ATT_HEADS = 4
ATT_QK_DIM = 64
REC_HEADS = 4
N_EXPERTS = 8

LANES = 128
GLA_BLOCK = 256
GLA_CHUNK = 64
GLA_GROUP = 8
GLA_LEVELS = (8, 16, 32)
VMEM_LIMIT = 56 * 1024 * 1024

NT_DIMS = (((1,), (1,)), ((), ()))
TN_DIMS = (((0,), (0,)), ((), ()))


def _cparams(*sem):
    return pltpu.CompilerParams(dimension_semantics=sem, vmem_limit_bytes=VMEM_LIMIT)


def _silu(a):
    return a * jax.nn.sigmoid(a)


def _rms(x):
    return x * lax.rsqrt(jnp.mean(x * x, axis=-1, keepdims=True) + EPS)


def _mod_kernel(c_ref, w_ref, b_ref, o_ref):
    s = _silu(c_ref[...])
    o_ref[...] = jnp.dot(s, w_ref[...], precision=HIGHEST, preferred_element_type=F32) + b_ref[...]


def _mod_call(cc, w_ada, b_ada):
    depth, d, n = w_ada.shape
    rows = cc.shape[0]
    tn = 1536
    return pl.pallas_call(
        _mod_kernel,
        grid=(depth, n // tn),
        in_specs=[
            pl.BlockSpec((rows, d), lambda l, j: (0, 0)),
            pl.BlockSpec((None, d, tn), lambda l, j: (l, 0, j)),
            pl.BlockSpec((None, 1, tn), lambda l, j: (l, 0, j)),
        ],
        out_specs=pl.BlockSpec((None, rows, tn), lambda l, j: (l, 0, j)),
        out_shape=jax.ShapeDtypeStruct((depth, rows, n), F32),
        compiler_params=_cparams("arbitrary", "arbitrary"),
        name="mod",
    )(cc, w_ada, b_ada.reshape(depth, 1, n))


def _log_forget(z, la, l1):
    ls = jnp.minimum(z, 0.0) - jnp.log1p(jnp.exp(-jnp.abs(z)))
    t = l1 + ls
    return jnp.maximum(la, t) + jnp.log1p(jnp.exp(-jnp.abs(la - t)))


def _inproj_kernel(x_ref, mod_ref, nw_ref, w_ref, cos_ref, sin_ref, la_ref, l1_ref,
                   qk_ref, vg_ref, qi_ref, lf_ref, g_ref, h_scr, *, rope):
    s = pl.program_id(1)

    @pl.when(s == 0)
    def _():
        h = _rms(x_ref[...]) * nw_ref[...]
        h_scr[...] = (h * (1.0 + mod_ref[1:2, :]) + mod_ref[0:1, :]).astype(BF16)

    def proj():
        return jnp.dot(h_scr[...], w_ref[...], preferred_element_type=F32)

    half = w_ref.shape[1] // 2

    @pl.when(s == 0)
    def _():
        p = proj()
        if rope:
            cos = cos_ref[...]
            sin = sin_ref[...]
            lane = lax.broadcasted_iota(jnp.int32, cos.shape, 1)
            first = (lane % ATT_QK_DIM) < (ATT_QK_DIM // 2)
        for j in range(p.shape[1] // LANES):
            blk = p[:, j * LANES:(j + 1) * LANES]
            if rope:
                swapped = jnp.where(first,
                                    pltpu.roll(blk, LANES - ATT_QK_DIM // 2, 1),
                                    pltpu.roll(blk, ATT_QK_DIM // 2, 1))
                blk = blk * cos + swapped * sin
            if j * LANES < half:
                blk = blk * (ATT_QK_DIM ** -0.5)
            qk_ref[:, j * LANES:(j + 1) * LANES] = blk.astype(BF16)

    @pl.when(s == 1)
    def _():
        p = proj()
        vg_ref[:, :half] = p[:, :half].astype(BF16)
        vg_ref[:, half:] = _silu(p[:, half:]).astype(BF16)

    @pl.when(s == 2)
    def _():
        p = proj()
        qi_ref[:, :half] = _silu(p[:, :half]).astype(BF16)
        qi_ref[:, half:] = p[:, half:].astype(BF16)

    @pl.when(s == 3)
    def _():
        lf_ref[...] = _log_forget(proj(), la_ref[...], l1_ref[...])

    @pl.when(s >= 4)
    def _():
        g_ref[...] = jax.nn.sigmoid(proj()).astype(BF16)


def _inproj_call(x, mod, nw, w, cos, sin, la, l1, *, tm, rows_per_mod, rope):
    n, d = x.shape
    ts = 1024
    n_sec = w.shape[1] // ts
    pos_tiles = cos.shape[0] // tm

    def row(i, s):
        return (i, 0)

    def gate_idx(i, s):
        return (i, jnp.maximum(s - 4, 0))

    if rows_per_mod is None:
        mod_map = lambda i, s: (mod.shape[0] - 1, 0, 0)
    else:
        mod_map = lambda i, s: (i // (rows_per_mod // tm), 0, 0)
    outs = pl.pallas_call(
        functools.partial(_inproj_kernel, rope=rope),
        grid=(n // tm, n_sec),
        in_specs=[
            pl.BlockSpec((tm, d), row),
            pl.BlockSpec((None, 6, d), mod_map),
            pl.BlockSpec((1, d), lambda i, s: (0, 0)),
            pl.BlockSpec((d, ts), lambda i, s: (0, s)),
            pl.BlockSpec((tm, LANES), lambda i, s: (i % pos_tiles, 0)),
            pl.BlockSpec((tm, LANES), lambda i, s: (i % pos_tiles, 0)),
            pl.BlockSpec((1, ts), lambda i, s: (0, 0)),
            pl.BlockSpec((1, ts), lambda i, s: (0, 0)),
        ],
        out_specs=[
            pl.BlockSpec((tm, ts), row),
            pl.BlockSpec((tm, ts), row),
            pl.BlockSpec((tm, ts), row),
            pl.BlockSpec((tm, ts), row),
            pl.BlockSpec((tm, ts), gate_idx),
        ],
        out_shape=[
            jax.ShapeDtypeStruct((n, ts), BF16),
            jax.ShapeDtypeStruct((n, ts), BF16),
            jax.ShapeDtypeStruct((n, ts), BF16),
            jax.ShapeDtypeStruct((n, ts), F32),
            jax.ShapeDtypeStruct((n, 2 * ts), BF16),
        ],
        scratch_shapes=[pltpu.VMEM((tm, d), BF16)],
        compiler_params=_cparams("arbitrary", "arbitrary"),
        name="inproj_rope" if rope else "inproj",
    )(x, mod, nw, w, cos, sin, la, l1)
    return outs


def _attn_kernel(lam_ref, q_ref, *refs, post_scale, with_lat):
    if with_lat:
        kl_ref, vl_ref, kc_ref, vc_ref, nw_ref, o_ref = refs
    else:
        kc_ref, vc_ref, nw_ref, o_ref = refs
    q = q_ref[...]
    lane = lax.broadcasted_iota(jnp.int32, q.shape, 1)
    zero = jnp.zeros_like(q)
    lam = lam_ref[0]

    def probs(qm):
        sc = lax.dot_general(qm, kc_ref[...], NT_DIMS, preferred_element_type=F32)
        m = jnp.max(sc, axis=-1, keepdims=True)
        if with_lat:
            sl = lax.dot_general(qm, kl_ref[...], NT_DIMS, preferred_element_type=F32)
            m = jnp.maximum(m, jnp.max(sl, axis=-1, keepdims=True))
        ec = jnp.exp(sc - m)
        tot = jnp.sum(ec, axis=-1, keepdims=True)
        el = None
        if with_lat:
            el = jnp.exp(sl - m)
            tot = tot + jnp.sum(el, axis=-1, keepdims=True)
        return ec, el, 1.0 / tot

    e1c, e1l, i1 = probs(jnp.where(lane < ATT_QK_DIM, q, zero))
    e2c, e2l, i2 = probs(jnp.where(lane >= ATT_QK_DIM, q, zero))
    i2 = lam * i2
    o = jnp.dot((e1c * i1 - e2c * i2).astype(BF16), vc_ref[...], preferred_element_type=F32)
    if with_lat:
        o = o + jnp.dot((e1l * i1 - e2l * i2).astype(BF16), vl_ref[...], preferred_element_type=F32)
    o_ref[...] = (_rms(o) * nw_ref[...] * post_scale).astype(BF16)


def _attn_call(lam, q_src, qk_lat, vg_lat, qk_ctx, vg_ctx, nw, *, batch, tq, lam_init, with_lat):
    n_q = q_src.shape[0]
    t_q = n_q // batch
    t_lat = qk_lat.shape[0] // batch if with_lat else 0
    t_ctx = qk_ctx.shape[0] // batch
    nq_tiles = t_q // tq
    h = ATT_HEADS
    in_specs = [
        pl.BlockSpec(memory_space=pltpu.SMEM),
        pl.BlockSpec((tq, LANES), lambda b, hh, i: (b * nq_tiles + i, hh)),
    ]
    args = [lam, q_src]
    if with_lat:
        in_specs += [
            pl.BlockSpec((t_lat, LANES), lambda b, hh, i: (b, h + hh)),
            pl.BlockSpec((t_lat, LANES), lambda b, hh, i: (b, hh)),
        ]
        args += [qk_lat, vg_lat]
    in_specs += [
        pl.BlockSpec((t_ctx, LANES), lambda b, hh, i: (b, h + hh)),
        pl.BlockSpec((t_ctx, LANES), lambda b, hh, i: (b, hh)),
        pl.BlockSpec((1, LANES), lambda b, hh, i: (0, 0)),
    ]
    args += [qk_ctx, vg_ctx, nw]
    return pl.pallas_call(
        functools.partial(_attn_kernel, post_scale=1.0 - lam_init, with_lat=with_lat),
        grid=(batch, h, nq_tiles),
        in_specs=in_specs,
        out_specs=pl.BlockSpec((tq, LANES), lambda b, hh, i: (b * nq_tiles + i, hh)),
        out_shape=jax.ShapeDtypeStruct((n_q, h * LANES), BF16),
        compiler_params=_cparams("arbitrary", "arbitrary", "arbitrary"),
        name="attn_lat" if with_lat else "attn_ctx",
    )(*args)


def _gla_constants():
    n = GLA_BLOCK
    t = np.arange(n)[:, None]
    s = np.arange(n)[None, :]
    same_chunk = (t // GLA_CHUNK) == (s // GLA_CHUNK)
    tri = np.stack([same_chunk & (s <= t), same_chunk & (s >= t)]).astype(np.float32)
    masks = np.zeros((2, len(GLA_LEVELS), n, n), np.float32)
    for li, m in enumerate(GLA_LEVELS):
        same = (t // (2 * m)) == (s // (2 * m))
        t_hi = (t % (2 * m)) >= m
        s_hi = (s % (2 * m)) >= m
        masks[0, li] = same & t_hi & ~s_hi
        masks[1, li] = same & ~t_hi & s_hi
    return jnp.asarray(tri, BF16), jnp.asarray(masks, F32)


def _bcast_rows(c, period, offset):
    n = c.shape[0]
    parts = [jnp.broadcast_to(c[b * period + offset:b * period + offset + 1, :], (period, c.shape[1]))
             for b in range(n // period)]
    return jnp.concatenate(parts, axis=0)


def _split3(a):
    hi = a.astype(BF16)
    r = a - hi.astype(F32)
    mid = r.astype(BF16)
    lo = (r - mid.astype(F32)).astype(BF16)
    return hi, mid, lo


def _gla_block(q, v, lf, st, tri, masks, *, reverse):
    n = q.shape[0]
    qf = q.astype(F32)
    vf = v.astype(F32)
    k = 1.0 - jnp.exp(lf)

    hi, mid, lo = _split3(lf)
    c = (jnp.dot(tri, hi, preferred_element_type=F32)
         + jnp.dot(tri, mid, preferred_element_type=F32)
         + jnp.dot(tri, lo, preferred_element_type=F32))

    scores = jnp.zeros((n, n), F32)
    for li, m in enumerate(GLA_LEVELS):
        ref = _bcast_rows(c, 2 * m, m if reverse else m - 1)
        e = jnp.exp(-jnp.abs(c - ref))
        s_l = lax.dot_general((qf * e).astype(BF16), (k * e).astype(BF16), NT_DIMS,
                              preferred_element_type=F32)
        scores = scores + s_l * masks[li]
    o = jnp.dot(scores.astype(BF16), v, preferred_element_type=F32)

    ones = jnp.ones((LANES, LANES), BF16)
    row8 = lax.broadcasted_iota(jnp.int32, (n, LANES), 0) % GLA_GROUP
    for j in range(GLA_GROUP):
        if j == 0:
            x = qf * k
            vv = vf
        else:
            sh = (n - j) if reverse else j
            valid = (row8 <= GLA_GROUP - 1 - j) if reverse else (row8 >= j)
            kr = pltpu.roll(k, sh, 0)
            cr = pltpu.roll(c, sh, 0)
            vv = pltpu.roll(vf, sh, 0)
            x = jnp.where(valid, qf * kr * jnp.exp(jnp.where(valid, c - cr, 0.0)), 0.0)
        o = o + jnp.dot(x.astype(BF16), ones, preferred_element_type=F32) * vv

    last = 0 if reverse else GLA_CHUNK - 1
    tot = _bcast_rows(c, GLA_CHUNK, last)
    q_in = (qf * jnp.exp(c)).astype(BF16)
    k_out = (k * jnp.exp(tot - c)).astype(BF16)
    n_chunks = n // GLA_CHUNK
    outs = [None] * n_chunks
    order = range(n_chunks - 1, -1, -1) if reverse else range(n_chunks)
    for ci in order:
        lo_r, hi_r = ci * GLA_CHUNK, (ci + 1) * GLA_CHUNK
        outs[ci] = lax.dot_general(q_in[lo_r:hi_r], st.astype(BF16), NT_DIMS, preferred_element_type=F32)
        decay = jnp.exp(c[lo_r + last:lo_r + last + 1, :])
        upd = lax.dot_general(v[lo_r:hi_r], k_out[lo_r:hi_r], TN_DIMS, preferred_element_type=F32)
        st = st * decay + upd
    return o + jnp.concatenate(outs, axis=0), st


def _gla_kernel(ql_ref, il_ref, ffl_ref, fbl_ref, gl_ref, qc_ref, ic_ref, ffc_ref, fbc_ref, gc_ref,
                nw_ref, tri_ref, mask_ref, ol_ref, oc_ref, o_scr):
    n_blocks = ql_ref.shape[0] // GLA_BLOCK
    nw = nw_ref[...]
    zero_state = jnp.zeros((LANES, LANES), F32)

    def finish(o, g):
        return (_rms(o) * nw * g.astype(F32)).astype(BF16)

    def rows(i):
        return pl.ds(pl.multiple_of(i * GLA_BLOCK, GLA_BLOCK), GLA_BLOCK)

    o_cf, st = _gla_block(qc_ref[...], ic_ref[...], ffc_ref[...], zero_state,
                          tri_ref[0], mask_ref[0], reverse=False)

    def fwd(i, st):
        r = rows(i)
        o, st = _gla_block(ql_ref[r, :], il_ref[r, :], ffl_ref[r, :], st,
                           tri_ref[0], mask_ref[0], reverse=False)
        o_scr[r, :] = o
        return st

    lax.fori_loop(0, n_blocks, fwd, st)

    o_cb, st = _gla_block(qc_ref[...], ic_ref[...], fbc_ref[...], zero_state,
                          tri_ref[1], mask_ref[1], reverse=True)
    oc_ref[...] = finish(o_cf + o_cb, gc_ref[...])

    def bwd(step, st):
        r = rows(n_blocks - 1 - step)
        o, st = _gla_block(ql_ref[r, :], il_ref[r, :], fbl_ref[r, :], st,
                           tri_ref[1], mask_ref[1], reverse=True)
        ol_ref[r, :] = finish(o + o_scr[r, :], gl_ref[r, :])
        return st

    lax.fori_loop(0, n_blocks, bwd, st)


def _gla_call(qi_lat, lf_lat, vg_lat, qi_ctx, lf_ctx, vg_ctx, nw, *, batch):
    t_lat = qi_lat.shape[0] // batch
    t_ctx = qi_ctx.shape[0] // batch
    assert t_ctx == GLA_BLOCK and t_lat % GLA_BLOCK == 0
    h = REC_HEADS
    tri, masks = _gla_constants()

    def col(off):
        return lambda b, hh: (b, off + hh)

    def seq(t):
        return [pl.BlockSpec((t, LANES), col(0)), pl.BlockSpec((t, LANES), col(h)),
                pl.BlockSpec((t, LANES), col(0)), pl.BlockSpec((t, LANES), col(h)),
                pl.BlockSpec((t, LANES), col(h))]

    return pl.pallas_call(
        _gla_kernel,
        grid=(batch, h),
        in_specs=seq(t_lat) + seq(t_ctx) + [
            pl.BlockSpec((1, LANES), lambda b, hh: (0, 0)),
            pl.BlockSpec(tri.shape, lambda b, hh: (0, 0, 0)),
            pl.BlockSpec(masks.shape, lambda b, hh: (0, 0, 0, 0)),
        ],
        out_specs=[pl.BlockSpec((t_lat, LANES), col(0)), pl.BlockSpec((t_ctx, LANES), col(0))],
        out_shape=[jax.ShapeDtypeStruct((batch * t_lat, h * LANES), BF16),
                   jax.ShapeDtypeStruct((batch * t_ctx, h * LANES), BF16)],
        scratch_shapes=[pltpu.VMEM((t_lat, LANES), F32)],
        compiler_params=_cparams("arbitrary", "arbitrary"),
        name="gla",
    )(qi_lat, qi_lat, lf_lat, lf_lat, vg_lat, qi_ctx, qi_ctx, lf_ctx, lf_ctx, vg_ctx, nw, tri, masks)


def _top2_gates(logits):
    lane = lax.broadcasted_iota(jnp.int32, logits.shape, 1).astype(F32)
    big = float(LANES)
    m1 = jnp.max(logits, axis=-1, keepdims=True)
    i1 = jnp.min(jnp.where(logits == m1, lane, big), axis=-1, keepdims=True)
    rest = jnp.where(lane == i1, -jnp.inf, logits)
    m2 = jnp.max(rest, axis=-1, keepdims=True)
    i2 = jnp.min(jnp.where(rest == m2, lane, big), axis=-1, keepdims=True)
    e = jnp.exp(m2 - m1)
    w1 = 1.0 / (1.0 + e)
    return jnp.where(lane == i1, w1, 0.0) + jnp.where(lane == i2, e * w1, 0.0)


def _outproj_kernel(ya_ref, yr_ref, g_ref, x_ref, mod_ref, wua_ref, wur_ref, wo_ref, nw2_ref, *refs, router):
    if router:
        rw_ref, xo_ref, h2_ref, gate_ref = refs
    else:
        xo_ref, h2_ref = refs
    d = x_ref.shape[1]
    ua = jnp.dot(ya_ref[...], wua_ref[...], preferred_element_type=F32)
    ur = jnp.dot(yr_ref[...], wur_ref[...], preferred_element_type=F32)
    u = g_ref[:, :d].astype(F32) * ua + g_ref[:, d:].astype(F32) * ur
    y = jnp.dot(u.astype(BF16), wo_ref[...], preferred_element_type=F32)
    xn = x_ref[...] + mod_ref[2:3, :] * y
    xo_ref[...] = xn
    h2 = (_rms(xn) * nw2_ref[...]) * (1.0 + mod_ref[4:5, :]) + mod_ref[3:4, :]
    h2_ref[...] = h2.astype(BF16)
    if router:
        logits = jnp.dot(h2, rw_ref[...], precision=HIGHEST, preferred_element_type=F32)
        lane = lax.broadcasted_iota(jnp.int32, logits.shape, 1)
        gate_ref[...] = _top2_gates(jnp.where(lane < N_EXPERTS, logits, -jnp.inf))


def _outproj_call(ya, yr, g, x, mod, wua, wur, wo, nw2, rw, *, tm, rows_per_mod):
    n, d = x.shape
    router = rw is not None
    row = lambda i: (i, 0)
    const = lambda i: (0, 0)
    if rows_per_mod is None:
        mod_map = lambda i: (mod.shape[0] - 1, 0, 0)
    else:
        mod_map = lambda i: (i // (rows_per_mod // tm), 0, 0)
    in_specs = [
        pl.BlockSpec((tm, ya.shape[1]), row),
        pl.BlockSpec((tm, yr.shape[1]), row),
        pl.BlockSpec((tm, 2 * d), row),
        pl.BlockSpec((tm, d), row),
        pl.BlockSpec((None, 6, d), mod_map),
        pl.BlockSpec(wua.shape, const),
        pl.BlockSpec(wur.shape, const),
        pl.BlockSpec(wo.shape, const),
        pl.BlockSpec((1, d), const),
    ]
    args = [ya, yr, g, x, mod, wua, wur, wo, nw2]
    out_specs = [pl.BlockSpec((tm, d), row), pl.BlockSpec((tm, d), row)]
    out_shape = [jax.ShapeDtypeStruct((n, d), F32), jax.ShapeDtypeStruct((n, d), BF16)]
    if router:
        in_specs.append(pl.BlockSpec(rw.shape, const))
        args.append(rw)
        out_specs.append(pl.BlockSpec((tm, LANES), row))
        out_shape.append(jax.ShapeDtypeStruct((n, LANES), F32))
    return pl.pallas_call(
        functools.partial(_outproj_kernel, router=router),
        grid=(n // tm,),
        in_specs=in_specs,
        out_specs=out_specs,
        out_shape=out_shape,
        compiler_params=_cparams("arbitrary"),
        name="outproj_router" if router else "outproj",
    )(*args)


def _ffn_kernel(h_ref, x_ref, mod_ref, *refs, gated, final_norm):
    refs = list(refs)
    gate_ref = refs.pop(0) if gated else None
    w1_ref, w3_ref, w2_ref = refs[:3]
    refs = refs[3:]
    fnw_ref = refs.pop(0) if final_norm else None
    o_ref, acc_ref = refs
    e = pl.program_id(1)
    f = pl.program_id(2)

    @pl.when((e == 0) & (f == 0))
    def _():
        acc_ref[...] = jnp.zeros_like(acc_ref)

    h = h_ref[...]
    a = jnp.dot(h, w1_ref[...], preferred_element_type=F32)
    b = jnp.dot(h, w3_ref[...], preferred_element_type=F32)
    hid = _silu(a) * b
    if gated:
        gates = gate_ref[...]
        lane = lax.broadcasted_iota(jnp.int32, gates.shape, 1)
        hid = hid * jnp.sum(jnp.where(lane == e, gates, 0.0), axis=-1, keepdims=True)
    acc_ref[...] += jnp.dot(hid.astype(BF16), w2_ref[...], preferred_element_type=F32)

    @pl.when((e == pl.num_programs(1) - 1) & (f == pl.num_programs(2) - 1))
    def _():
        xn = x_ref[...] + mod_ref[5:6, :] * acc_ref[...]
        if final_norm:
            xn = _rms(xn) * fnw_ref[...]
        o_ref[...] = xn


def _ffn_call(h, x, mod, gates, w1, w3, w2, fnw, *, tm, tf, rows_per_mod):
    n, d = x.shape
    n_e, _, ff = w1.shape
    gated = gates is not None
    final_norm = fnw is not None
    row = lambda i, e, f: (i, 0)
    if rows_per_mod is None:
        mod_map = lambda i, e, f: (mod.shape[0] - 1, 0, 0)
    else:
        mod_map = lambda i, e, f: (i // (rows_per_mod // tm), 0, 0)
    in_specs = [pl.BlockSpec((tm, d), row), pl.BlockSpec((tm, d), row), pl.BlockSpec((None, 6, d), mod_map)]
    args = [h, x, mod]
    if gated:
        in_specs.append(pl.BlockSpec((tm, LANES), row))
        args.append(gates)
    in_specs += [
        pl.BlockSpec((None, d, tf), lambda i, e, f: (e, 0, f)),
        pl.BlockSpec((None, d, tf), lambda i, e, f: (e, 0, f)),
        pl.BlockSpec((None, tf, d), lambda i, e, f: (e, f, 0)),
    ]
    args += [w1, w3, w2]
    if final_norm:
        in_specs.append(pl.BlockSpec((1, d), lambda i, e, f: (0, 0)))
        args.append(fnw)
    return pl.pallas_call(
        functools.partial(_ffn_kernel, gated=gated, final_norm=final_norm),
        grid=(n // tm, n_e, ff // tf),
        in_specs=in_specs,
        out_specs=pl.BlockSpec((tm, d), row),
        out_shape=jax.ShapeDtypeStruct((n, d), F32),
        scratch_shapes=[pltpu.VMEM((tm, d), F32)],
        compiler_params=_cparams("arbitrary", "arbitrary", "arbitrary"),
        name="moe" if gated else "ffn",
    )(*args)


def _rope_tables(t_lat):
    rows = t_lat // GRID_W
    row = jnp.repeat(jnp.arange(rows, dtype=F32), GRID_W)
    col = jnp.tile(jnp.arange(GRID_W, dtype=F32), rows)
    n_freq = ATT_QK_DIM // 4
    inv_freq = ROPE_THETA ** (-jnp.arange(n_freq, dtype=F32) / n_freq)
    ang = jnp.concatenate([row[:, None] * inv_freq, col[:, None] * inv_freq], axis=-1)
    cos, sin = jnp.cos(ang), jnp.sin(ang)
    reps = LANES // ATT_QK_DIM
    return (jnp.tile(jnp.concatenate([cos, cos], axis=-1), (1, reps)),
            jnp.tile(jnp.concatenate([-sin, sin], axis=-1), (1, reps)))


def _layer_lower_bounds(lb_param):
    cs = jnp.cumsum(jax.nn.softmax(lb_param.astype(F32), axis=0), axis=0)
    return cs - cs[0:1]


def _win_columns(w_in_l):
    c = [w_in_l[:, i * 512:(i + 1) * 512] for i in range(8)]
    return jnp.concatenate([c[0], c[1], c[2], c[7], c[3], c[4], c[5], c[6], w_in_l[:, 4096:]], axis=1).astype(BF16)


def _pad_ff(w, axis, mult):
    ff = w.shape[axis]
    pad = (-ff) % mult
    if pad == 0:
        return w
    widths = [(0, 0)] * w.ndim
    widths[axis] = (0, pad)
    return jnp.pad(w, widths)


def kernel(x, c, ctx, c_ctx, w_ada, b_ada, norm_mix_w, norm_ffn_w, w_in, lambda_q1, lambda_k1, lambda_q2,
           lambda_k2, att_norm_w, rec_norm_w, lb_fwd, lb_bwd, w_up_att, w_up_rec, w_out, ffn_w1, ffn_w3,
           ffn_w2, router_w, moe_w1, moe_w3, moe_w2, final_norm_w):
    batch, t_lat, d = x.shape
    t_ctx = ctx.shape[1]
    depth = w_ada.shape[0]
    n_lat, n_ctx = batch * t_lat, batch * t_ctx
    tm = 512

    xl = x.reshape(n_lat, d)
    xc = ctx.reshape(n_ctx, d)

    pad_rows = (-(batch + 1)) % 8
    cc = jnp.concatenate([c, jnp.zeros((pad_rows, d), F32), c_ctx[None, :]], axis=0)
    mod_all = _mod_call(cc, w_ada, b_ada).reshape(depth, cc.shape[0], 6, d)

    cos, sin = _rope_tables(t_lat)
    lbs_f = _layer_lower_bounds(lb_fwd)
    lbs_b = _layer_lower_bounds(lb_bwd)

    for l in range(depth):
        last = l == depth - 1
        mod = mod_all[l]
        lam_init = 0.8 - 0.6 * math.exp(-0.3 * l)
        lam = (jnp.exp(jnp.sum(lambda_q1[l] * lambda_k1[l])) - jnp.exp(jnp.sum(lambda_q2[l] * lambda_k2[l]))
               + lam_init).reshape(1).astype(F32)
        lb = jnp.concatenate([lbs_f[l], lbs_b[l]])[None, :]
        la, l1 = jnp.log(lb), jnp.log1p(-lb)
        w_l = _win_columns(w_in[l])
        nw = norm_mix_w[l][None, :]

        pl_lat = _inproj_call(xl, mod, nw, w_l, cos, sin, la, l1, tm=tm, rows_per_mod=t_lat, rope=True)
        pl_ctx = _inproj_call(xc, mod, nw, w_l, cos, sin, la, l1, tm=tm, rows_per_mod=None, rope=False)
        qk_l, vg_l, qi_l, lf_l, g_l = pl_lat
        qk_c, vg_c, qi_c, lf_c, g_c = pl_ctx

        anw = att_norm_w[l][None, :]
        ya_l = _attn_call(lam, qk_l, qk_l, vg_l, qk_c, vg_c, anw, batch=batch, tq=256,
                          lam_init=lam_init, with_lat=True)
        yr_l, yr_c = _gla_call(qi_l, lf_l, vg_l, qi_c, lf_c, vg_c, rec_norm_w[l][None, :], batch=batch)

        wua = w_up_att[l].astype(BF16)
        wur = w_up_rec[l].astype(BF16)
        wo = w_out[l].astype(BF16)
        nw2 = norm_ffn_w[l][None, :]
        moe_layer = l % 2 == 1
        j = l // 2
        rw = None
        if moe_layer:
            rw = jnp.pad(router_w[j], ((0, 0), (0, LANES - N_EXPERTS)))
        res = _outproj_call(ya_l, yr_l, g_l, xl, mod, wua, wur, wo, nw2, rw, tm=tm, rows_per_mod=t_lat)
        xl, h2_l = res[0], res[1]
        if not last:
            ya_c = _attn_call(lam, qk_c, None, None, qk_c, vg_c, anw, batch=batch, tq=t_ctx,
                              lam_init=lam_init, with_lat=False)
            res_c = _outproj_call(ya_c, yr_c, g_c, xc, mod, wua, wur, wo, nw2, rw, tm=tm, rows_per_mod=None)
            xc, h2_c = res_c[0], res_c[1]

        fnw = final_norm_w[None, :] if last else None
        if moe_layer:
            w1 = moe_w1[j].astype(BF16)
            w3 = moe_w3[j].astype(BF16)
            w2 = moe_w2[j].astype(BF16)
            tf = 512
            xl = _ffn_call(h2_l, xl, mod, res[2], w1, w3, w2, fnw, tm=tm, tf=tf, rows_per_mod=t_lat)
            if not last:
                xc = _ffn_call(h2_c, xc, mod, res_c[2], w1, w3, w2, None, tm=tm, tf=tf, rows_per_mod=None)
        else:
            w1 = _pad_ff(ffn_w1[j], 1, 2 * LANES).astype(BF16)[None]
            w3 = _pad_ff(ffn_w3[j], 1, 2 * LANES).astype(BF16)[None]
            w2 = _pad_ff(ffn_w2[j], 0, 2 * LANES).astype(BF16)[None]
            tf = w1.shape[2] // 2
            xl = _ffn_call(h2_l, xl, mod, None, w1, w3, w2, fnw, tm=tm, tf=tf, rows_per_mod=t_lat)
            if not last:
                xc = _ffn_call(h2_c, xc, mod, None, w1, w3, w2, None, tm=tm, tf=tf, rows_per_mod=None)

    return xl.reshape(batch, t_lat, d)
```

```python
import functools
import math

import numpy as np
import jax
import jax.numpy as jnp
from jax import lax
from jax.experimental import pallas as pl
from jax.experimental.pallas import tpu as pltpu

F32 = jnp.float32
BF16 = jnp.bfloat16
HIGHEST = lax.Precision.HIGHEST

EPS = 1e-6
GRID_W = 64
ROPE_THETA = 10000.0
ATT_HEADS = 4
ATT_QK_DIM = 64
REC_HEADS = 4
N_EXPERTS = 8
Q_SCALE = ATT_QK_DIM ** -0.5 * math.log2(math.e)

LANES = 128
GLA_BLOCK = 256
GLA_CHUNK = 64
GLA_LEVELS = (1, 2, 4, 8, 16, 32)
VMEM_LIMIT = 56 * 1024 * 1024
MOE_VMEM_LIMIT = 60 * 1024 * 1024
MOE_ROWS = 256

NT_DIMS = (((1,), (1,)), ((), ()))
TN_DIMS = (((0,), (0,)), ((), ()))


def _cparams(*sem):
    return pltpu.CompilerParams(dimension_semantics=sem, vmem_limit_bytes=VMEM_LIMIT)


def _silu(a):
    return a * jax.nn.sigmoid(a)


def _rms(x):
    return x * lax.rsqrt(jnp.mean(x * x, axis=-1, keepdims=True) + EPS)


def _mod_kernel(c_ref, w_ref, b_ref, o_ref):
    s = _silu(c_ref[...])
    o_ref[...] = jnp.dot(s, w_ref[...], precision=HIGHEST, preferred_element_type=F32) + b_ref[...]


def _mod_call(cc, w_ada, b_ada):
    depth, d, n = w_ada.shape
    rows = cc.shape[0]
    tn = 1536
    return pl.pallas_call(
        _mod_kernel,
        grid=(depth, n // tn),
        in_specs=[
            pl.BlockSpec((rows, d), lambda l, j: (0, 0)),
            pl.BlockSpec((None, d, tn), lambda l, j: (l, 0, j)),
            pl.BlockSpec((None, 1, tn), lambda l, j: (l, 0, j)),
        ],
        out_specs=pl.BlockSpec((None, rows, tn), lambda l, j: (l, 0, j)),
        out_shape=jax.ShapeDtypeStruct((depth, rows, n), F32),
        compiler_params=_cparams("arbitrary", "arbitrary"),
        name="mod",
    )(cc, w_ada, b_ada.reshape(depth, 1, n))


def _log_forget(z, lb):
    t = jnp.exp(-jnp.abs(z))
    num = jnp.where(z >= 0.0, 1.0 + lb * t, lb + t)
    return jnp.where(num > 0.0, jnp.log(num / (1.0 + t)), z)


def _inproj_kernel(x_ref, mod_ref, nw_ref, w_ref, cos_ref, sin_ref, lb_ref,
                   qk_ref, vg_ref, qi_ref, lf_ref, g_ref, h_scr, *, rope):
    s = pl.program_id(1)

    @pl.when(s == 0)
    def _():
        h = _rms(x_ref[...]) * nw_ref[...]
        h_scr[...] = (h * (1.0 + mod_ref[1:2, :]) + mod_ref[0:1, :]).astype(BF16)

    def proj():
        return jnp.dot(h_scr[...], w_ref[...], preferred_element_type=F32)

    half = w_ref.shape[1] // 2

    @pl.when(s == 0)
    def _():
        p = proj()
        if rope:
            cos = cos_ref[...]
            sin = sin_ref[...]
            lane = lax.broadcasted_iota(jnp.int32, cos.shape, 1)
            first = (lane % ATT_QK_DIM) < (ATT_QK_DIM // 2)
        for j in range(p.shape[1] // LANES):
            blk = p[:, j * LANES:(j + 1) * LANES]
            if rope:
                swapped = jnp.where(first,
                                    pltpu.roll(blk, LANES - ATT_QK_DIM // 2, 1),
                                    pltpu.roll(blk, ATT_QK_DIM // 2, 1))
                blk = blk * cos + swapped * sin
            if j * LANES < half:
                blk = blk * Q_SCALE
            qk_ref[:, j * LANES:(j + 1) * LANES] = blk.astype(BF16)

    @pl.when(s == 1)
    def _():
        p = proj()
        vg_ref[:, :half] = p[:, :half].astype(BF16)
        vg_ref[:, half:] = _silu(p[:, half:]).astype(BF16)

    @pl.when(s == 2)
    def _():
        p = proj()
        qi_ref[:, :half] = _silu(p[:, :half]).astype(BF16)
        qi_ref[:, half:] = p[:, half:].astype(BF16)

    @pl.when(s == 3)
    def _():
        lf_ref[...] = _log_forget(proj(), lb_ref[...])

    @pl.when(s >= 4)
    def _():
        g_ref[...] = jax.nn.sigmoid(proj()).astype(BF16)


def _inproj_call(x, mod, nw, w, cos, sin, lb, *, tm, rows_per_mod, rope):
    n, d = x.shape
    ts = 1024
    n_sec = w.shape[1] // ts
    pos_tiles = cos.shape[0] // tm

    def row(i, s):
        return (i, 0)

    def gate_idx(i, s):
        return (i, jnp.maximum(s - 4, 0))

    if rows_per_mod is None:
        mod_map = lambda i, s: (mod.shape[0] - 1, 0, 0)
    else:
        mod_map = lambda i, s: (i // (rows_per_mod // tm), 0, 0)
    outs = pl.pallas_call(
        functools.partial(_inproj_kernel, rope=rope),
        grid=(n // tm, n_sec),
        in_specs=[
            pl.BlockSpec((tm, d), row),
            pl.BlockSpec((None, 6, d), mod_map),
            pl.BlockSpec((1, d), lambda i, s: (0, 0)),
            pl.BlockSpec((d, ts), lambda i, s: (0, s)),
            pl.BlockSpec((tm, LANES), lambda i, s: (i % pos_tiles, 0)),
            pl.BlockSpec((tm, LANES), lambda i, s: (i % pos_tiles, 0)),
            pl.BlockSpec((1, ts), lambda i, s: (0, 0)),
        ],
        out_specs=[
            pl.BlockSpec((tm, ts), row),
            pl.BlockSpec((tm, ts), row),
            pl.BlockSpec((tm, ts), row),
            pl.BlockSpec((tm, ts), row),
            pl.BlockSpec((tm, ts), gate_idx),
        ],
        out_shape=[
            jax.ShapeDtypeStruct((n, ts), BF16),
            jax.ShapeDtypeStruct((n, ts), BF16),
            jax.ShapeDtypeStruct((n, ts), BF16),
            jax.ShapeDtypeStruct((n, ts), F32),
            jax.ShapeDtypeStruct((n, 2 * ts), BF16),
        ],
        scratch_shapes=[pltpu.VMEM((tm, d), BF16)],
        compiler_params=_cparams("arbitrary", "arbitrary"),
        name="inproj_rope" if rope else "inproj",
    )(x, mod, nw, w, cos, sin, lb)
    return outs


def _attn_kernel(lam_ref, q_ref, *refs, post_scale, with_lat):
    if with_lat:
        kl_ref, vl_ref, kc_ref, vc_ref, nw_ref, o_ref, vce_scr, vle_scr = refs
    else:
        kc_ref, vc_ref, nw_ref, o_ref, vce_scr = refs

    @pl.when(pl.program_id(2) == 0)
    def _():
        vce_scr[:, :LANES] = vc_ref[...]
        vce_scr[:, LANES:] = jnp.ones(vc_ref.shape, BF16)
        if with_lat:
            vle_scr[:, :LANES] = vl_ref[...]
            vle_scr[:, LANES:] = jnp.ones(vl_ref.shape, BF16)

    q = q_ref[...]
    lane = lax.broadcasted_iota(jnp.int32, q.shape, 1)
    zero = jnp.zeros_like(q)

    def branch(qm):
        sc = lax.dot_general(qm, kc_ref[...], NT_DIMS, preferred_element_type=F32)
        m = jnp.max(sc, axis=-1, keepdims=True)
        if with_lat:
            sl = lax.dot_general(qm, kl_ref[...], NT_DIMS, preferred_element_type=F32)
            m = jnp.maximum(m, jnp.max(sl, axis=-1, keepdims=True))
        oe = jnp.dot(jnp.exp2(sc - m).astype(BF16), vce_scr[...], preferred_element_type=F32)
        if with_lat:
            oe = oe + jnp.dot(jnp.exp2(sl - m).astype(BF16), vle_scr[...], preferred_element_type=F32)
        return oe[:, :LANES] * (1.0 / oe[:, LANES:])

    o = branch(jnp.where(lane < ATT_QK_DIM, q, zero)) - lam_ref[0] * branch(jnp.where(lane >= ATT_QK_DIM, q, zero))
    o_ref[...] = (_rms(o) * nw_ref[...] * post_scale).astype(BF16)


def _attn_call(lam, q_src, qk_lat, vg_lat, qk_ctx, vg_ctx, nw, *, batch, tq, lam_init, with_lat):
    n_q = q_src.shape[0]
    t_q = n_q // batch
    t_lat = qk_lat.shape[0] // batch if with_lat else 0
    t_ctx = qk_ctx.shape[0] // batch
    nq_tiles = t_q // tq
    h = ATT_HEADS
    in_specs = [
        pl.BlockSpec(memory_space=pltpu.SMEM),
        pl.BlockSpec((tq, LANES), lambda b, hh, i: (b * nq_tiles + i, hh)),
    ]
    args = [lam, q_src]
    if with_lat:
        in_specs += [
            pl.BlockSpec((t_lat, LANES), lambda b, hh, i: (b, h + hh)),
            pl.BlockSpec((t_lat, LANES), lambda b, hh, i: (b, hh)),
        ]
        args += [qk_lat, vg_lat]
    in_specs += [
        pl.BlockSpec((t_ctx, LANES), lambda b, hh, i: (b, h + hh)),
        pl.BlockSpec((t_ctx, LANES), lambda b, hh, i: (b, hh)),
        pl.BlockSpec((1, LANES), lambda b, hh, i: (0, 0)),
    ]
    args += [qk_ctx, vg_ctx, nw]
    return pl.pallas_call(
        functools.partial(_attn_kernel, post_scale=1.0 - lam_init, with_lat=with_lat),
        grid=(batch, h, nq_tiles),
        in_specs=in_specs,
        out_specs=pl.BlockSpec((tq, LANES), lambda b, hh, i: (b * nq_tiles + i, hh)),
        out_shape=jax.ShapeDtypeStruct((n_q, h * LANES), BF16),
        scratch_shapes=[pltpu.VMEM((t_ctx, 2 * LANES), BF16)]
        + ([pltpu.VMEM((t_lat, 2 * LANES), BF16)] if with_lat else []),
        compiler_params=_cparams("arbitrary", "arbitrary", "arbitrary"),
        name="attn_lat" if with_lat else "attn_ctx",
    )(*args)


def _gla_constants():
    n = GLA_BLOCK
    idx = np.arange(n)
    same_chunk = (idx[:, None] // GLA_CHUNK) == (idx[None, :] // GLA_CHUNK)
    stacks, masks = [], []
    for reverse in (False, True):
        order = (idx[None, :] >= idx[:, None]) if reverse else (idx[None, :] <= idx[:, None])
        cum = (same_chunk & order).astype(np.int32)
        groups, lvl_masks = [cum], []
        for m in GLA_LEVELS:
            ref = (idx // (2 * m)) * (2 * m) + (m if reverse else m - 1)
            groups.append(np.abs(cum - cum[ref]))
            upper = (idx % (2 * m)) >= m
            same = (idx[:, None] // (2 * m)) == (idx[None, :] // (2 * m))
            q_side, k_side = (~upper, upper) if reverse else (upper, ~upper)
            lvl_masks.append(same & q_side[:, None] & k_side[None, :])
        end = (idx // GLA_CHUNK) * GLA_CHUNK + (0 if reverse else GLA_CHUNK - 1)
        groups.append(np.abs(cum[end] - cum))
        stacks.append(np.concatenate(groups, axis=0))
        masks.append(np.stack(lvl_masks))
    return jnp.asarray(np.stack(stacks), BF16), jnp.asarray(np.stack(masks), F32)


def _gla_block(q, v, lf, st, stack, masks, *, reverse):
    n = q.shape[0]
    qf = q.astype(F32)
    k = 1.0 - jnp.exp(lf)

    hi = lf.astype(BF16)
    mid = (lf - hi.astype(F32)).astype(BF16)
    expo = jnp.dot(stack, jnp.concatenate([hi, mid], axis=1), preferred_element_type=F32)

    def exponent(group):
        blk = expo[group * n:(group + 1) * n]
        return blk[:, :LANES] + blk[:, LANES:]

    c = exponent(0)
    scores = jnp.zeros((n, n), F32)
    for li in range(len(GLA_LEVELS)):
        e = jnp.exp(exponent(1 + li))
        s_l = lax.dot_general((qf * e).astype(BF16), (k * e).astype(BF16), NT_DIMS,
                              preferred_element_type=F32)
        scores = scores + s_l * masks[li]
    o = jnp.dot(scores.astype(BF16), v, preferred_element_type=F32)
    o = o + jnp.sum(qf * k, axis=-1, keepdims=True) * v.astype(F32)

    last = 0 if reverse else GLA_CHUNK - 1
    q_in = (qf * jnp.exp(c)).astype(BF16)
    k_out = (k * jnp.exp(exponent(1 + len(GLA_LEVELS)))).astype(BF16)
    n_chunks = n // GLA_CHUNK
    outs = [None] * n_chunks
    order = range(n_chunks - 1, -1, -1) if reverse else range(n_chunks)
    for ci in order:
        lo_r, hi_r = ci * GLA_CHUNK, (ci + 1) * GLA_CHUNK
        outs[ci] = lax.dot_general(q_in[lo_r:hi_r], st.astype(BF16), NT_DIMS, preferred_element_type=F32)
        decay = jnp.exp(c[lo_r + last:lo_r + last + 1, :])
        upd = lax.dot_general(v[lo_r:hi_r], k_out[lo_r:hi_r], TN_DIMS, preferred_element_type=F32)
        st = st * decay + upd
    return o + jnp.concatenate(outs, axis=0), st


def _gla_kernel(ql_ref, il_ref, ffl_ref, fbl_ref, gl_ref, qc_ref, ic_ref, ffc_ref, fbc_ref, gc_ref,
                nw_ref, stack_ref, mask_ref, ol_ref, oc_ref, of_scr, ob_scr):
    n_blocks = ql_ref.shape[0] // GLA_BLOCK
    nw = nw_ref[...]
    zero_state = jnp.zeros((LANES, LANES), F32)

    def finish(o, g):
        return (_rms(o) * nw * g.astype(F32)).astype(BF16)

    def rows(i):
        return pl.ds(pl.multiple_of(i * GLA_BLOCK, GLA_BLOCK), GLA_BLOCK)

    o_cf, st_f = _gla_block(qc_ref[...], ic_ref[...], ffc_ref[...], zero_state,
                            stack_ref[0], mask_ref[0], reverse=False)
    o_cb, st_b = _gla_block(qc_ref[...], ic_ref[...], fbc_ref[...], zero_state,
                            stack_ref[1], mask_ref[1], reverse=True)
    oc_ref[...] = finish(o_cf + o_cb, gc_ref[...])

    def step(i, carry):
        st_f, st_b = carry
        rf = rows(i)
        rb = rows(n_blocks - 1 - i)
        o_f, st_f = _gla_block(ql_ref[rf, :], il_ref[rf, :], ffl_ref[rf, :], st_f,
                               stack_ref[0], mask_ref[0], reverse=False)
        o_b, st_b = _gla_block(ql_ref[rb, :], il_ref[rb, :], fbl_ref[rb, :], st_b,
                               stack_ref[1], mask_ref[1], reverse=True)
        of_scr[rf, :] = o_f
        ob_scr[rb, :] = o_b
        return st_f, st_b

    lax.fori_loop(0, n_blocks, step, (st_f, st_b))

    def emit(i, carry):
        r = rows(i)
        ol_ref[r, :] = finish(of_scr[r, :] + ob_scr[r, :], gl_ref[r, :])
        return carry

    lax.fori_loop(0, n_blocks, emit, 0)


def _gla_call(qi_lat, lf_lat, vg_lat, qi_ctx, lf_ctx, vg_ctx, nw, *, batch):
    t_lat = qi_lat.shape[0] // batch
    t_ctx = qi_ctx.shape[0] // batch
    assert t_ctx == GLA_BLOCK and t_lat % GLA_BLOCK == 0
    h = REC_HEADS
    stack, masks = _gla_constants()

    def col(off):
        return lambda b, hh: (b, off + hh)

    def seq(t):
        return [pl.BlockSpec((t, LANES), col(0)), pl.BlockSpec((t, LANES), col(h)),
                pl.BlockSpec((t, LANES), col(0)), pl.BlockSpec((t, LANES), col(h)),
                pl.BlockSpec((t, LANES), col(h))]

    return pl.pallas_call(
        _gla_kernel,
        grid=(batch, h),
        in_specs=seq(t_lat) + seq(t_ctx) + [
            pl.BlockSpec((1, LANES), lambda b, hh: (0, 0)),
            pl.BlockSpec(stack.shape, lambda b, hh: (0, 0, 0)),
            pl.BlockSpec(masks.shape, lambda b, hh: (0, 0, 0, 0)),
        ],
        out_specs=[pl.BlockSpec((t_lat, LANES), col(0)), pl.BlockSpec((t_ctx, LANES), col(0))],
        out_shape=[jax.ShapeDtypeStruct((batch * t_lat, h * LANES), BF16),
                   jax.ShapeDtypeStruct((batch * t_ctx, h * LANES), BF16)],
        scratch_shapes=[pltpu.VMEM((t_lat, LANES), F32), pltpu.VMEM((t_lat, LANES), F32)],
        compiler_params=_cparams("arbitrary", "arbitrary"),
        name="gla",
    )(qi_lat, qi_lat, lf_lat, lf_lat, vg_lat, qi_ctx, qi_ctx, lf_ctx, lf_ctx, vg_ctx, nw, stack, masks)


def _top2_gates(logits):
    lane = lax.broadcasted_iota(jnp.int32, logits.shape, 1).astype(F32)
    big = float(LANES)
    m1 = jnp.max(logits, axis=-1, keepdims=True)
    i1 = jnp.min(jnp.where(logits == m1, lane, big), axis=-1, keepdims=True)
    rest = jnp.where(lane == i1, -jnp.inf, logits)
    m2 = jnp.max(rest, axis=-1, keepdims=True)
    i2 = jnp.min(jnp.where(rest == m2, lane, big), axis=-1, keepdims=True)
    e = jnp.exp(m2 - m1)
    w1 = 1.0 / (1.0 + e)
    return jnp.where(lane == i1, w1, 0.0) + jnp.where(lane == i2, e * w1, 0.0)


def _outproj_kernel(ya_ref, yr_ref, g_ref, x_ref, mod_ref, wua_ref, wur_ref, wo_ref, nw2_ref, *refs,
                    router, tiles_per_block):
    if router:
        rw_ref, tri_ref, xo_ref, h2_ref, gate_ref, pos_ref, cnt_ref, carry_scr = refs
    else:
        xo_ref, h2_ref = refs
    d = x_ref.shape[1]
    ua = jnp.dot(ya_ref[...], wua_ref[...], preferred_element_type=F32)
    ur = jnp.dot(yr_ref[...], wur_ref[...], preferred_element_type=F32)
    u = g_ref[:, :d].astype(F32) * ua + g_ref[:, d:].astype(F32) * ur
    y = jnp.dot(u.astype(BF16), wo_ref[...], preferred_element_type=F32)
    xn = x_ref[...] + mod_ref[2:3, :] * y
    xo_ref[...] = xn
    h2 = (_rms(xn) * nw2_ref[...]) * (1.0 + mod_ref[4:5, :]) + mod_ref[3:4, :]
    h2_hi = h2.astype(BF16)
    h2_ref[...] = h2_hi
    if router:
        h2_lo = (h2 - h2_hi.astype(F32)).astype(BF16)
        logits = (jnp.dot(h2_hi, rw_ref[0], preferred_element_type=F32)
                  + jnp.dot(h2_lo, rw_ref[0], preferred_element_type=F32)
                  + jnp.dot(h2_hi, rw_ref[1], preferred_element_type=F32))
        lane = lax.broadcasted_iota(jnp.int32, logits.shape, 1)
        gates = _top2_gates(jnp.where(lane < N_EXPERTS, logits, -jnp.inf))
        gate_ref[...] = gates

        @pl.when(pl.program_id(0) % tiles_per_block == 0)
        def _():
            carry_scr[...] = jnp.zeros_like(carry_scr)

        sel = gates > 0.0
        sel_f = jnp.where(sel, 1.0, 0.0)
        rank = jnp.dot(tri_ref[...], sel_f.astype(BF16), preferred_element_type=F32) + carry_scr[...]
        pos_ref[...] = jnp.where(sel, rank, -1.0)
        total = carry_scr[...] + jnp.sum(sel_f, axis=0, keepdims=True)
        carry_scr[...] = total
        cnt_ref[...] = total


def _outproj_call(ya, yr, g, x, mod, wua, wur, wo, nw2, rw, *, tm, rows_per_mod, route_block=None):
    n, d = x.shape
    router = rw is not None
    row = lambda i: (i, 0)
    const = lambda i: (0, 0)
    tiles_per_block = route_block // tm if router else 1
    if rows_per_mod is None:
        mod_map = lambda i: (mod.shape[0] - 1, 0, 0)
    else:
        mod_map = lambda i: (i // (rows_per_mod // tm), 0, 0)
    in_specs = [
        pl.BlockSpec((tm, ya.shape[1]), row),
        pl.BlockSpec((tm, yr.shape[1]), row),
        pl.BlockSpec((tm, 2 * d), row),
        pl.BlockSpec((tm, d), row),
        pl.BlockSpec((None, 6, d), mod_map),
        pl.BlockSpec(wua.shape, const),
        pl.BlockSpec(wur.shape, const),
        pl.BlockSpec(wo.shape, const),
        pl.BlockSpec((1, d), const),
    ]
    args = [ya, yr, g, x, mod, wua, wur, wo, nw2]
    out_specs = [pl.BlockSpec((tm, d), row), pl.BlockSpec((tm, d), row)]
    out_shape = [jax.ShapeDtypeStruct((n, d), F32), jax.ShapeDtypeStruct((n, d), BF16)]
    scratch = []
    if router:
        tri = jnp.asarray(np.tril(np.ones((tm, tm), np.float32), -1), BF16)
        in_specs += [pl.BlockSpec(rw.shape, lambda i: (0, 0, 0)), pl.BlockSpec(tri.shape, const)]
        args += [rw, tri]
        out_specs += [pl.BlockSpec((tm, LANES), row), pl.BlockSpec((tm, LANES), row),
                      pl.BlockSpec((None, 1, LANES), lambda i: (i // tiles_per_block, 0, 0))]
        out_shape += [jax.ShapeDtypeStruct((n, LANES), F32), jax.ShapeDtypeStruct((n, LANES), F32),
                      jax.ShapeDtypeStruct((n // route_block, 1, LANES), F32)]
        scratch = [pltpu.VMEM((1, LANES), F32)]
    return pl.pallas_call(
        functools.partial(_outproj_kernel, router=router, tiles_per_block=tiles_per_block),
        grid=(n // tm,),
        in_specs=in_specs,
        out_specs=out_specs,
        out_shape=out_shape,
        scratch_shapes=scratch,
        compiler_params=_cparams("arbitrary"),
        name="outproj_router" if router else "outproj",
    )(*args)


def _swiglu_chunk(xs, w1_ref, w3_ref, w2_ref):
    a = jnp.dot(xs, w1_ref[...], preferred_element_type=F32)
    b = jnp.dot(xs, w3_ref[...], preferred_element_type=F32)
    return jnp.dot((_silu(a) * b).astype(BF16), w2_ref[...], preferred_element_type=F32)


def _ffn_kernel(h_ref, x_ref, mod_ref, w1_ref, w3_ref, w2_ref, o_ref, acc_ref):
    f = pl.program_id(1)

    @pl.when(f == 0)
    def _():
        acc_ref[...] = jnp.zeros_like(acc_ref)

    acc_ref[...] += _swiglu_chunk(h_ref[...], w1_ref, w3_ref, w2_ref)

    @pl.when(f == pl.num_programs(1) - 1)
    def _():
        o_ref[...] = x_ref[...] + mod_ref[5:6, :] * acc_ref[...]


def _ffn_call(h, x, mod, w1, w3, w2, *, tm, tf, rows_per_mod):
    n, d = x.shape
    ff = w1.shape[1]
    row = lambda i, f: (i, 0)
    if rows_per_mod is None:
        mod_map = lambda i, f: (mod.shape[0] - 1, 0, 0)
    else:
        mod_map = lambda i, f: (i // (rows_per_mod // tm), 0, 0)
    return pl.pallas_call(
        _ffn_kernel,
        grid=(n // tm, ff // tf),
        in_specs=[
            pl.BlockSpec((tm, d), row), pl.BlockSpec((tm, d), row), pl.BlockSpec((None, 6, d), mod_map),
            pl.BlockSpec((d, tf), lambda i, f: (0, f)),
            pl.BlockSpec((d, tf), lambda i, f: (0, f)),
            pl.BlockSpec((tf, d), lambda i, f: (f, 0)),
        ],
        out_specs=pl.BlockSpec((tm, d), row),
        out_shape=jax.ShapeDtypeStruct((n, d), F32),
        scratch_shapes=[pltpu.VMEM((tm, d), F32)],
        compiler_params=_cparams("arbitrary", "arbitrary"),
        name="ffn",
    )(h, x, mod, w1, w3, w2)


def _moe_kernel(cnt_ref, h_ref, pos_ref, gate_ref, w1_ref, w3_ref, w2_ref, o_ref, xg_scr, y_scr, gr_scr):
    b, e, f = pl.program_id(0), pl.program_id(1), pl.program_id(2)
    n_e, n_f = pl.num_programs(1), pl.num_programs(2)
    n_tiles = (cnt_ref[b * n_e + e] + MOE_ROWS - 1) // MOE_ROWS
    tb = h_ref.shape[0]
    pos_row = pos_ref[pl.ds(e, 1), :]

    def tile_rows(s):
        return pl.ds(pl.multiple_of(s * MOE_ROWS, MOE_ROWS), MOE_ROWS)

    def one_hot(s):
        slot = lax.broadcasted_iota(jnp.int32, (MOE_ROWS, tb), 0) + s * MOE_ROWS
        return pos_row == slot.astype(F32)

    @pl.when((e == 0) & (f == 0))
    def _():
        o_ref[...] = jnp.zeros_like(o_ref)

    @pl.when(f == 0)
    def _():
        gate_row = gate_ref[pl.ds(e, 1), :]

        def gather(s, carry):
            hit = one_hot(s)
            rows = tile_rows(s)
            xg_scr[rows, :] = jnp.dot(jnp.where(hit, 1.0, 0.0).astype(BF16), h_ref[...],
                                      preferred_element_type=F32).astype(BF16)
            gr_scr[rows, :] = jnp.sum(jnp.where(hit, gate_row, 0.0), axis=-1, keepdims=True)
            return carry

        lax.fori_loop(0, n_tiles, gather, 0)

    def expert(s, carry):
        rows = tile_rows(s)
        part = _swiglu_chunk(xg_scr[rows, :], w1_ref, w3_ref, w2_ref)

        @pl.when(f == 0)
        def _():
            y_scr[rows, :] = part

        @pl.when(f > 0)
        def _():
            y_scr[rows, :] += part

        return carry

    lax.fori_loop(0, n_tiles, expert, 0)

    @pl.when(f == n_f - 1)
    def _():
        def scatter(s, carry):
            rows = tile_rows(s)
            y = (y_scr[rows, :] * gr_scr[rows, :]).astype(BF16)
            o_ref[...] += lax.dot_general(jnp.where(one_hot(s), 1.0, 0.0).astype(BF16), y, TN_DIMS,
                                          preferred_element_type=F32)
            return carry

        lax.fori_loop(0, n_tiles, scatter, 0)


def _moe_call(cnt, h, pos_t, gate_t, w1, w3, w2, *, tb, tf):
    n, d = h.shape
    n_e, _, ff = w1.shape
    grid_spec = pltpu.PrefetchScalarGridSpec(
        num_scalar_prefetch=1,
        grid=(n // tb, n_e, ff // tf),
        in_specs=[
            pl.BlockSpec((tb, d), lambda b, e, f, c: (b, 0)),
            pl.BlockSpec((n_e, tb), lambda b, e, f, c: (0, b)),
            pl.BlockSpec((n_e, tb), lambda b, e, f, c: (0, b)),
            pl.BlockSpec((None, d, tf), lambda b, e, f, c: (e, 0, f)),
            pl.BlockSpec((None, d, tf), lambda b, e, f, c: (e, 0, f)),
            pl.BlockSpec((None, tf, d), lambda b, e, f, c: (e, f, 0)),
        ],
        out_specs=pl.BlockSpec((tb, d), lambda b, e, f, c: (b, 0)),
        scratch_shapes=[pltpu.VMEM((tb, d), BF16), pltpu.VMEM((tb, d), F32), pltpu.VMEM((tb, 1), F32)],
    )
    return pl.pallas_call(
        _moe_kernel,
        grid_spec=grid_spec,
        out_shape=jax.ShapeDtypeStruct((n, d), F32),
        compiler_params=pltpu.CompilerParams(dimension_semantics=("arbitrary",) * 3,
                                             vmem_limit_bytes=MOE_VMEM_LIMIT),
        name="moe",
    )(cnt, h, pos_t, gate_t, w1, w3, w2)


def _resnorm_kernel(x_ref, y_ref, mod_ref, fnw_ref, o_ref):
    o_ref[...] = _rms(x_ref[...] + mod_ref[5:6, :] * y_ref[...]) * fnw_ref[...]


def _resnorm_call(x, y, mod, fnw, *, tm, rows_per_mod):
    n, d = x.shape
    row = lambda i: (i, 0)
    return pl.pallas_call(
        _resnorm_kernel,
        grid=(n // tm,),
        in_specs=[pl.BlockSpec((tm, d), row), pl.BlockSpec((tm, d), row),
                  pl.BlockSpec((None, 6, d), lambda i: (i // (rows_per_mod // tm), 0, 0)),
                  pl.BlockSpec((1, d), lambda i: (0, 0))],
        out_specs=pl.BlockSpec((tm, d), row),
        out_shape=jax.ShapeDtypeStruct((n, d), F32),
        compiler_params=_cparams("arbitrary"),
        name="resnorm",
    )(x, y, mod, fnw)


def _rope_tables(t_lat):
    rows = t_lat // GRID_W
    row = jnp.repeat(jnp.arange(rows, dtype=F32), GRID_W)
    col = jnp.tile(jnp.arange(GRID_W, dtype=F32), rows)
    n_freq = ATT_QK_DIM // 4
    inv_freq = ROPE_THETA ** (-jnp.arange(n_freq, dtype=F32) / n_freq)
    ang = jnp.concatenate([row[:, None] * inv_freq, col[:, None] * inv_freq], axis=-1)
    cos, sin = jnp.cos(ang), jnp.sin(ang)
    reps = LANES // ATT_QK_DIM
    return (jnp.tile(jnp.concatenate([cos, cos], axis=-1), (1, reps)),
            jnp.tile(jnp.concatenate([-sin, sin], axis=-1), (1, reps)))


def _layer_lower_bounds(lb_param):
    cs = jnp.cumsum(jax.nn.softmax(lb_param.astype(F32), axis=0), axis=0)
    return cs - cs[0:1]


def _win_columns(w_in_l):
    c = [w_in_l[:, i * 512:(i + 1) * 512] for i in range(8)]
    return jnp.concatenate([c[0], c[1], c[2], c[7], c[3], c[4], c[5], c[6], w_in_l[:, 4096:]], axis=1).astype(BF16)


def _pad_ff(w, axis, mult):
    ff = w.shape[axis]
    pad = (-ff) % mult
    if pad == 0:
        return w
    widths = [(0, 0)] * w.ndim
    widths[axis] = (0, pad)
    return jnp.pad(w, widths)


def kernel(x, c, ctx, c_ctx, w_ada, b_ada, norm_mix_w, norm_ffn_w, w_in, lambda_q1, lambda_k1, lambda_q2,
           lambda_k2, att_norm_w, rec_norm_w, lb_fwd, lb_bwd, w_up_att, w_up_rec, w_out, ffn_w1, ffn_w3,
           ffn_w2, router_w, moe_w1, moe_w3, moe_w2, final_norm_w):
    batch, t_lat, d = x.shape
    t_ctx = ctx.shape[1]
    depth = w_ada.shape[0]
    n_lat, n_ctx = batch * t_lat, batch * t_ctx
    tm = 512

    xl = x.reshape(n_lat, d)
    xc = ctx.reshape(n_ctx, d)

    pad_rows = (-(batch + 1)) % 8
    cc = jnp.concatenate([c, jnp.zeros((pad_rows, d), F32), c_ctx[None, :]], axis=0)
    mod_all = _mod_call(cc, w_ada, b_ada).reshape(depth, cc.shape[0], 6, d)

    cos, sin = _rope_tables(t_lat)
    lbs_f = _layer_lower_bounds(lb_fwd)
    lbs_b = _layer_lower_bounds(lb_bwd)

    for l in range(depth):
        last = l == depth - 1
        mod = mod_all[l]
        lam_init = 0.8 - 0.6 * math.exp(-0.3 * l)
        lam = (jnp.exp(jnp.sum(lambda_q1[l] * lambda_k1[l])) - jnp.exp(jnp.sum(lambda_q2[l] * lambda_k2[l]))
               + lam_init).reshape(1).astype(F32)
        lb = jnp.concatenate([lbs_f[l], lbs_b[l]])[None, :]
        w_l = _win_columns(w_in[l])
        nw = norm_mix_w[l][None, :]

        pl_lat = _inproj_call(xl, mod, nw, w_l, cos, sin, lb, tm=tm, rows_per_mod=t_lat, rope=True)
        pl_ctx = _inproj_call(xc, mod, nw, w_l, cos, sin, lb, tm=tm, rows_per_mod=None, rope=False)
        qk_l, vg_l, qi_l, lf_l, g_l = pl_lat
        qk_c, vg_c, qi_c, lf_c, g_c = pl_ctx

        anw = att_norm_w[l][None, :]
        ya_l = _attn_call(lam, qk_l, qk_l, vg_l, qk_c, vg_c, anw, batch=batch, tq=256,
                          lam_init=lam_init, with_lat=True)
        yr_l, yr_c = _gla_call(qi_l, lf_l, vg_l, qi_c, lf_c, vg_c, rec_norm_w[l][None, :], batch=batch)

        wua = w_up_att[l].astype(BF16)
        wur = w_up_rec[l].astype(BF16)
        wo = w_out[l].astype(BF16)
        nw2 = norm_ffn_w[l][None, :]
        moe_layer = l % 2 == 1
        j = l // 2
        assert moe_layer == last
        rw = None
        if moe_layer:
            rw32 = jnp.pad(router_w[j], ((0, 0), (0, LANES - N_EXPERTS)))
            rw_hi = rw32.astype(BF16)
            rw = jnp.stack([rw_hi, (rw32 - rw_hi.astype(F32)).astype(BF16)])
        res = _outproj_call(ya_l, yr_l, g_l, xl, mod, wua, wur, wo, nw2, rw, tm=tm, rows_per_mod=t_lat,
                            route_block=t_lat)
        xl, h2_l = res[0], res[1]
        if not last:
            ya_c = _attn_call(lam, qk_c, None, None, qk_c, vg_c, anw, batch=batch, tq=t_ctx,
                              lam_init=lam_init, with_lat=False)
            xc, h2_c = _outproj_call(ya_c, yr_c, g_c, xc, mod, wua, wur, wo, nw2, None, tm=tm, rows_per_mod=None)

        if moe_layer:
            gates, pos, cnt = res[2], res[3], res[4]
            cnt = cnt[:, 0, :N_EXPERTS].astype(jnp.int32).reshape(-1)
            y = _moe_call(cnt, h2_l, pos[:, :N_EXPERTS].T, gates[:, :N_EXPERTS].T,
                          moe_w1[j].astype(BF16), moe_w3[j].astype(BF16), moe_w2[j].astype(BF16),
                          tb=t_lat, tf=512)
            xl = _resnorm_call(xl, y, mod, final_norm_w[None, :], tm=tm, rows_per_mod=t_lat)
        else:
            w1 = _pad_ff(ffn_w1[j], 1, 2 * LANES).astype(BF16)
            w3 = _pad_ff(ffn_w3[j], 1, 2 * LANES).astype(BF16)
            w2 = _pad_ff(ffn_w2[j], 0, 2 * LANES).astype(BF16)
            tf = w1.shape[1] // 2
            xl = _ffn_call(h2_l, xl, mod, w1, w3, w2, tm=tm, tf=tf, rows_per_mod=t_lat)
            xc = _ffn_call(h2_c, xc, mod, w1, w3, w2, tm=tm, tf=tf, rows_per_mod=None)

    return xl.reshape(batch, t_lat, d)
```

```python
import functools
import math

import numpy as np
import jax
import jax.numpy as jnp
from jax import lax
from jax.experimental import pallas as pl
from jax.experimental.pallas import tpu as pltpu

F32 = jnp.float32
BF16 = jnp.bfloat16
HIGHEST = lax.Precision.HIGHEST

EPS = 1e-6
GRID_W = 64
ROPE_THETA = 10000.0
ATT_HEADS = 4
ATT_QK_DIM = 64
REC_HEADS = 4
N_EXPERTS = 8
Q_SCALE = ATT_QK_DIM ** -0.5 * math.log2(math.e)

LANES = 128
GLA_BLOCK = 256
GLA_CHUNK = 64
GLA_LEVELS = (1, 2, 4, 8, 16, 32)
VMEM_LIMIT = 56 * 1024 * 1024
MOE_VMEM_LIMIT = 60 * 1024 * 1024
MOE_ROWS = 256
ATT_SPLIT = 2

NT_DIMS = (((1,), (1,)), ((), ()))
TN_DIMS = (((0,), (0,)), ((), ()))


def _cparams(*sem):
    return pltpu.CompilerParams(dimension_semantics=sem, vmem_limit_bytes=VMEM_LIMIT)


def _silu(a):
    return a * jax.nn.sigmoid(a)


def _rms(x):
    return x * lax.rsqrt(jnp.mean(x * x, axis=-1, keepdims=True) + EPS)


def _mod_kernel(c_ref, w_ref, b_ref, o_ref):
    s = _silu(c_ref[...])
    o_ref[...] = jnp.dot(s, w_ref[...], precision=HIGHEST, preferred_element_type=F32) + b_ref[...]


def _mod_call(cc, w_ada, b_ada):
    depth, d, n = w_ada.shape
    rows = cc.shape[0]
    tn = 1536
    return pl.pallas_call(
        _mod_kernel,
        grid=(depth, n // tn),
        in_specs=[
            pl.BlockSpec((rows, d), lambda l, j: (0, 0)),
            pl.BlockSpec((None, d, tn), lambda l, j: (l, 0, j)),
            pl.BlockSpec((None, 1, tn), lambda l, j: (l, 0, j)),
        ],
        out_specs=pl.BlockSpec((None, rows, tn), lambda l, j: (l, 0, j)),
        out_shape=jax.ShapeDtypeStruct((depth, rows, n), F32),
        compiler_params=_cparams("arbitrary", "arbitrary"),
        name="mod",
    )(cc, w_ada, b_ada.reshape(depth, 1, n))


def _log_forget(z, lb):
    t = jnp.exp(-jnp.abs(z))
    num = jnp.where(z >= 0.0, 1.0 + lb * t, lb + t)
    return jnp.where(num > 0.0, jnp.log(num / (1.0 + t)), z)


def _inproj_kernel(x_ref, mod_ref, nw_ref, w_ref, cos_ref, sin_ref, lb_ref,
                   qk_ref, vg_ref, qi_ref, lf_ref, g_ref, h_scr, *, rope):
    s = pl.program_id(1)

    @pl.when(s == 0)
    def _():
        h = _rms(x_ref[...]) * nw_ref[...]
        h_scr[...] = (h * (1.0 + mod_ref[1:2, :]) + mod_ref[0:1, :]).astype(BF16)

    def proj():
        return jnp.dot(h_scr[...], w_ref[...], preferred_element_type=F32)

    half = w_ref.shape[1] // 2

    @pl.when(s == 0)
    def _():
        p = proj()
        if rope:
            cos = cos_ref[...]
            sin = sin_ref[...]
            lane = lax.broadcasted_iota(jnp.int32, cos.shape, 1)
            first = (lane % ATT_QK_DIM) < (ATT_QK_DIM // 2)
        for j in range(p.shape[1] // LANES):
            blk = p[:, j * LANES:(j + 1) * LANES]
            if rope:
                swapped = jnp.where(first,
                                    pltpu.roll(blk, LANES - ATT_QK_DIM // 2, 1),
                                    pltpu.roll(blk, ATT_QK_DIM // 2, 1))
                blk = blk * cos + swapped * sin
            if j * LANES < half:
                blk = blk * Q_SCALE
            qk_ref[:, j * LANES:(j + 1) * LANES] = blk.astype(BF16)

    @pl.when(s == 1)
    def _():
        p = proj()
        vg_ref[:, :half] = p[:, :half].astype(BF16)
        vg_ref[:, half:] = _silu(p[:, half:]).astype(BF16)

    @pl.when(s == 2)
    def _():
        p = proj()
        qi_ref[:, :half] = _silu(p[:, :half]).astype(BF16)
        qi_ref[:, half:] = p[:, half:].astype(BF16)

    @pl.when(s == 3)
    def _():
        lf_ref[...] = _log_forget(proj(), lb_ref[...])

    @pl.when(s >= 4)
    def _():
        g_ref[...] = jax.nn.sigmoid(proj()).astype(BF16)


def _inproj_call(x, mod, nw, w, cos, sin, lb, *, tm, rows_per_mod, rope):
    n, d = x.shape
    ts = 1024
    n_sec = w.shape[1] // ts
    pos_tiles = cos.shape[0] // tm

    def row(i, s):
        return (i, 0)

    def gate_idx(i, s):
        return (i, jnp.maximum(s - 4, 0))

    if rows_per_mod is None:
        mod_map = lambda i, s: (mod.shape[0] - 1, 0, 0)
    else:
        mod_map = lambda i, s: (i // (rows_per_mod // tm), 0, 0)
    outs = pl.pallas_call(
        functools.partial(_inproj_kernel, rope=rope),
        grid=(n // tm, n_sec),
        in_specs=[
            pl.BlockSpec((tm, d), row),
            pl.BlockSpec((None, 6, d), mod_map),
            pl.BlockSpec((1, d), lambda i, s: (0, 0)),
            pl.BlockSpec((d, ts), lambda i, s: (0, s)),
            pl.BlockSpec((tm, LANES), lambda i, s: (i % pos_tiles, 0)),
            pl.BlockSpec((tm, LANES), lambda i, s: (i % pos_tiles, 0)),
            pl.BlockSpec((1, ts), lambda i, s: (0, 0)),
        ],
        out_specs=[
            pl.BlockSpec((tm, ts), row),
            pl.BlockSpec((tm, ts), row),
            pl.BlockSpec((tm, ts), row),
            pl.BlockSpec((tm, ts), row),
            pl.BlockSpec((tm, ts), gate_idx),
        ],
        out_shape=[
            jax.ShapeDtypeStruct((n, ts), BF16),
            jax.ShapeDtypeStruct((n, ts), BF16),
            jax.ShapeDtypeStruct((n, ts), BF16),
            jax.ShapeDtypeStruct((n, ts), F32),
            jax.ShapeDtypeStruct((n, 2 * ts), BF16),
        ],
        scratch_shapes=[pltpu.VMEM((tm, d), BF16)],
        compiler_params=_cparams("arbitrary", "arbitrary"),
        name="inproj_rope" if rope else "inproj",
    )(x, mod, nw, w, cos, sin, lb)
    return outs


def _attn_kernel(lam_ref, q_ref, *refs, post_scale, with_lat):
    if with_lat:
        kl_ref, vl_ref, kc_ref, vc_ref, nw_ref, o_ref, vce_scr, vle_scr = refs
    else:
        kc_ref, vc_ref, nw_ref, o_ref, vce_scr = refs

    @pl.when(pl.program_id(2) == 0)
    def _():
        vce_scr[:, :LANES] = vc_ref[...]
        vce_scr[:, LANES:] = jnp.ones(vc_ref.shape, BF16)
        if with_lat:
            vle_scr[:, :LANES] = vl_ref[...]
            vle_scr[:, LANES:] = jnp.ones(vl_ref.shape, BF16)

    rows = q_ref.shape[0] // ATT_SPLIT
    for part in range(ATT_SPLIT):
        q = q_ref[part * rows:(part + 1) * rows, :]
        lane = lax.broadcasted_iota(jnp.int32, q.shape, 1)
        zero = jnp.zeros_like(q)
        qq = jnp.concatenate([jnp.where(lane < ATT_QK_DIM, q, zero), jnp.where(lane >= ATT_QK_DIM, q, zero)],
                             axis=0)
        sc = lax.dot_general(qq, kc_ref[...], NT_DIMS, preferred_element_type=F32)
        m = jnp.max(sc, axis=-1, keepdims=True)
        if with_lat:
            sl = lax.dot_general(qq, kl_ref[...], NT_DIMS, preferred_element_type=F32)
            m = jnp.maximum(m, jnp.max(sl, axis=-1, keepdims=True))
        oe = jnp.dot(jnp.exp2(sc - m).astype(BF16), vce_scr[...], preferred_element_type=F32)
        if with_lat:
            oe = oe + jnp.dot(jnp.exp2(sl - m).astype(BF16), vle_scr[...], preferred_element_type=F32)
        on = oe[:, :LANES] * (1.0 / oe[:, LANES:])
        o = on[:rows] - lam_ref[0] * on[rows:]
        o_ref[part * rows:(part + 1) * rows, :] = (_rms(o) * nw_ref[...] * post_scale).astype(BF16)


def _attn_call(lam, q_src, qk_lat, vg_lat, qk_ctx, vg_ctx, nw, *, batch, tq, lam_init, with_lat):
    n_q = q_src.shape[0]
    t_q = n_q // batch
    t_lat = qk_lat.shape[0] // batch if with_lat else 0
    t_ctx = qk_ctx.shape[0] // batch
    nq_tiles = t_q // tq
    h = ATT_HEADS
    in_specs = [
        pl.BlockSpec(memory_space=pltpu.SMEM),
        pl.BlockSpec((tq, LANES), lambda b, hh, i: (b * nq_tiles + i, hh)),
    ]
    args = [lam, q_src]
    if with_lat:
        in_specs += [
            pl.BlockSpec((t_lat, LANES), lambda b, hh, i: (b, h + hh)),
            pl.BlockSpec((t_lat, LANES), lambda b, hh, i: (b, hh)),
        ]
        args += [qk_lat, vg_lat]
    in_specs += [
        pl.BlockSpec((t_ctx, LANES), lambda b, hh, i: (b, h + hh)),
        pl.BlockSpec((t_ctx, LANES), lambda b, hh, i: (b, hh)),
        pl.BlockSpec((1, LANES), lambda b, hh, i: (0, 0)),
    ]
    args += [qk_ctx, vg_ctx, nw]
    return pl.pallas_call(
        functools.partial(_attn_kernel, post_scale=1.0 - lam_init, with_lat=with_lat),
        grid=(batch, h, nq_tiles),
        in_specs=in_specs,
        out_specs=pl.BlockSpec((tq, LANES), lambda b, hh, i: (b * nq_tiles + i, hh)),
        out_shape=jax.ShapeDtypeStruct((n_q, h * LANES), BF16),
        scratch_shapes=[pltpu.VMEM((t_ctx, 2 * LANES), BF16)]
        + ([pltpu.VMEM((t_lat, 2 * LANES), BF16)] if with_lat else []),
        compiler_params=_cparams("arbitrary", "arbitrary", "arbitrary"),
        name="attn_lat" if with_lat else "attn_ctx",
    )(*args)


def _gla_constants():
    n = GLA_BLOCK
    idx = np.arange(n)
    same_chunk = (idx[:, None] // GLA_CHUNK) == (idx[None, :] // GLA_CHUNK)
    stacks, masks = [], []
    for reverse in (False, True):
        order = (idx[None, :] >= idx[:, None]) if reverse else (idx[None, :] <= idx[:, None])
        cum = (same_chunk & order).astype(np.int32)
        groups, lvl_masks = [cum], []
        for m in GLA_LEVELS:
            ref = (idx // (2 * m)) * (2 * m) + (m if reverse else m - 1)
            groups.append(np.abs(cum - cum[ref]))
            upper = (idx % (2 * m)) >= m
            same = (idx[:, None] // (2 * m)) == (idx[None, :] // (2 * m))
            q_side, k_side = (~upper, upper) if reverse else (upper, ~upper)
            lvl_masks.append(same & q_side[:, None] & k_side[None, :])
        end = (idx // GLA_CHUNK) * GLA_CHUNK + (0 if reverse else GLA_CHUNK - 1)
        groups.append(np.abs(cum[end] - cum))
        stacks.append(np.concatenate(groups, axis=0))
        masks.append(np.stack(lvl_masks))
    return jnp.asarray(np.stack(stacks), BF16), jnp.asarray(np.stack(masks), F32)


def _gla_block(q, v, lf, st, stack, masks, *, reverse):
    n = q.shape[0]
    qf = q.astype(F32)
    k = 1.0 - jnp.exp(lf)

    hi = lf.astype(BF16)
    mid = (lf - hi.astype(F32)).astype(BF16)
    expo = jnp.dot(stack, jnp.concatenate([hi, mid], axis=1), preferred_element_type=F32)

    def exponent(group):
        blk = expo[group * n:(group + 1) * n]
        return blk[:, :LANES] + blk[:, LANES:]

    c = exponent(0)
    scores = jnp.zeros((n, n), F32)
    for li in range(len(GLA_LEVELS)):
        e = jnp.exp(exponent(1 + li))
        s_l = lax.dot_general((qf * e).astype(BF16), (k * e).astype(BF16), NT_DIMS,
                              preferred_element_type=F32)
        scores = scores + s_l * masks[li]
    o = jnp.dot(scores.astype(BF16), v, preferred_element_type=F32)
    o = o + jnp.sum(qf * k, axis=-1, keepdims=True) * v.astype(F32)

    last = 0 if reverse else GLA_CHUNK - 1
    q_in = (qf * jnp.exp(c)).astype(BF16)
    k_out = (k * jnp.exp(exponent(1 + len(GLA_LEVELS)))).astype(BF16)
    n_chunks = n // GLA_CHUNK
    outs = [None] * n_chunks
    order = range(n_chunks - 1, -1, -1) if reverse else range(n_chunks)
    for ci in order:
        lo_r, hi_r = ci * GLA_CHUNK, (ci + 1) * GLA_CHUNK
        outs[ci] = lax.dot_general(q_in[lo_r:hi_r], st.astype(BF16), NT_DIMS, preferred_element_type=F32)
        decay = jnp.exp(c[lo_r + last:lo_r + last + 1, :])
        upd = lax.dot_general(v[lo_r:hi_r], k_out[lo_r:hi_r], TN_DIMS, preferred_element_type=F32)
        st = st * decay + upd
    return o + jnp.concatenate(outs, axis=0), st


def _gla_kernel(ql_ref, il_ref, ffl_ref, fbl_ref, gl_ref, qc_ref, ic_ref, ffc_ref, fbc_ref, gc_ref,
                nw_ref, stack_ref, mask_ref, ol_ref, oc_ref, of_scr, ob_scr):
    n_blocks = ql_ref.shape[0] // GLA_BLOCK
    nw = nw_ref[...]
    zero_state = jnp.zeros((LANES, LANES), F32)

    def finish(o, g):
        return (_rms(o) * nw * g.astype(F32)).astype(BF16)

    def rows(i):
        return pl.ds(pl.multiple_of(i * GLA_BLOCK, GLA_BLOCK), GLA_BLOCK)

    o_cf, st_f = _gla_block(qc_ref[...], ic_ref[...], ffc_ref[...], zero_state,
                            stack_ref[0], mask_ref[0], reverse=False)
    o_cb, st_b = _gla_block(qc_ref[...], ic_ref[...], fbc_ref[...], zero_state,
                            stack_ref[1], mask_ref[1], reverse=True)
    oc_ref[...] = finish(o_cf + o_cb, gc_ref[...])

    def step(i, carry):
        st_f, st_b = carry
        rf = rows(i)
        rb = rows(n_blocks - 1 - i)
        o_f, st_f = _gla_block(ql_ref[rf, :], il_ref[rf, :], ffl_ref[rf, :], st_f,
                               stack_ref[0], mask_ref[0], reverse=False)
        o_b, st_b = _gla_block(ql_ref[rb, :], il_ref[rb, :], fbl_ref[rb, :], st_b,
                               stack_ref[1], mask_ref[1], reverse=True)
        of_scr[rf, :] = o_f
        ob_scr[rb, :] = o_b
        return st_f, st_b

    lax.fori_loop(0, n_blocks, step, (st_f, st_b))

    def emit(i, carry):
        r = rows(i)
        ol_ref[r, :] = finish(of_scr[r, :] + ob_scr[r, :], gl_ref[r, :])
        return carry

    lax.fori_loop(0, n_blocks, emit, 0)


def _gla_call(qi_lat, lf_lat, vg_lat, qi_ctx, lf_ctx, vg_ctx, nw, *, batch):
    t_lat = qi_lat.shape[0] // batch
    t_ctx = qi_ctx.shape[0] // batch
    assert t_ctx == GLA_BLOCK and t_lat % GLA_BLOCK == 0
    h = REC_HEADS
    stack, masks = _gla_constants()

    def col(off):
        return lambda b, hh: (b, off + hh)

    def seq(t):
        return [pl.BlockSpec((t, LANES), col(0)), pl.BlockSpec((t, LANES), col(h)),
                pl.BlockSpec((t, LANES), col(0)), pl.BlockSpec((t, LANES), col(h)),
                pl.BlockSpec((t, LANES), col(h))]

    return pl.pallas_call(
        _gla_kernel,
        grid=(batch, h),
        in_specs=seq(t_lat) + seq(t_ctx) + [
            pl.BlockSpec((1, LANES), lambda b, hh: (0, 0)),
            pl.BlockSpec(stack.shape, lambda b, hh: (0, 0, 0)),
            pl.BlockSpec(masks.shape, lambda b, hh: (0, 0, 0, 0)),
        ],
        out_specs=[pl.BlockSpec((t_lat, LANES), col(0)), pl.BlockSpec((t_ctx, LANES), col(0))],
        out_shape=[jax.ShapeDtypeStruct((batch * t_lat, h * LANES), BF16),
                   jax.ShapeDtypeStruct((batch * t_ctx, h * LANES), BF16)],
        scratch_shapes=[pltpu.VMEM((t_lat, LANES), F32), pltpu.VMEM((t_lat, LANES), F32)],
        compiler_params=_cparams("arbitrary", "arbitrary"),
        name="gla",
    )(qi_lat, qi_lat, lf_lat, lf_lat, vg_lat, qi_ctx, qi_ctx, lf_ctx, lf_ctx, vg_ctx, nw, stack, masks)


def _top2_gates(logits):
    lane = lax.broadcasted_iota(jnp.int32, logits.shape, 1).astype(F32)
    big = float(LANES)
    m1 = jnp.max(logits, axis=-1, keepdims=True)
    i1 = jnp.min(jnp.where(logits == m1, lane, big), axis=-1, keepdims=True)
    rest = jnp.where(lane == i1, -jnp.inf, logits)
    m2 = jnp.max(rest, axis=-1, keepdims=True)
    i2 = jnp.min(jnp.where(rest == m2, lane, big), axis=-1, keepdims=True)
    e = jnp.exp(m2 - m1)
    w1 = 1.0 / (1.0 + e)
    return jnp.where(lane == i1, w1, 0.0) + jnp.where(lane == i2, e * w1, 0.0)


def _outproj_kernel(ya_ref, yr_ref, g_ref, x_ref, mod_ref, wua_ref, wur_ref, wo_ref, nw2_ref, *refs,
                    router, tiles_per_block):
    if router:
        rw_ref, tri_ref, xo_ref, h2_ref, gate_ref, pos_ref, cnt_ref, carry_scr = refs
    else:
        xo_ref, h2_ref = refs
    d = x_ref.shape[1]
    ua = jnp.dot(ya_ref[...], wua_ref[...], preferred_element_type=F32)
    ur = jnp.dot(yr_ref[...], wur_ref[...], preferred_element_type=F32)
    u = g_ref[:, :d].astype(F32) * ua + g_ref[:, d:].astype(F32) * ur
    y = jnp.dot(u.astype(BF16), wo_ref[...], preferred_element_type=F32)
    xn = x_ref[...] + mod_ref[2:3, :] * y
    xo_ref[...] = xn
    h2 = (_rms(xn) * nw2_ref[...]) * (1.0 + mod_ref[4:5, :]) + mod_ref[3:4, :]
    h2_hi = h2.astype(BF16)
    h2_ref[...] = h2_hi
    if router:
        h2_lo = (h2 - h2_hi.astype(F32)).astype(BF16)
        logits = (jnp.dot(h2_hi, rw_ref[0], preferred_element_type=F32)
                  + jnp.dot(h2_lo, rw_ref[0], preferred_element_type=F32)
                  + jnp.dot(h2_hi, rw_ref[1], preferred_element_type=F32))
        lane = lax.broadcasted_iota(jnp.int32, logits.shape, 1)
        gates = _top2_gates(jnp.where(lane < N_EXPERTS, logits, -jnp.inf))
        gate_ref[...] = gates

        @pl.when(pl.program_id(0) % tiles_per_block == 0)
        def _():
            carry_scr[...] = jnp.zeros_like(carry_scr)

        sel = gates > 0.0
        sel_f = jnp.where(sel, 1.0, 0.0)
        rank = jnp.dot(tri_ref[...], sel_f.astype(BF16), preferred_element_type=F32) + carry_scr[...]
        pos_ref[...] = jnp.where(sel, rank, -1.0)
        total = carry_scr[...] + jnp.sum(sel_f, axis=0, keepdims=True)
        carry_scr[...] = total
        cnt_ref[...] = total


def _outproj_call(ya, yr, g, x, mod, wua, wur, wo, nw2, rw, *, tm, rows_per_mod, route_block=None):
    n, d = x.shape
    router = rw is not None
    row = lambda i: (i, 0)
    const = lambda i: (0, 0)
    tiles_per_block = route_block // tm if router else 1
    if rows_per_mod is None:
        mod_map = lambda i: (mod.shape[0] - 1, 0, 0)
    else:
        mod_map = lambda i: (i // (rows_per_mod // tm), 0, 0)
    in_specs = [
        pl.BlockSpec((tm, ya.shape[1]), row),
        pl.BlockSpec((tm, yr.shape[1]), row),
        pl.BlockSpec((tm, 2 * d), row),
        pl.BlockSpec((tm, d), row),
        pl.BlockSpec((None, 6, d), mod_map),
        pl.BlockSpec(wua.shape, const),
        pl.BlockSpec(wur.shape, const),
        pl.BlockSpec(wo.shape, const),
        pl.BlockSpec((1, d), const),
    ]
    args = [ya, yr, g, x, mod, wua, wur, wo, nw2]
    out_specs = [pl.BlockSpec((tm, d), row), pl.BlockSpec((tm, d), row)]
    out_shape = [jax.ShapeDtypeStruct((n, d), F32), jax.ShapeDtypeStruct((n, d), BF16)]
    scratch = []
    if router:
        tri = jnp.asarray(np.tril(np.ones((tm, tm), np.float32), -1), BF16)
        in_specs += [pl.BlockSpec(rw.shape, lambda i: (0, 0, 0)), pl.BlockSpec(tri.shape, const)]
        args += [rw, tri]
        out_specs += [pl.BlockSpec((tm, LANES), row), pl.BlockSpec((tm, LANES), row),
                      pl.BlockSpec((None, 1, LANES), lambda i: (i // tiles_per_block, 0, 0))]
        out_shape += [jax.ShapeDtypeStruct((n, LANES), F32), jax.ShapeDtypeStruct((n, LANES), F32),
                      jax.ShapeDtypeStruct((n // route_block, 1, LANES), F32)]
        scratch = [pltpu.VMEM((1, LANES), F32)]
    return pl.pallas_call(
        functools.partial(_outproj_kernel, router=router, tiles_per_block=tiles_per_block),
        grid=(n // tm,),
        in_specs=in_specs,
        out_specs=out_specs,
        out_shape=out_shape,
        scratch_shapes=scratch,
        compiler_params=_cparams("arbitrary"),
        name="outproj_router" if router else "outproj",
    )(*args)


def _swiglu_chunk(xs, w1_ref, w3_ref, w2_ref):
    a = jnp.dot(xs, w1_ref[...], preferred_element_type=F32)
    b = jnp.dot(xs, w3_ref[...], preferred_element_type=F32)
    return jnp.dot((_silu(a) * b).astype(BF16), w2_ref[...], preferred_element_type=F32)


def _ffn_kernel(h_ref, x_ref, mod_ref, w1_ref, w3_ref, w2_ref, o_ref, acc_ref):
    f = pl.program_id(1)

    @pl.when(f == 0)
    def _():
        acc_ref[...] = jnp.zeros_like(acc_ref)

    acc_ref[...] += _swiglu_chunk(h_ref[...], w1_ref, w3_ref, w2_ref)

    @pl.when(f == pl.num_programs(1) - 1)
    def _():
        o_ref[...] = x_ref[...] + mod_ref[5:6, :] * acc_ref[...]


def _ffn_call(h, x, mod, w1, w3, w2, *, tm, tf, rows_per_mod):
    n, d = x.shape
    ff = w1.shape[1]
    row = lambda i, f: (i, 0)
    if rows_per_mod is None:
        mod_map = lambda i, f: (mod.shape[0] - 1, 0, 0)
    else:
        mod_map = lambda i, f: (i // (rows_per_mod // tm), 0, 0)
    return pl.pallas_call(
        _ffn_kernel,
        grid=(n // tm, ff // tf),
        in_specs=[
            pl.BlockSpec((tm, d), row), pl.BlockSpec((tm, d), row), pl.BlockSpec((None, 6, d), mod_map),
            pl.BlockSpec((d, tf), lambda i, f: (0, f)),
            pl.BlockSpec((d, tf), lambda i, f: (0, f)),
            pl.BlockSpec((tf, d), lambda i, f: (f, 0)),
        ],
        out_specs=pl.BlockSpec((tm, d), row),
        out_shape=jax.ShapeDtypeStruct((n, d), F32),
        scratch_shapes=[pltpu.VMEM((tm, d), F32)],
        compiler_params=_cparams("arbitrary", "arbitrary"),
        name="ffn",
    )(h, x, mod, w1, w3, w2)


def _moe_kernel(cnt_ref, h_ref, pos_ref, gate_ref, w1_ref, w3_ref, w2_ref, o_ref, xg_scr, y_scr, gr_scr):
    b, e, f = pl.program_id(0), pl.program_id(1), pl.program_id(2)
    n_e, n_f = pl.num_programs(1), pl.num_programs(2)
    n_tiles = (cnt_ref[b * n_e + e] + MOE_ROWS - 1) // MOE_ROWS
    tb = h_ref.shape[0]
    pos_row = pos_ref[pl.ds(e, 1), :]

    def tile_rows(s):
        return pl.ds(pl.multiple_of(s * MOE_ROWS, MOE_ROWS), MOE_ROWS)

    def one_hot(s):
        slot = lax.broadcasted_iota(jnp.int32, (MOE_ROWS, tb), 0) + s * MOE_ROWS
        return pos_row == slot.astype(F32)

    @pl.when((e == 0) & (f == 0))
    def _():
        o_ref[...] = jnp.zeros_like(o_ref)

    @pl.when(f == 0)
    def _():
        gate_row = gate_ref[pl.ds(e, 1), :]

        def gather(s, carry):
            hit = one_hot(s)
            rows = tile_rows(s)
            xg_scr[rows, :] = jnp.dot(jnp.where(hit, 1.0, 0.0).astype(BF16), h_ref[...],
                                      preferred_element_type=F32).astype(BF16)
            gr_scr[rows, :] = jnp.sum(jnp.where(hit, gate_row, 0.0), axis=-1, keepdims=True)
            return carry

        lax.fori_loop(0, n_tiles, gather, 0)

    def expert(s, carry):
        rows = tile_rows(s)
        part = _swiglu_chunk(xg_scr[rows, :], w1_ref, w3_ref, w2_ref)

        @pl.when(f == 0)
        def _():
            y_scr[rows, :] = part

        @pl.when(f > 0)
        def _():
            y_scr[rows, :] += part

        return carry

    lax.fori_loop(0, n_tiles, expert, 0)

    @pl.when(f == n_f - 1)
    def _():
        def scatter(s, carry):
            rows = tile_rows(s)
            y = (y_scr[rows, :] * gr_scr[rows, :]).astype(BF16)
            o_ref[...] += lax.dot_general(jnp.where(one_hot(s), 1.0, 0.0).astype(BF16), y, TN_DIMS,
                                          preferred_element_type=F32)
            return carry

        lax.fori_loop(0, n_tiles, scatter, 0)


def _moe_call(cnt, h, pos_t, gate_t, w1, w3, w2, *, tb, tf):
    n, d = h.shape
    n_e, _, ff = w1.shape
    grid_spec = pltpu.PrefetchScalarGridSpec(
        num_scalar_prefetch=1,
        grid=(n // tb, n_e, ff // tf),
        in_specs=[
            pl.BlockSpec((tb, d), lambda b, e, f, c: (b, 0)),
            pl.BlockSpec((n_e, tb), lambda b, e, f, c: (0, b)),
            pl.BlockSpec((n_e, tb), lambda b, e, f, c: (0, b)),
            pl.BlockSpec((None, d, tf), lambda b, e, f, c: (e, 0, f)),
            pl.BlockSpec((None, d, tf), lambda b, e, f, c: (e, 0, f)),
            pl.BlockSpec((None, tf, d), lambda b, e, f, c: (e, f, 0)),
        ],
        out_specs=pl.BlockSpec((tb, d), lambda b, e, f, c: (b, 0)),
        scratch_shapes=[pltpu.VMEM((tb, d), BF16), pltpu.VMEM((tb, d), F32), pltpu.VMEM((tb, 1), F32)],
    )
    return pl.pallas_call(
        _moe_kernel,
        grid_spec=grid_spec,
        out_shape=jax.ShapeDtypeStruct((n, d), F32),
        compiler_params=pltpu.CompilerParams(dimension_semantics=("arbitrary",) * 3,
                                             vmem_limit_bytes=MOE_VMEM_LIMIT),
        name="moe",
    )(cnt, h, pos_t, gate_t, w1, w3, w2)


def _resnorm_kernel(x_ref, y_ref, mod_ref, fnw_ref, o_ref):
    o_ref[...] = _rms(x_ref[...] + mod_ref[5:6, :] * y_ref[...]) * fnw_ref[...]


def _resnorm_call(x, y, mod, fnw, *, tm, rows_per_mod):
    n, d = x.shape
    row = lambda i: (i, 0)
    return pl.pallas_call(
        _resnorm_kernel,
        grid=(n // tm,),
        in_specs=[pl.BlockSpec((tm, d), row), pl.BlockSpec((tm, d), row),
                  pl.BlockSpec((None, 6, d), lambda i: (i // (rows_per_mod // tm), 0, 0)),
                  pl.BlockSpec((1, d), lambda i: (0, 0))],
        out_specs=pl.BlockSpec((tm, d), row),
        out_shape=jax.ShapeDtypeStruct((n, d), F32),
        compiler_params=_cparams("arbitrary"),
        name="resnorm",
    )(x, y, mod, fnw)


def _rope_tables(t_lat):
    rows = t_lat // GRID_W
    row = jnp.repeat(jnp.arange(rows, dtype=F32), GRID_W)
    col = jnp.tile(jnp.arange(GRID_W, dtype=F32), rows)
    n_freq = ATT_QK_DIM // 4
    inv_freq = ROPE_THETA ** (-jnp.arange(n_freq, dtype=F32) / n_freq)
    ang = jnp.concatenate([row[:, None] * inv_freq, col[:, None] * inv_freq], axis=-1)
    cos, sin = jnp.cos(ang), jnp.sin(ang)
    reps = LANES // ATT_QK_DIM
    return (jnp.tile(jnp.concatenate([cos, cos], axis=-1), (1, reps)),
            jnp.tile(jnp.concatenate([-sin, sin], axis=-1), (1, reps)))


def _layer_lower_bounds(lb_param):
    cs = jnp.cumsum(jax.nn.softmax(lb_param.astype(F32), axis=0), axis=0)
    return cs - cs[0:1]


def _win_columns(w_in_l):
    c = [w_in_l[:, i * 512:(i + 1) * 512] for i in range(8)]
    return jnp.concatenate([c[0], c[1], c[2], c[7], c[3], c[4], c[5], c[6], w_in_l[:, 4096:]], axis=1).astype(BF16)


def _pad_ff(w, axis, mult):
    ff = w.shape[axis]
    pad = (-ff) % mult
    if pad == 0:
        return w
    widths = [(0, 0)] * w.ndim
    widths[axis] = (0, pad)
    return jnp.pad(w, widths)


def kernel(x, c, ctx, c_ctx, w_ada, b_ada, norm_mix_w, norm_ffn_w, w_in, lambda_q1, lambda_k1, lambda_q2,
           lambda_k2, att_norm_w, rec_norm_w, lb_fwd, lb_bwd, w_up_att, w_up_rec, w_out, ffn_w1, ffn_w3,
           ffn_w2, router_w, moe_w1, moe_w3, moe_w2, final_norm_w):
    batch, t_lat, d = x.shape
    t_ctx = ctx.shape[1]
    depth = w_ada.shape[0]
    n_lat, n_ctx = batch * t_lat, batch * t_ctx
    tm = 512
    tm_in = min(1024, t_lat)
    tq = min(512, t_lat)

    xl = x.reshape(n_lat, d)
    xc = ctx.reshape(n_ctx, d)

    pad_rows = (-(batch + 1)) % 8
    cc = jnp.concatenate([c, jnp.zeros((pad_rows, d), F32), c_ctx[None, :]], axis=0)
    mod_all = _mod_call(cc, w_ada, b_ada).reshape(depth, cc.shape[0], 6, d)

    cos, sin = _rope_tables(t_lat)
    lbs_f = _layer_lower_bounds(lb_fwd)
    lbs_b = _layer_lower_bounds(lb_bwd)

    for l in range(depth):
        last = l == depth - 1
        mod = mod_all[l]
        lam_init = 0.8 - 0.6 * math.exp(-0.3 * l)
        lam = (jnp.exp(jnp.sum(lambda_q1[l] * lambda_k1[l])) - jnp.exp(jnp.sum(lambda_q2[l] * lambda_k2[l]))
               + lam_init).reshape(1).astype(F32)
        lb = jnp.concatenate([lbs_f[l], lbs_b[l]])[None, :]
        w_l = _win_columns(w_in[l])
        nw = norm_mix_w[l][None, :]

        pl_lat = _inproj_call(xl, mod, nw, w_l, cos, sin, lb, tm=tm_in, rows_per_mod=t_lat, rope=True)
        pl_ctx = _inproj_call(xc, mod, nw, w_l, cos, sin, lb, tm=min(tm_in, n_ctx), rows_per_mod=None,
                              rope=False)
        qk_l, vg_l, qi_l, lf_l, g_l = pl_lat
        qk_c, vg_c, qi_c, lf_c, g_c = pl_ctx

        anw = att_norm_w[l][None, :]
        ya_l = _attn_call(lam, qk_l, qk_l, vg_l, qk_c, vg_c, anw, batch=batch, tq=tq,
                          lam_init=lam_init, with_lat=True)
        yr_l, yr_c = _gla_call(qi_l, lf_l, vg_l, qi_c, lf_c, vg_c, rec_norm_w[l][None, :], batch=batch)

        wua = w_up_att[l].astype(BF16)
        wur = w_up_rec[l].astype(BF16)
        wo = w_out[l].astype(BF16)
        nw2 = norm_ffn_w[l][None, :]
        moe_layer = l % 2 == 1
        j = l // 2
        assert moe_layer == last
        rw = None
        if moe_layer:
            rw32 = jnp.pad(router_w[j], ((0, 0), (0, LANES - N_EXPERTS)))
            rw_hi = rw32.astype(BF16)
            rw = jnp.stack([rw_hi, (rw32 - rw_hi.astype(F32)).astype(BF16)])
        res = _outproj_call(ya_l, yr_l, g_l, xl, mod, wua, wur, wo, nw2, rw, tm=tm, rows_per_mod=t_lat,
                            route_block=t_lat)
        xl, h2_l = res[0], res[1]
        if not last:
            ya_c = _attn_call(lam, qk_c, None, None, qk_c, vg_c, anw, batch=batch, tq=t_ctx,
                              lam_init=lam_init, with_lat=False)
            xc, h2_c = _outproj_call(ya_c, yr_c, g_c, xc, mod, wua, wur, wo, nw2, None, tm=tm, rows_per_mod=None)

        if moe_layer:
            gates, pos, cnt = res[2], res[3], res[4]
            cnt = cnt[:, 0, :N_EXPERTS].astype(jnp.int32).reshape(-1)
            y = _moe_call(cnt, h2_l, pos[:, :N_EXPERTS].T, gates[:, :N_EXPERTS].T,
                          moe_w1[j].astype(BF16), moe_w3[j].astype(BF16), moe_w2[j].astype(BF16),
                          tb=t_lat, tf=512)
            xl = _resnorm_call(xl, y, mod, final_norm_w[None, :], tm=tm, rows_per_mod=t_lat)
        else:
            w1 = _pad_ff(ffn_w1[j], 1, 2 * LANES).astype(BF16)
            w3 = _pad_ff(ffn_w3[j], 1, 2 * LANES).astype(BF16)
            w2 = _pad_ff(ffn_w2[j], 0, 2 * LANES).astype(BF16)
            tf = w1.shape[1] // 2
            xl = _ffn_call(h2_l, xl, mod, w1, w3, w2, tm=tm, tf=tf, rows_per_mod=t_lat)
            xc = _ffn_call(h2_c, xc, mod, w1, w3, w2, tm=tm, tf=tf, rows_per_mod=None)

    return xl.reshape(batch, t_lat, d)
```

```python
import functools
import math

import numpy as np
import jax
import jax.numpy as jnp
from jax import lax
from jax.experimental import pallas as pl
from jax.experimental.pallas import tpu as pltpu

F32 = jnp.float32
BF16 = jnp.bfloat16
HIGHEST = lax.Precision.HIGHEST

EPS = 1e-6
GRID_W = 64
ROPE_THETA = 10000.0
ATT_HEADS = 4
ATT_QK_DIM = 64
REC_HEADS = 4
N_EXPERTS = 8
Q_SCALE = ATT_QK_DIM ** -0.5 * math.log2(math.e)

LANES = 128
GLA_BLOCK = 256
GLA_CHUNK = 256
GLA_LEVELS = tuple(2 ** i for i in range(GLA_CHUNK.bit_length() - 1))
VMEM_LIMIT = 56 * 1024 * 1024
MOE_VMEM_LIMIT = 60 * 1024 * 1024
MOE_ROWS = 256
ATT_SPLIT = 2

NT_DIMS = (((1,), (1,)), ((), ()))
TN_DIMS = (((0,), (0,)), ((), ()))


def _cparams(*sem):
    return pltpu.CompilerParams(dimension_semantics=sem, vmem_limit_bytes=VMEM_LIMIT)


def _silu(a):
    return a * jax.nn.sigmoid(a)


def _rms(x):
    return x * lax.rsqrt(jnp.mean(x * x, axis=-1, keepdims=True) + EPS)


def _mod_kernel(c_ref, w_ref, b_ref, o_ref):
    s = _silu(c_ref[...])
    o_ref[...] = jnp.dot(s, w_ref[...], precision=HIGHEST, preferred_element_type=F32) + b_ref[...]


def _mod_call(cc, w_ada, b_ada):
    depth, d, n = w_ada.shape
    rows = cc.shape[0]
    tn = 1536
    return pl.pallas_call(
        _mod_kernel,
        grid=(depth, n // tn),
        in_specs=[
            pl.BlockSpec((rows, d), lambda l, j: (0, 0)),
            pl.BlockSpec((None, d, tn), lambda l, j: (l, 0, j)),
            pl.BlockSpec((None, 1, tn), lambda l, j: (l, 0, j)),
        ],
        out_specs=pl.BlockSpec((None, rows, tn), lambda l, j: (l, 0, j)),
        out_shape=jax.ShapeDtypeStruct((depth, rows, n), F32),
        compiler_params=_cparams("arbitrary", "arbitrary"),
        name="mod",
    )(cc, w_ada, b_ada.reshape(depth, 1, n))


def _log_forget(z, lb):
    t = jnp.exp(-jnp.abs(z))
    num = jnp.where(z >= 0.0, 1.0 + lb * t, lb + t)
    return jnp.where(num > 0.0, jnp.log(num / (1.0 + t)), z)


def _inproj_kernel(x_ref, mod_ref, nw_ref, w_ref, cos_ref, sin_ref, lb_ref,
                   qk_ref, vg_ref, qi_ref, lf_ref, g_ref, h_scr, *, rope):
    s = pl.program_id(1)

    @pl.when(s == 0)
    def _():
        h = _rms(x_ref[...]) * nw_ref[...]
        h_scr[...] = (h * (1.0 + mod_ref[1:2, :]) + mod_ref[0:1, :]).astype(BF16)

    def proj():
        return jnp.dot(h_scr[...], w_ref[...], preferred_element_type=F32)

    half = w_ref.shape[1] // 2

    @pl.when(s == 0)
    def _():
        p = proj()
        if rope:
            cos = cos_ref[...]
            sin = sin_ref[...]
            lane = lax.broadcasted_iota(jnp.int32, cos.shape, 1)
            first = (lane % ATT_QK_DIM) < (ATT_QK_DIM // 2)
        for j in range(p.shape[1] // LANES):
            blk = p[:, j * LANES:(j + 1) * LANES]
            if rope:
                swapped = jnp.where(first,
                                    pltpu.roll(blk, LANES - ATT_QK_DIM // 2, 1),
                                    pltpu.roll(blk, ATT_QK_DIM // 2, 1))
                blk = blk * cos + swapped * sin
            if j * LANES < half:
                blk = blk * Q_SCALE
            qk_ref[:, j * LANES:(j + 1) * LANES] = blk.astype(BF16)

    @pl.when(s == 1)
    def _():
        p = proj()
        vg_ref[:, :half] = p[:, :half].astype(BF16)
        vg_ref[:, half:] = _silu(p[:, half:]).astype(BF16)

    @pl.when(s == 2)
    def _():
        p = proj()
        qi_ref[:, :half] = _silu(p[:, :half]).astype(BF16)
        qi_ref[:, half:] = p[:, half:].astype(BF16)

    @pl.when(s == 3)
    def _():
        lf_ref[...] = _log_forget(proj(), lb_ref[...])

    @pl.when(s >= 4)
    def _():
        g_ref[...] = jax.nn.sigmoid(proj()).astype(BF16)


def _inproj_call(x, mod, nw, w, cos, sin, lb, *, tm, rows_per_mod, rope):
    n, d = x.shape
    ts = 1024
    n_sec = w.shape[1] // ts
    pos_tiles = cos.shape[0] // tm

    def row(i, s):
        return (i, 0)

    def gate_idx(i, s):
        return (i, jnp.maximum(s - 4, 0))

    if rows_per_mod is None:
        mod_map = lambda i, s: (mod.shape[0] - 1, 0, 0)
    else:
        mod_map = lambda i, s: (i // (rows_per_mod // tm), 0, 0)
    outs = pl.pallas_call(
        functools.partial(_inproj_kernel, rope=rope),
        grid=(n // tm, n_sec),
        in_specs=[
            pl.BlockSpec((tm, d), row),
            pl.BlockSpec((None, 6, d), mod_map),
            pl.BlockSpec((1, d), lambda i, s: (0, 0)),
            pl.BlockSpec((d, ts), lambda i, s: (0, s)),
            pl.BlockSpec((tm, LANES), lambda i, s: (i % pos_tiles, 0)),
            pl.BlockSpec((tm, LANES), lambda i, s: (i % pos_tiles, 0)),
            pl.BlockSpec((1, ts), lambda i, s: (0, 0)),
        ],
        out_specs=[
            pl.BlockSpec((tm, ts), row),
            pl.BlockSpec((tm, ts), row),
            pl.BlockSpec((tm, ts), row),
            pl.BlockSpec((tm, ts), row),
            pl.BlockSpec((tm, ts), gate_idx),
        ],
        out_shape=[
            jax.ShapeDtypeStruct((n, ts), BF16),
            jax.ShapeDtypeStruct((n, ts), BF16),
            jax.ShapeDtypeStruct((n, ts), BF16),
            jax.ShapeDtypeStruct((n, ts), F32),
            jax.ShapeDtypeStruct((n, 2 * ts), BF16),
        ],
        scratch_shapes=[pltpu.VMEM((tm, d), BF16)],
        compiler_params=_cparams("arbitrary", "arbitrary"),
        name="inproj_rope" if rope else "inproj",
    )(x, mod, nw, w, cos, sin, lb)
    return outs


def _attn_kernel(lam_ref, q_ref, *refs, post_scale, with_lat):
    if with_lat:
        kl_ref, vl_ref, kc_ref, vc_ref, nw_ref, o_ref, vce_scr, vle_scr = refs
    else:
        kc_ref, vc_ref, nw_ref, o_ref, vce_scr = refs

    @pl.when(pl.program_id(2) == 0)
    def _():
        vce_scr[:, :LANES] = vc_ref[...]
        vce_scr[:, LANES:] = jnp.ones(vc_ref.shape, BF16)
        if with_lat:
            vle_scr[:, :LANES] = vl_ref[...]
            vle_scr[:, LANES:] = jnp.ones(vl_ref.shape, BF16)

    rows = q_ref.shape[0] // ATT_SPLIT
    for part in range(ATT_SPLIT):
        q = q_ref[part * rows:(part + 1) * rows, :]
        lane = lax.broadcasted_iota(jnp.int32, q.shape, 1)
        zero = jnp.zeros_like(q)
        qq = jnp.concatenate([jnp.where(lane < ATT_QK_DIM, q, zero), jnp.where(lane >= ATT_QK_DIM, q, zero)],
                             axis=0)
        sc = lax.dot_general(qq, kc_ref[...], NT_DIMS, preferred_element_type=F32)
        m = jnp.max(sc, axis=-1, keepdims=True)
        if with_lat:
            sl = lax.dot_general(qq, kl_ref[...], NT_DIMS, preferred_element_type=F32)
            m = jnp.maximum(m, jnp.max(sl, axis=-1, keepdims=True))
        oe = jnp.dot(jnp.exp2(sc - m).astype(BF16), vce_scr[...], preferred_element_type=F32)
        if with_lat:
            oe = oe + jnp.dot(jnp.exp2(sl - m).astype(BF16), vle_scr[...], preferred_element_type=F32)
        on = oe[:, :LANES] * (1.0 / oe[:, LANES:])
        o = on[:rows] - lam_ref[0] * on[rows:]
        o_ref[part * rows:(part + 1) * rows, :] = (_rms(o) * nw_ref[...] * post_scale).astype(BF16)


def _attn_call(lam, q_src, qk_lat, vg_lat, qk_ctx, vg_ctx, nw, *, batch, tq, lam_init, with_lat):
    n_q = q_src.shape[0]
    t_q = n_q // batch
    t_lat = qk_lat.shape[0] // batch if with_lat else 0
    t_ctx = qk_ctx.shape[0] // batch
    nq_tiles = t_q // tq
    h = ATT_HEADS
    in_specs = [
        pl.BlockSpec(memory_space=pltpu.SMEM),
        pl.BlockSpec((tq, LANES), lambda b, hh, i: (b * nq_tiles + i, hh)),
    ]
    args = [lam, q_src]
    if with_lat:
        in_specs += [
            pl.BlockSpec((t_lat, LANES), lambda b, hh, i: (b, h + hh)),
            pl.BlockSpec((t_lat, LANES), lambda b, hh, i: (b, hh)),
        ]
        args += [qk_lat, vg_lat]
    in_specs += [
        pl.BlockSpec((t_ctx, LANES), lambda b, hh, i: (b, h + hh)),
        pl.BlockSpec((t_ctx, LANES), lambda b, hh, i: (b, hh)),
        pl.BlockSpec((1, LANES), lambda b, hh, i: (0, 0)),
    ]
    args += [qk_ctx, vg_ctx, nw]
    return pl.pallas_call(
        functools.partial(_attn_kernel, post_scale=1.0 - lam_init, with_lat=with_lat),
        grid=(batch, h, nq_tiles),
        in_specs=in_specs,
        out_specs=pl.BlockSpec((tq, LANES), lambda b, hh, i: (b * nq_tiles + i, hh)),
        out_shape=jax.ShapeDtypeStruct((n_q, h * LANES), BF16),
        scratch_shapes=[pltpu.VMEM((t_ctx, 2 * LANES), BF16)]
        + ([pltpu.VMEM((t_lat, 2 * LANES), BF16)] if with_lat else []),
        compiler_params=_cparams("arbitrary", "arbitrary", "arbitrary"),
        name="attn_lat" if with_lat else "attn_ctx",
    )(*args)


def _gla_constants():
    n = GLA_BLOCK
    idx = np.arange(n)
    same_chunk = (idx[:, None] // GLA_CHUNK) == (idx[None, :] // GLA_CHUNK)
    stacks, masks = [], []
    for reverse in (False, True):
        order = (idx[None, :] >= idx[:, None]) if reverse else (idx[None, :] <= idx[:, None])
        cum = (same_chunk & order).astype(np.int32)
        groups, lvl_masks = [cum], []
        for m in GLA_LEVELS:
            ref = (idx // (2 * m)) * (2 * m) + (m if reverse else m - 1)
            groups.append(np.abs(cum - cum[ref]))
            upper = (idx % (2 * m)) >= m
            same = (idx[:, None] // (2 * m)) == (idx[None, :] // (2 * m))
            q_side, k_side = (~upper, upper) if reverse else (upper, ~upper)
            lvl_masks.append(same & q_side[:, None] & k_side[None, :])
        end = (idx // GLA_CHUNK) * GLA_CHUNK + (0 if reverse else GLA_CHUNK - 1)
        groups.append(np.abs(cum[end] - cum))
        stacks.append(np.concatenate(groups, axis=0))
        masks.append(np.stack(lvl_masks))
    return jnp.asarray(np.stack(stacks), BF16), jnp.asarray(np.stack(masks), F32)


def _gla_prepare(q, v, lf, stack, *, reverse):
    n = q.shape[0]
    qf = q.astype(F32)
    k = 1.0 - jnp.exp(lf)

    hi = lf.astype(BF16)
    mid = (lf - hi.astype(F32)).astype(BF16)
    expo = jnp.dot(stack, jnp.concatenate([hi, mid], axis=1), preferred_element_type=F32)

    def exponent(group):
        blk = expo[group * n:(group + 1) * n]
        return blk[:, :LANES] + blk[:, LANES:]

    scale = [jnp.exp(exponent(1 + li)) for li in range(len(GLA_LEVELS))]
    qs = jnp.concatenate([(qf * e).astype(BF16) for e in scale], axis=1)
    ks = jnp.concatenate([(k * e).astype(BF16) for e in scale], axis=1)
    c = exponent(0)
    io = jnp.concatenate([(qf * jnp.exp(c)).astype(BF16),
                          (k * jnp.exp(exponent(1 + len(GLA_LEVELS)))).astype(BF16)], axis=1)
    diag = jnp.sum(qf * k, axis=-1, keepdims=True) * v.astype(F32)
    last = 0 if reverse else GLA_CHUNK - 1
    n_chunks = n // GLA_CHUNK
    decay = jnp.concatenate([jnp.exp(c[ci * GLA_CHUNK + last:ci * GLA_CHUNK + last + 1, :])
                             for ci in range(n_chunks)] + [jnp.ones((8 - n_chunks, LANES), F32)], axis=0)
    return qs, ks, io, diag, decay


def _gla_scan(qs, ks, io, diag, decay, v, st, masks, *, reverse):
    n = v.shape[0]
    scores = jnp.zeros((n, n), F32)
    for li in range(len(GLA_LEVELS)):
        cols = slice(li * LANES, (li + 1) * LANES)
        s_l = lax.dot_general(qs[:, cols], ks[:, cols], NT_DIMS, preferred_element_type=F32)
        scores = scores + s_l * masks[li]
    o = jnp.dot(scores.astype(BF16), v, preferred_element_type=F32) + diag

    n_chunks = n // GLA_CHUNK
    outs = [None] * n_chunks
    order = range(n_chunks - 1, -1, -1) if reverse else range(n_chunks)
    for ci in order:
        r = slice(ci * GLA_CHUNK, (ci + 1) * GLA_CHUNK)
        outs[ci] = lax.dot_general(io[r, :LANES], st.astype(BF16), NT_DIMS, preferred_element_type=F32)
        upd = lax.dot_general(v[r], io[r, LANES:], TN_DIMS, preferred_element_type=F32)
        st = st * decay[ci:ci + 1, :] + upd
    return o + jnp.concatenate(outs, axis=0), st


def _gla_kernel(ql_ref, il_ref, ffl_ref, fbl_ref, gl_ref, qc_ref, ic_ref, ffc_ref, fbc_ref, gc_ref,
                nw_ref, stack_ref, mask_ref, ol_ref, oc_ref,
                qs_scr, ks_scr, io_scr, dg_scr, dc_scr, of_scr, ob_scr):
    n_blocks = ql_ref.shape[0] // GLA_BLOCK
    nw = nw_ref[...]

    def finish(o, g):
        return (_rms(o) * nw * g.astype(F32)).astype(BF16)

    def rows(i):
        return pl.ds(pl.multiple_of(i * GLA_BLOCK, GLA_BLOCK), GLA_BLOCK)

    def prepare(slot, q, v, lf_f, lf_b):
        for d, lf in enumerate((lf_f, lf_b)):
            qs, ks, io, diag, decay = _gla_prepare(q, v, lf, stack_ref[d], reverse=bool(d))
            qs_scr[d, slot] = qs
            ks_scr[d, slot] = ks
            io_scr[d, slot] = io
            dg_scr[d, slot] = diag
            dc_scr[d, slot] = decay

    def scan(d, slot, v, st):
        return _gla_scan(qs_scr[d, slot], ks_scr[d, slot], io_scr[d, slot], dg_scr[d, slot], dc_scr[d, slot],
                         v, st, mask_ref[d], reverse=bool(d))

    prepare(0, qc_ref[...], ic_ref[...], ffc_ref[...], fbc_ref[...])

    def prep(i, carry):
        r = rows(i)
        prepare(i + 1, ql_ref[r, :], il_ref[r, :], ffl_ref[r, :], fbl_ref[r, :])
        return carry

    lax.fori_loop(0, n_blocks, prep, 0)

    zero_state = jnp.zeros((LANES, LANES), F32)
    o_cf, st_f = scan(0, 0, ic_ref[...], zero_state)
    o_cb, st_b = scan(1, 0, ic_ref[...], zero_state)
    oc_ref[...] = finish(o_cf + o_cb, gc_ref[...])

    def step(i, carry):
        st_f, st_b = carry
        j = n_blocks - 1 - i
        o_f, st_f = scan(0, i + 1, il_ref[rows(i), :], st_f)
        o_b, st_b = scan(1, j + 1, il_ref[rows(j), :], st_b)
        of_scr[rows(i), :] = o_f
        ob_scr[rows(j), :] = o_b
        return st_f, st_b

    lax.fori_loop(0, n_blocks, step, (st_f, st_b))

    def emit(i, carry):
        r = rows(i)
        ol_ref[r, :] = finish(of_scr[r, :] + ob_scr[r, :], gl_ref[r, :])
        return carry

    lax.fori_loop(0, n_blocks, emit, 0)


def _gla_call(qi_lat, lf_lat, vg_lat, qi_ctx, lf_ctx, vg_ctx, nw, *, batch):
    t_lat = qi_lat.shape[0] // batch
    t_ctx = qi_ctx.shape[0] // batch
    assert t_ctx == GLA_BLOCK and t_lat % GLA_BLOCK == 0
    h = REC_HEADS
    stack, masks = _gla_constants()
    n_slots = 1 + t_lat // GLA_BLOCK
    n_lvl = len(GLA_LEVELS)

    def col(off):
        return lambda b, hh: (b, off + hh)

    def seq(t):
        return [pl.BlockSpec((t, LANES), col(0)), pl.BlockSpec((t, LANES), col(h)),
                pl.BlockSpec((t, LANES), col(0)), pl.BlockSpec((t, LANES), col(h)),
                pl.BlockSpec((t, LANES), col(h))]

    return pl.pallas_call(
        _gla_kernel,
        grid=(batch, h),
        in_specs=seq(t_lat) + seq(t_ctx) + [
            pl.BlockSpec((1, LANES), lambda b, hh: (0, 0)),
            pl.BlockSpec(stack.shape, lambda b, hh: (0, 0, 0)),
            pl.BlockSpec(masks.shape, lambda b, hh: (0, 0, 0, 0)),
        ],
        out_specs=[pl.BlockSpec((t_lat, LANES), col(0)), pl.BlockSpec((t_ctx, LANES), col(0))],
        out_shape=[jax.ShapeDtypeStruct((batch * t_lat, h * LANES), BF16),
                   jax.ShapeDtypeStruct((batch * t_ctx, h * LANES), BF16)],
        scratch_shapes=[
            pltpu.VMEM((2, n_slots, GLA_BLOCK, n_lvl * LANES), BF16),
            pltpu.VMEM((2, n_slots, GLA_BLOCK, n_lvl * LANES), BF16),
            pltpu.VMEM((2, n_slots, GLA_BLOCK, 2 * LANES), BF16),
            pltpu.VMEM((2, n_slots, GLA_BLOCK, LANES), F32),
            pltpu.VMEM((2, n_slots, 8, LANES), F32),
            pltpu.VMEM((t_lat, LANES), F32), pltpu.VMEM((t_lat, LANES), F32),
        ],
        compiler_params=_cparams("arbitrary", "arbitrary"),
        name="gla",
    )(qi_lat, qi_lat, lf_lat, lf_lat, vg_lat, qi_ctx, qi_ctx, lf_ctx, lf_ctx, vg_ctx, nw, stack, masks)


def _top2_gates(logits):
    lane = lax.broadcasted_iota(jnp.int32, logits.shape, 1).astype(F32)
    big = float(LANES)
    m1 = jnp.max(logits, axis=-1, keepdims=True)
    i1 = jnp.min(jnp.where(logits == m1, lane, big), axis=-1, keepdims=True)
    rest = jnp.where(lane == i1, -jnp.inf, logits)
    m2 = jnp.max(rest, axis=-1, keepdims=True)
    i2 = jnp.min(jnp.where(rest == m2, lane, big), axis=-1, keepdims=True)
    e = jnp.exp(m2 - m1)
    w1 = 1.0 / (1.0 + e)
    return jnp.where(lane == i1, w1, 0.0) + jnp.where(lane == i2, e * w1, 0.0)


def _outproj_kernel(ya_ref, yr_ref, g_ref, x_ref, mod_ref, wua_ref, wur_ref, wo_ref, nw2_ref, *refs,
                    router, tiles_per_block):
    if router:
        rw_ref, tri_ref, xo_ref, h2_ref, gate_ref, pos_ref, cnt_ref, carry_scr = refs
    else:
        xo_ref, h2_ref = refs
    d = x_ref.shape[1]
    ua = jnp.dot(ya_ref[...], wua_ref[...], preferred_element_type=F32)
    ur = jnp.dot(yr_ref[...], wur_ref[...], preferred_element_type=F32)
    u = g_ref[:, :d].astype(F32) * ua + g_ref[:, d:].astype(F32) * ur
    y = jnp.dot(u.astype(BF16), wo_ref[...], preferred_element_type=F32)
    xn = x_ref[...] + mod_ref[2:3, :] * y
    xo_ref[...] = xn
    h2 = (_rms(xn) * nw2_ref[...]) * (1.0 + mod_ref[4:5, :]) + mod_ref[3:4, :]
    h2_hi = h2.astype(BF16)
    h2_ref[...] = h2_hi
    if router:
        h2_lo = (h2 - h2_hi.astype(F32)).astype(BF16)
        logits = (jnp.dot(h2_hi, rw_ref[0], preferred_element_type=F32)
                  + jnp.dot(h2_lo, rw_ref[0], preferred_element_type=F32)
                  + jnp.dot(h2_hi, rw_ref[1], preferred_element_type=F32))
        lane = lax.broadcasted_iota(jnp.int32, logits.shape, 1)
        gates = _top2_gates(jnp.where(lane < N_EXPERTS, logits, -jnp.inf))
        gate_ref[...] = gates

        @pl.when(pl.program_id(0) % tiles_per_block == 0)
        def _():
            carry_scr[...] = jnp.zeros_like(carry_scr)

        sel = gates > 0.0
        sel_f = jnp.where(sel, 1.0, 0.0)
        rank = jnp.dot(tri_ref[...], sel_f.astype(BF16), preferred_element_type=F32) + carry_scr[...]
        pos_ref[...] = jnp.where(sel, rank, -1.0)
        total = carry_scr[...] + jnp.sum(sel_f, axis=0, keepdims=True)
        carry_scr[...] = total
        cnt_ref[...] = total


def _outproj_call(ya, yr, g, x, mod, wua, wur, wo, nw2, rw, *, tm, rows_per_mod, route_block=None):
    n, d = x.shape
    router = rw is not None
    row = lambda i: (i, 0)
    const = lambda i: (0, 0)
    tiles_per_block = route_block // tm if router else 1
    if rows_per_mod is None:
        mod_map = lambda i: (mod.shape[0] - 1, 0, 0)
    else:
        mod_map = lambda i: (i // (rows_per_mod // tm), 0, 0)
    in_specs = [
        pl.BlockSpec((tm, ya.shape[1]), row),
        pl.BlockSpec((tm, yr.shape[1]), row),
        pl.BlockSpec((tm, 2 * d), row),
        pl.BlockSpec((tm, d), row),
        pl.BlockSpec((None, 6, d), mod_map),
        pl.BlockSpec(wua.shape, const),
        pl.BlockSpec(wur.shape, const),
        pl.BlockSpec(wo.shape, const),
        pl.BlockSpec((1, d), const),
    ]
    args = [ya, yr, g, x, mod, wua, wur, wo, nw2]
    out_specs = [pl.BlockSpec((tm, d), row), pl.BlockSpec((tm, d), row)]
    out_shape = [jax.ShapeDtypeStruct((n, d), F32), jax.ShapeDtypeStruct((n, d), BF16)]
    scratch = []
    if router:
        tri = jnp.asarray(np.tril(np.ones((tm, tm), np.float32), -1), BF16)
        in_specs += [pl.BlockSpec(rw.shape, lambda i: (0, 0, 0)), pl.BlockSpec(tri.shape, const)]
        args += [rw, tri]
        out_specs += [pl.BlockSpec((tm, LANES), row), pl.BlockSpec((tm, LANES), row),
                      pl.BlockSpec((None, 1, LANES), lambda i: (i // tiles_per_block, 0, 0))]
        out_shape += [jax.ShapeDtypeStruct((n, LANES), F32), jax.ShapeDtypeStruct((n, LANES), F32),
                      jax.ShapeDtypeStruct((n // route_block, 1, LANES), F32)]
        scratch = [pltpu.VMEM((1, LANES), F32)]
    return pl.pallas_call(
        functools.partial(_outproj_kernel, router=router, tiles_per_block=tiles_per_block),
        grid=(n // tm,),
        in_specs=in_specs,
        out_specs=out_specs,
        out_shape=out_shape,
        scratch_shapes=scratch,
        compiler_params=_cparams("arbitrary"),
        name="outproj_router" if router else "outproj",
    )(*args)


def _swiglu_chunk(xs, w1_ref, w3_ref, w2_ref):
    a = jnp.dot(xs, w1_ref[...], preferred_element_type=F32)
    b = jnp.dot(xs, w3_ref[...], preferred_element_type=F32)
    return jnp.dot((_silu(a) * b).astype(BF16), w2_ref[...], preferred_element_type=F32)


def _ffn_kernel(h_ref, x_ref, mod_ref, w1_ref, w3_ref, w2_ref, o_ref, acc_ref):
    f = pl.program_id(1)

    @pl.when(f == 0)
    def _():
        acc_ref[...] = jnp.zeros_like(acc_ref)

    acc_ref[...] += _swiglu_chunk(h_ref[...], w1_ref, w3_ref, w2_ref)

    @pl.when(f == pl.num_programs(1) - 1)
    def _():
        o_ref[...] = x_ref[...] + mod_ref[5:6, :] * acc_ref[...]


def _ffn_call(h, x, mod, w1, w3, w2, *, tm, tf, rows_per_mod):
    n, d = x.shape
    ff = w1.shape[1]
    row = lambda i, f: (i, 0)
    if rows_per_mod is None:
        mod_map = lambda i, f: (mod.shape[0] - 1, 0, 0)
    else:
        mod_map = lambda i, f: (i // (rows_per_mod // tm), 0, 0)
    return pl.pallas_call(
        _ffn_kernel,
        grid=(n // tm, ff // tf),
        in_specs=[
            pl.BlockSpec((tm, d), row), pl.BlockSpec((tm, d), row), pl.BlockSpec((None, 6, d), mod_map),
            pl.BlockSpec((d, tf), lambda i, f: (0, f)),
            pl.BlockSpec((d, tf), lambda i, f: (0, f)),
            pl.BlockSpec((tf, d), lambda i, f: (f, 0)),
        ],
        out_specs=pl.BlockSpec((tm, d), row),
        out_shape=jax.ShapeDtypeStruct((n, d), F32),
        scratch_shapes=[pltpu.VMEM((tm, d), F32)],
        compiler_params=_cparams("arbitrary", "arbitrary"),
        name="ffn",
    )(h, x, mod, w1, w3, w2)


def _moe_kernel(cnt_ref, h_ref, pos_ref, gate_ref, w1_ref, w3_ref, w2_ref, o_ref, xg_scr, y_scr, gr_scr):
    b, e, f = pl.program_id(0), pl.program_id(1), pl.program_id(2)
    n_e, n_f = pl.num_programs(1), pl.num_programs(2)
    tb = h_ref.shape[0]
    pos_row = pos_ref[pl.ds(e, 1), :]
    half = MOE_ROWS // 2
    n_half = (cnt_ref[b * n_e + e] + half - 1) // half
    n_full = n_half // 2
    has_half = n_half % 2 == 1
    tail = pl.multiple_of(n_full * MOE_ROWS, MOE_ROWS)

    def one_hot(start, r):
        slot = lax.broadcasted_iota(jnp.int32, (r, tb), 0) + start
        return pos_row == slot.astype(F32)

    def for_tiles(fn, rows_per_step):
        k = rows_per_step // MOE_ROWS
        n_steps = n_full // k

        def body(s, carry):
            fn(pl.multiple_of(s * rows_per_step, rows_per_step), rows_per_step)
            return carry

        lax.fori_loop(0, n_steps, body, 0)
        if k > 1:
            @pl.when(n_full % k == 1)
            def _():
                fn(pl.multiple_of(n_steps * rows_per_step, rows_per_step), MOE_ROWS)

        @pl.when(has_half)
        def _():
            fn(tail, half)

    @pl.when((e == 0) & (f == 0))
    def _():
        o_ref[...] = jnp.zeros_like(o_ref)

    @pl.when(f == 0)
    def _():
        gate_row = gate_ref[pl.ds(e, 1), :]

        def gather(start, r):
            hit = one_hot(start, r)
            rows = pl.ds(start, r)
            xg_scr[rows, :] = jnp.dot(jnp.where(hit, 1.0, 0.0).astype(BF16), h_ref[...],
                                      preferred_element_type=F32).astype(BF16)
            gr_scr[rows, :] = jnp.sum(jnp.where(hit, gate_row, 0.0), axis=-1, keepdims=True)
            y_scr[rows, :] = jnp.zeros((r, y_scr.shape[1]), F32)

        for_tiles(gather, MOE_ROWS)

    def expert(start, r):
        rows = pl.ds(start, r)
        y_scr[rows, :] += _swiglu_chunk(xg_scr[rows, :], w1_ref, w3_ref, w2_ref)

    for_tiles(expert, 2 * MOE_ROWS)

    @pl.when(f == n_f - 1)
    def _():
        def scatter(start, r):
            rows = pl.ds(start, r)
            y = (y_scr[rows, :] * gr_scr[rows, :]).astype(BF16)
            o_ref[...] += lax.dot_general(jnp.where(one_hot(start, r), 1.0, 0.0).astype(BF16), y, TN_DIMS,
                                          preferred_element_type=F32)

        for_tiles(scatter, MOE_ROWS)


def _moe_call(cnt, h, pos_t, gate_t, w1, w3, w2, *, tb, tf):
    n, d = h.shape
    n_e, _, ff = w1.shape
    grid_spec = pltpu.PrefetchScalarGridSpec(
        num_scalar_prefetch=1,
        grid=(n // tb, n_e, ff // tf),
        in_specs=[
            pl.BlockSpec((tb, d), lambda b, e, f, c: (b, 0)),
            pl.BlockSpec((n_e, tb), lambda b, e, f, c: (0, b)),
            pl.BlockSpec((n_e, tb), lambda b, e, f, c: (0, b)),
            pl.BlockSpec((None, d, tf), lambda b, e, f, c: (e, 0, f)),
            pl.BlockSpec((None, d, tf), lambda b, e, f, c: (e, 0, f)),
            pl.BlockSpec((None, tf, d), lambda b, e, f, c: (e, f, 0)),
        ],
        out_specs=pl.BlockSpec((tb, d), lambda b, e, f, c: (b, 0)),
        scratch_shapes=[pltpu.VMEM((tb, d), BF16), pltpu.VMEM((tb, d), F32), pltpu.VMEM((tb, 1), F32)],
    )
    return pl.pallas_call(
        _moe_kernel,
        grid_spec=grid_spec,
        out_shape=jax.ShapeDtypeStruct((n, d), F32),
        compiler_params=pltpu.CompilerParams(dimension_semantics=("arbitrary",) * 3,
                                             vmem_limit_bytes=MOE_VMEM_LIMIT),
        name="moe",
    )(cnt, h, pos_t, gate_t, w1, w3, w2)


def _resnorm_kernel(x_ref, y_ref, mod_ref, fnw_ref, o_ref):
    o_ref[...] = _rms(x_ref[...] + mod_ref[5:6, :] * y_ref[...]) * fnw_ref[...]


def _resnorm_call(x, y, mod, fnw, *, tm, rows_per_mod):
    n, d = x.shape
    row = lambda i: (i, 0)
    return pl.pallas_call(
        _resnorm_kernel,
        grid=(n // tm,),
        in_specs=[pl.BlockSpec((tm, d), row), pl.BlockSpec((tm, d), row),
                  pl.BlockSpec((None, 6, d), lambda i: (i // (rows_per_mod // tm), 0, 0)),
                  pl.BlockSpec((1, d), lambda i: (0, 0))],
        out_specs=pl.BlockSpec((tm, d), row),
        out_shape=jax.ShapeDtypeStruct((n, d), F32),
        compiler_params=_cparams("arbitrary"),
        name="resnorm",
    )(x, y, mod, fnw)


def _rope_tables(t_lat):
    rows = t_lat // GRID_W
    row = jnp.repeat(jnp.arange(rows, dtype=F32), GRID_W)
    col = jnp.tile(jnp.arange(GRID_W, dtype=F32), rows)
    n_freq = ATT_QK_DIM // 4
    inv_freq = ROPE_THETA ** (-jnp.arange(n_freq, dtype=F32) / n_freq)
    ang = jnp.concatenate([row[:, None] * inv_freq, col[:, None] * inv_freq], axis=-1)
    cos, sin = jnp.cos(ang), jnp.sin(ang)
    reps = LANES // ATT_QK_DIM
    return (jnp.tile(jnp.concatenate([cos, cos], axis=-1), (1, reps)),
            jnp.tile(jnp.concatenate([-sin, sin], axis=-1), (1, reps)))


def _layer_lower_bounds(lb_param):
    cs = jnp.cumsum(jax.nn.softmax(lb_param.astype(F32), axis=0), axis=0)
    return cs - cs[0:1]


def _win_columns(w_in_l):
    c = [w_in_l[:, i * 512:(i + 1) * 512] for i in range(8)]
    return jnp.concatenate([c[0], c[1], c[2], c[7], c[3], c[4], c[5], c[6], w_in_l[:, 4096:]], axis=1).astype(BF16)


def _pad_ff(w, axis, mult):
    ff = w.shape[axis]
    pad = (-ff) % mult
    if pad == 0:
        return w
    widths = [(0, 0)] * w.ndim
    widths[axis] = (0, pad)
    return jnp.pad(w, widths)


def kernel(x, c, ctx, c_ctx, w_ada, b_ada, norm_mix_w, norm_ffn_w, w_in, lambda_q1, lambda_k1, lambda_q2,
           lambda_k2, att_norm_w, rec_norm_w, lb_fwd, lb_bwd, w_up_att, w_up_rec, w_out, ffn_w1, ffn_w3,
           ffn_w2, router_w, moe_w1, moe_w3, moe_w2, final_norm_w):
    batch, t_lat, d = x.shape
    t_ctx = ctx.shape[1]
    depth = w_ada.shape[0]
    n_lat, n_ctx = batch * t_lat, batch * t_ctx
    tm = 512
    tm_in = min(1024, t_lat)
    tq = min(512, t_lat)

    xl = x.reshape(n_lat, d)
    xc = ctx.reshape(n_ctx, d)

    pad_rows = (-(batch + 1)) % 8
    cc = jnp.concatenate([c, jnp.zeros((pad_rows, d), F32), c_ctx[None, :]], axis=0)
    mod_all = _mod_call(cc, w_ada, b_ada).reshape(depth, cc.shape[0], 6, d)

    cos, sin = _rope_tables(t_lat)
    lbs_f = _layer_lower_bounds(lb_fwd)
    lbs_b = _layer_lower_bounds(lb_bwd)

    for l in range(depth):
        last = l == depth - 1
        mod = mod_all[l]
        lam_init = 0.8 - 0.6 * math.exp(-0.3 * l)
        lam = (jnp.exp(jnp.sum(lambda_q1[l] * lambda_k1[l])) - jnp.exp(jnp.sum(lambda_q2[l] * lambda_k2[l]))
               + lam_init).reshape(1).astype(F32)
        lb = jnp.concatenate([lbs_f[l], lbs_b[l]])[None, :]
        w_l = _win_columns(w_in[l])
        nw = norm_mix_w[l][None, :]

        pl_lat = _inproj_call(xl, mod, nw, w_l, cos, sin, lb, tm=tm_in, rows_per_mod=t_lat, rope=True)
        pl_ctx = _inproj_call(xc, mod, nw, w_l, cos, sin, lb, tm=min(tm_in, n_ctx), rows_per_mod=None,
                              rope=False)
        qk_l, vg_l, qi_l, lf_l, g_l = pl_lat
        qk_c, vg_c, qi_c, lf_c, g_c = pl_ctx

        anw = att_norm_w[l][None, :]
        ya_l = _attn_call(lam, qk_l, qk_l, vg_l, qk_c, vg_c, anw, batch=batch, tq=tq,
                          lam_init=lam_init, with_lat=True)
        yr_l, yr_c = _gla_call(qi_l, lf_l, vg_l, qi_c, lf_c, vg_c, rec_norm_w[l][None, :], batch=batch)

        wua = w_up_att[l].astype(BF16)
        wur = w_up_rec[l].astype(BF16)
        wo = w_out[l].astype(BF16)
        nw2 = norm_ffn_w[l][None, :]
        moe_layer = l % 2 == 1
        j = l // 2
        assert moe_layer == last
        rw = None
        if moe_layer:
            rw32 = jnp.pad(router_w[j], ((0, 0), (0, LANES - N_EXPERTS)))
            rw_hi = rw32.astype(BF16)
            rw = jnp.stack([rw_hi, (rw32 - rw_hi.astype(F32)).astype(BF16)])
        res = _outproj_call(ya_l, yr_l, g_l, xl, mod, wua, wur, wo, nw2, rw, tm=tm, rows_per_mod=t_lat,
                            route_block=t_lat)
        xl, h2_l = res[0], res[1]
        if not last:
            ya_c = _attn_call(lam, qk_c, None, None, qk_c, vg_c, anw, batch=batch, tq=t_ctx,
                              lam_init=lam_init, with_lat=False)
            xc, h2_c = _outproj_call(ya_c, yr_c, g_c, xc, mod, wua, wur, wo, nw2, None, tm=tm, rows_per_mod=None)

        if moe_layer:
            gates, pos, cnt = res[2], res[3], res[4]
            cnt = cnt[:, 0, :N_EXPERTS].astype(jnp.int32).reshape(-1)
            y = _moe_call(cnt, h2_l, pos[:, :N_EXPERTS].T, gates[:, :N_EXPERTS].T,
                          moe_w1[j].astype(BF16), moe_w3[j].astype(BF16), moe_w2[j].astype(BF16),
                          tb=t_lat, tf=512)
            xl = _resnorm_call(xl, y, mod, final_norm_w[None, :], tm=tm, rows_per_mod=t_lat)
        else:
            w1 = _pad_ff(ffn_w1[j], 1, 2 * LANES).astype(BF16)
            w3 = _pad_ff(ffn_w3[j], 1, 2 * LANES).astype(BF16)
            w2 = _pad_ff(ffn_w2[j], 0, 2 * LANES).astype(BF16)
            tf = w1.shape[1] // 2
            xl = _ffn_call(h2_l, xl, mod, w1, w3, w2, tm=tm, tf=tf, rows_per_mod=t_lat)
            xc = _ffn_call(h2_c, xc, mod, w1, w3, w2, tm=tm, tf=tf, rows_per_mod=None)

    return xl.reshape(batch, t_lat, d)
```

```python
import functools
import math

import numpy as np
import jax
import jax.numpy as jnp
from jax import lax
from jax.experimental import pallas as pl
from jax.experimental.pallas import tpu as pltpu

F32 = jnp.float32
BF16 = jnp.bfloat16
HIGHEST = lax.Precision.HIGHEST

EPS = 1e-6
GRID_W = 64
ROPE_THETA = 10000.0
ATT_HEADS = 4
ATT_QK_DIM = 64
REC_HEADS = 4
N_EXPERTS = 8
Q_SCALE = ATT_QK_DIM ** -0.5 * math.log2(math.e)

LANES = 128
GLA_BLOCK = 256
GLA_CHUNK = 256
GLA_LEVELS = tuple(2 ** i for i in range(GLA_CHUNK.bit_length() - 1))
VMEM_LIMIT = 56 * 1024 * 1024
MOE_VMEM_LIMIT = 60 * 1024 * 1024
MOE_ROWS = 256
ATT_KEYS = 256

NT_DIMS = (((1,), (1,)), ((), ()))
TN_DIMS = (((0,), (0,)), ((), ()))


def _cparams(*sem):
    return pltpu.CompilerParams(dimension_semantics=sem, vmem_limit_bytes=VMEM_LIMIT)


def _silu(a):
    return a * jax.nn.sigmoid(a)


def _rms(x):
    return x * lax.rsqrt(jnp.mean(x * x, axis=-1, keepdims=True) + EPS)


def _mod_kernel(c_ref, w_ref, b_ref, o_ref):
    s = _silu(c_ref[...])
    o_ref[...] = jnp.dot(s, w_ref[...], precision=HIGHEST, preferred_element_type=F32) + b_ref[...]


def _mod_call(cc, w_ada, b_ada):
    depth, d, n = w_ada.shape
    rows = cc.shape[0]
    tn = 1536
    return pl.pallas_call(
        _mod_kernel,
        grid=(depth, n // tn),
        in_specs=[
            pl.BlockSpec((rows, d), lambda l, j: (0, 0)),
            pl.BlockSpec((None, d, tn), lambda l, j: (l, 0, j)),
            pl.BlockSpec((None, 1, tn), lambda l, j: (l, 0, j)),
        ],
        out_specs=pl.BlockSpec((None, rows, tn), lambda l, j: (l, 0, j)),
        out_shape=jax.ShapeDtypeStruct((depth, rows, n), F32),
        compiler_params=_cparams("arbitrary", "arbitrary"),
        name="mod",
    )(cc, w_ada, b_ada.reshape(depth, 1, n))


def _log_forget(z, lb):
    t = jnp.exp(-jnp.abs(z))
    num = jnp.where(z >= 0.0, 1.0 + lb * t, lb + t)
    return jnp.where(num > 0.0, jnp.log(num / (1.0 + t)), z)


def _inproj_kernel(x_ref, mod_ref, nw_ref, w_ref, cos_ref, sin_ref, lb_ref,
                   qk_ref, vg_ref, qi_ref, lf_ref, g_ref, h_scr, *, rope):
    s = pl.program_id(1)

    @pl.when(s == 0)
    def _():
        h = _rms(x_ref[...]) * nw_ref[...]
        h_scr[...] = (h * (1.0 + mod_ref[1:2, :]) + mod_ref[0:1, :]).astype(BF16)

    def proj():
        return jnp.dot(h_scr[...], w_ref[...], preferred_element_type=F32)

    half = w_ref.shape[1] // 2

    @pl.when(s == 0)
    def _():
        p = proj()
        if rope:
            cos = cos_ref[...]
            sin = sin_ref[...]
            lane = lax.broadcasted_iota(jnp.int32, cos.shape, 1)
            first = (lane % ATT_QK_DIM) < (ATT_QK_DIM // 2)
        for j in range(p.shape[1] // LANES):
            blk = p[:, j * LANES:(j + 1) * LANES]
            if rope:
                swapped = jnp.where(first,
                                    pltpu.roll(blk, LANES - ATT_QK_DIM // 2, 1),
                                    pltpu.roll(blk, ATT_QK_DIM // 2, 1))
                blk = blk * cos + swapped * sin
            if j * LANES < half:
                blk = blk * Q_SCALE
            qk_ref[:, j * LANES:(j + 1) * LANES] = blk.astype(BF16)

    @pl.when(s == 1)
    def _():
        p = proj()
        vg_ref[:, :half] = p[:, :half].astype(BF16)
        vg_ref[:, half:] = _silu(p[:, half:]).astype(BF16)

    @pl.when(s == 2)
    def _():
        p = proj()
        qi_ref[:, :half] = _silu(p[:, :half]).astype(BF16)
        qi_ref[:, half:] = p[:, half:].astype(BF16)

    @pl.when(s == 3)
    def _():
        lf_ref[...] = _log_forget(proj(), lb_ref[...])

    @pl.when(s >= 4)
    def _():
        g_ref[...] = jax.nn.sigmoid(proj()).astype(BF16)


def _inproj_call(x, mod, nw, w, cos, sin, lb, *, tm, rows_per_mod, rope):
    n, d = x.shape
    ts = 1024
    n_sec = w.shape[1] // ts
    pos_tiles = cos.shape[0] // tm

    def row(i, s):
        return (i, 0)

    def gate_idx(i, s):
        return (i, jnp.maximum(s - 4, 0))

    if rows_per_mod is None:
        mod_map = lambda i, s: (mod.shape[0] - 1, 0, 0)
    else:
        mod_map = lambda i, s: (i // (rows_per_mod // tm), 0, 0)
    outs = pl.pallas_call(
        functools.partial(_inproj_kernel, rope=rope),
        grid=(n // tm, n_sec),
        in_specs=[
            pl.BlockSpec((tm, d), row),
            pl.BlockSpec((None, 6, d), mod_map),
            pl.BlockSpec((1, d), lambda i, s: (0, 0)),
            pl.BlockSpec((d, ts), lambda i, s: (0, s)),
            pl.BlockSpec((tm, LANES), lambda i, s: (i % pos_tiles, 0)),
            pl.BlockSpec((tm, LANES), lambda i, s: (i % pos_tiles, 0)),
            pl.BlockSpec((1, ts), lambda i, s: (0, 0)),
        ],
        out_specs=[
            pl.BlockSpec((tm, ts), row),
            pl.BlockSpec((tm, ts), row),
            pl.BlockSpec((tm, ts), row),
            pl.BlockSpec((tm, ts), row),
            pl.BlockSpec((tm, ts), gate_idx),
        ],
        out_shape=[
            jax.ShapeDtypeStruct((n, ts), BF16),
            jax.ShapeDtypeStruct((n, ts), BF16),
            jax.ShapeDtypeStruct((n, ts), BF16),
            jax.ShapeDtypeStruct((n, ts), F32),
            jax.ShapeDtypeStruct((n, 2 * ts), BF16),
        ],
        scratch_shapes=[pltpu.VMEM((tm, d), BF16)],
        compiler_params=_cparams("arbitrary", "arbitrary"),
        name="inproj_rope" if rope else "inproj",
    )(x, mod, nw, w, cos, sin, lb)
    return outs


def _attn_kernel(lam_ref, q_ref, *refs, post_scale, with_lat):
    if with_lat:
        kl_ref, vl_ref, kc_ref, vc_ref, nw_ref, o_ref, vce_scr, vle_scr = refs
    else:
        kc_ref, vc_ref, nw_ref, o_ref, vce_scr = refs

    @pl.when(pl.program_id(2) == 0)
    def _():
        vce_scr[:, :LANES] = vc_ref[...]
        vce_scr[:, LANES:] = jnp.ones(vc_ref.shape, BF16)
        if with_lat:
            vle_scr[:, :LANES] = vl_ref[...]
            vle_scr[:, LANES:] = jnp.ones(vl_ref.shape, BF16)

    q = q_ref[...]
    rows = q.shape[0]
    lane = lax.broadcasted_iota(jnp.int32, q.shape, 1)
    zero = jnp.zeros_like(q)
    qq = jnp.concatenate([jnp.where(lane < ATT_QK_DIM, q, zero), jnp.where(lane >= ATT_QK_DIM, q, zero)], axis=0)

    blocks = [(kc_ref, vce_scr, 0, kc_ref.shape[0])]
    if with_lat:
        blocks += [(kl_ref, vle_scr, j, ATT_KEYS) for j in range(0, kl_ref.shape[0], ATT_KEYS)]
    m = jnp.full((2 * rows, 1), -jnp.inf, F32)
    acc = jnp.zeros((2 * rows, 2 * LANES), F32)
    for k_ref, v_scr, start, size in blocks:
        s = lax.dot_general(qq, k_ref[start:start + size, :], NT_DIMS, preferred_element_type=F32)
        m_new = jnp.maximum(m, jnp.max(s, axis=-1, keepdims=True))
        p = jnp.exp2(s - m_new).astype(BF16)
        acc = acc * jnp.exp2(m - m_new) + jnp.dot(p, v_scr[start:start + size, :], preferred_element_type=F32)
        m = m_new
    on = acc[:, :LANES] * (1.0 / acc[:, LANES:])
    o = on[:rows] - lam_ref[0] * on[rows:]
    o_ref[...] = (_rms(o) * nw_ref[...] * post_scale).astype(BF16)


def _attn_call(lam, q_src, qk_lat, vg_lat, qk_ctx, vg_ctx, nw, *, batch, tq, lam_init, with_lat):
    n_q = q_src.shape[0]
    t_q = n_q // batch
    t_lat = qk_lat.shape[0] // batch if with_lat else 0
    t_ctx = qk_ctx.shape[0] // batch
    nq_tiles = t_q // tq
    h = ATT_HEADS
    in_specs = [
        pl.BlockSpec(memory_space=pltpu.SMEM),
        pl.BlockSpec((tq, LANES), lambda b, hh, i: (b * nq_tiles + i, hh)),
    ]
    args = [lam, q_src]
    if with_lat:
        in_specs += [
            pl.BlockSpec((t_lat, LANES), lambda b, hh, i: (b, h + hh)),
            pl.BlockSpec((t_lat, LANES), lambda b, hh, i: (b, hh)),
        ]
        args += [qk_lat, vg_lat]
    in_specs += [
        pl.BlockSpec((t_ctx, LANES), lambda b, hh, i: (b, h + hh)),
        pl.BlockSpec((t_ctx, LANES), lambda b, hh, i: (b, hh)),
        pl.BlockSpec((1, LANES), lambda b, hh, i: (0, 0)),
    ]
    args += [qk_ctx, vg_ctx, nw]
    return pl.pallas_call(
        functools.partial(_attn_kernel, post_scale=1.0 - lam_init, with_lat=with_lat),
        grid=(batch, h, nq_tiles),
        in_specs=in_specs,
        out_specs=pl.BlockSpec((tq, LANES), lambda b, hh, i: (b * nq_tiles + i, hh)),
        out_shape=jax.ShapeDtypeStruct((n_q, h * LANES), BF16),
        scratch_shapes=[pltpu.VMEM((t_ctx, 2 * LANES), BF16)]
        + ([pltpu.VMEM((t_lat, 2 * LANES), BF16)] if with_lat else []),
        compiler_params=_cparams("arbitrary", "arbitrary", "arbitrary"),
        name="attn_lat" if with_lat else "attn_ctx",
    )(*args)


def _gla_constants():
    n = GLA_BLOCK
    idx = np.arange(n)
    same_chunk = (idx[:, None] // GLA_CHUNK) == (idx[None, :] // GLA_CHUNK)
    stacks, masks = [], []
    for reverse in (False, True):
        order = (idx[None, :] >= idx[:, None]) if reverse else (idx[None, :] <= idx[:, None])
        cum = (same_chunk & order).astype(np.int32)
        groups, lvl_masks = [cum], []
        for m in GLA_LEVELS:
            ref = (idx // (2 * m)) * (2 * m) + (m if reverse else m - 1)
            groups.append(np.abs(cum - cum[ref]))
            upper = (idx % (2 * m)) >= m
            same = (idx[:, None] // (2 * m)) == (idx[None, :] // (2 * m))
            q_side, k_side = (~upper, upper) if reverse else (upper, ~upper)
            lvl_masks.append(same & q_side[:, None] & k_side[None, :])
        end = (idx // GLA_CHUNK) * GLA_CHUNK + (0 if reverse else GLA_CHUNK - 1)
        groups.append(np.abs(cum[end] - cum))
        stacks.append(np.concatenate(groups, axis=0))
        masks.append(np.stack(lvl_masks))
    return jnp.asarray(np.stack(stacks), BF16), jnp.asarray(np.stack(masks), F32)


def _gla_prepare(q, v, lf, stack, *, reverse):
    n = q.shape[0]
    qf = q.astype(F32)
    k = 1.0 - jnp.exp(lf)

    hi = lf.astype(BF16)
    mid = (lf - hi.astype(F32)).astype(BF16)
    expo = jnp.dot(stack, jnp.concatenate([hi, mid], axis=1), preferred_element_type=F32)

    def exponent(group):
        blk = expo[group * n:(group + 1) * n]
        return blk[:, :LANES] + blk[:, LANES:]

    scale = [jnp.exp(exponent(1 + li)) for li in range(len(GLA_LEVELS))]
    qs = jnp.concatenate([(qf * e).astype(BF16) for e in scale], axis=1)
    ks = jnp.concatenate([(k * e).astype(BF16) for e in scale], axis=1)
    c = exponent(0)
    io = jnp.concatenate([(qf * jnp.exp(c)).astype(BF16),
                          (k * jnp.exp(exponent(1 + len(GLA_LEVELS)))).astype(BF16)], axis=1)
    diag = jnp.sum(qf * k, axis=-1, keepdims=True) * v.astype(F32)
    last = 0 if reverse else GLA_CHUNK - 1
    n_chunks = n // GLA_CHUNK
    decay = jnp.concatenate([jnp.exp(c[ci * GLA_CHUNK + last:ci * GLA_CHUNK + last + 1, :])
                             for ci in range(n_chunks)] + [jnp.ones((8 - n_chunks, LANES), F32)], axis=0)
    return qs, ks, io, diag, decay


def _gla_scan(qs, ks, io, diag, decay, v, st, masks, *, reverse):
    n = v.shape[0]
    scores = jnp.zeros((n, n), F32)
    for li in range(len(GLA_LEVELS)):
        cols = slice(li * LANES, (li + 1) * LANES)
        s_l = lax.dot_general(qs[:, cols], ks[:, cols], NT_DIMS, preferred_element_type=F32)
        scores = scores + s_l * masks[li]
    o = jnp.dot(scores.astype(BF16), v, preferred_element_type=F32) + diag

    n_chunks = n // GLA_CHUNK
    outs = [None] * n_chunks
    order = range(n_chunks - 1, -1, -1) if reverse else range(n_chunks)
    for ci in order:
        r = slice(ci * GLA_CHUNK, (ci + 1) * GLA_CHUNK)
        outs[ci] = lax.dot_general(io[r, :LANES], st.astype(BF16), NT_DIMS, preferred_element_type=F32)
        upd = lax.dot_general(v[r], io[r, LANES:], TN_DIMS, preferred_element_type=F32)
        st = st * decay[ci:ci + 1, :] + upd
    return o + jnp.concatenate(outs, axis=0), st


def _gla_kernel(ql_ref, il_ref, ffl_ref, fbl_ref, gl_ref, qc_ref, ic_ref, ffc_ref, fbc_ref, gc_ref,
                nw_ref, stack_ref, mask_ref, ol_ref, oc_ref,
                qs_scr, ks_scr, io_scr, dg_scr, dc_scr, of_scr, ob_scr):
    n_blocks = ql_ref.shape[0] // GLA_BLOCK
    nw = nw_ref[...]

    def finish(o, g):
        return (_rms(o) * nw * g.astype(F32)).astype(BF16)

    def rows(i):
        return pl.ds(pl.multiple_of(i * GLA_BLOCK, GLA_BLOCK), GLA_BLOCK)

    def prepare(slot, q, v, lf_f, lf_b):
        for d, lf in enumerate((lf_f, lf_b)):
            qs, ks, io, diag, decay = _gla_prepare(q, v, lf, stack_ref[d], reverse=bool(d))
            qs_scr[d, slot] = qs
            ks_scr[d, slot] = ks
            io_scr[d, slot] = io
            dg_scr[d, slot] = diag
            dc_scr[d, slot] = decay

    def scan(d, slot, v, st):
        return _gla_scan(qs_scr[d, slot], ks_scr[d, slot], io_scr[d, slot], dg_scr[d, slot], dc_scr[d, slot],
                         v, st, mask_ref[d], reverse=bool(d))

    prepare(0, qc_ref[...], ic_ref[...], ffc_ref[...], fbc_ref[...])

    def prep(i, carry):
        r = rows(i)
        prepare(i + 1, ql_ref[r, :], il_ref[r, :], ffl_ref[r, :], fbl_ref[r, :])
        return carry

    lax.fori_loop(0, n_blocks, prep, 0)

    zero_state = jnp.zeros((LANES, LANES), F32)
    o_cf, st_f = scan(0, 0, ic_ref[...], zero_state)
    o_cb, st_b = scan(1, 0, ic_ref[...], zero_state)
    oc_ref[...] = finish(o_cf + o_cb, gc_ref[...])

    def step(i, carry):
        st_f, st_b = carry
        j = n_blocks - 1 - i
        o_f, st_f = scan(0, i + 1, il_ref[rows(i), :], st_f)
        o_b, st_b = scan(1, j + 1, il_ref[rows(j), :], st_b)
        of_scr[rows(i), :] = o_f
        ob_scr[rows(j), :] = o_b
        return st_f, st_b

    lax.fori_loop(0, n_blocks, step, (st_f, st_b))

    def emit(i, carry):
        r = rows(i)
        ol_ref[r, :] = finish(of_scr[r, :] + ob_scr[r, :], gl_ref[r, :])
        return carry

    lax.fori_loop(0, n_blocks, emit, 0)


def _gla_call(qi_lat, lf_lat, vg_lat, qi_ctx, lf_ctx, vg_ctx, nw, *, batch):
    t_lat = qi_lat.shape[0] // batch
    t_ctx = qi_ctx.shape[0] // batch
    assert t_ctx == GLA_BLOCK and t_lat % GLA_BLOCK == 0
    h = REC_HEADS
    stack, masks = _gla_constants()
    n_slots = 1 + t_lat // GLA_BLOCK
    n_lvl = len(GLA_LEVELS)

    def col(off):
        return lambda b, hh: (b, off + hh)

    def seq(t):
        return [pl.BlockSpec((t, LANES), col(0)), pl.BlockSpec((t, LANES), col(h)),
                pl.BlockSpec((t, LANES), col(0)), pl.BlockSpec((t, LANES), col(h)),
                pl.BlockSpec((t, LANES), col(h))]

    return pl.pallas_call(
        _gla_kernel,
        grid=(batch, h),
        in_specs=seq(t_lat) + seq(t_ctx) + [
            pl.BlockSpec((1, LANES), lambda b, hh: (0, 0)),
            pl.BlockSpec(stack.shape, lambda b, hh: (0, 0, 0)),
            pl.BlockSpec(masks.shape, lambda b, hh: (0, 0, 0, 0)),
        ],
        out_specs=[pl.BlockSpec((t_lat, LANES), col(0)), pl.BlockSpec((t_ctx, LANES), col(0))],
        out_shape=[jax.ShapeDtypeStruct((batch * t_lat, h * LANES), BF16),
                   jax.ShapeDtypeStruct((batch * t_ctx, h * LANES), BF16)],
        scratch_shapes=[
            pltpu.VMEM((2, n_slots, GLA_BLOCK, n_lvl * LANES), BF16),
            pltpu.VMEM((2, n_slots, GLA_BLOCK, n_lvl * LANES), BF16),
            pltpu.VMEM((2, n_slots, GLA_BLOCK, 2 * LANES), BF16),
            pltpu.VMEM((2, n_slots, GLA_BLOCK, LANES), F32),
            pltpu.VMEM((2, n_slots, 8, LANES), F32),
            pltpu.VMEM((t_lat, LANES), F32), pltpu.VMEM((t_lat, LANES), F32),
        ],
        compiler_params=_cparams("arbitrary", "arbitrary"),
        name="gla",
    )(qi_lat, qi_lat, lf_lat, lf_lat, vg_lat, qi_ctx, qi_ctx, lf_ctx, lf_ctx, vg_ctx, nw, stack, masks)


def _top2_gates(logits):
    lane = lax.broadcasted_iota(jnp.int32, logits.shape, 1).astype(F32)
    big = float(LANES)
    m1 = jnp.max(logits, axis=-1, keepdims=True)
    i1 = jnp.min(jnp.where(logits == m1, lane, big), axis=-1, keepdims=True)
    rest = jnp.where(lane == i1, -jnp.inf, logits)
    m2 = jnp.max(rest, axis=-1, keepdims=True)
    i2 = jnp.min(jnp.where(rest == m2, lane, big), axis=-1, keepdims=True)
    e = jnp.exp(m2 - m1)
    w1 = 1.0 / (1.0 + e)
    return jnp.where(lane == i1, w1, 0.0) + jnp.where(lane == i2, e * w1, 0.0)


def _outproj_kernel(ya_ref, yr_ref, g_ref, x_ref, mod_ref, wua_ref, wur_ref, wo_ref, nw2_ref, *refs,
                    router, tiles_per_block):
    if router:
        rw_ref, tri_ref, xo_ref, h2_ref, gate_ref, pos_ref, cnt_ref, carry_scr = refs
    else:
        xo_ref, h2_ref = refs
    d = x_ref.shape[1]
    ua = jnp.dot(ya_ref[...], wua_ref[...], preferred_element_type=F32)
    ur = jnp.dot(yr_ref[...], wur_ref[...], preferred_element_type=F32)
    u = g_ref[:, :d].astype(F32) * ua + g_ref[:, d:].astype(F32) * ur
    y = jnp.dot(u.astype(BF16), wo_ref[...], preferred_element_type=F32)
    xn = x_ref[...] + mod_ref[2:3, :] * y
    xo_ref[...] = xn
    h2 = (_rms(xn) * nw2_ref[...]) * (1.0 + mod_ref[4:5, :]) + mod_ref[3:4, :]
    h2_hi = h2.astype(BF16)
    h2_ref[...] = h2_hi
    if router:
        h2_lo = (h2 - h2_hi.astype(F32)).astype(BF16)
        logits = (jnp.dot(h2_hi, rw_ref[0], preferred_element_type=F32)
                  + jnp.dot(h2_lo, rw_ref[0], preferred_element_type=F32)
                  + jnp.dot(h2_hi, rw_ref[1], preferred_element_type=F32))
        lane = lax.broadcasted_iota(jnp.int32, logits.shape, 1)
        gates = _top2_gates(jnp.where(lane < N_EXPERTS, logits, -jnp.inf))
        gate_ref[...] = gates

        @pl.when(pl.program_id(0) % tiles_per_block == 0)
        def _():
            carry_scr[...] = jnp.zeros_like(carry_scr)

        sel = gates > 0.0
        sel_f = jnp.where(sel, 1.0, 0.0)
        rank = jnp.dot(tri_ref[...], sel_f.astype(BF16), preferred_element_type=F32) + carry_scr[...]
        pos_ref[...] = jnp.where(sel, rank, -1.0)
        total = carry_scr[...] + jnp.sum(sel_f, axis=0, keepdims=True)
        carry_scr[...] = total
        cnt_ref[...] = total


def _outproj_call(ya, yr, g, x, mod, wua, wur, wo, nw2, rw, *, tm, rows_per_mod, route_block=None):
    n, d = x.shape
    router = rw is not None
    row = lambda i: (i, 0)
    const = lambda i: (0, 0)
    tiles_per_block = route_block // tm if router else 1
    if rows_per_mod is None:
        mod_map = lambda i: (mod.shape[0] - 1, 0, 0)
    else:
        mod_map = lambda i: (i // (rows_per_mod // tm), 0, 0)
    in_specs = [
        pl.BlockSpec((tm, ya.shape[1]), row),
        pl.BlockSpec((tm, yr.shape[1]), row),
        pl.BlockSpec((tm, 2 * d), row),
        pl.BlockSpec((tm, d), row),
        pl.BlockSpec((None, 6, d), mod_map),
        pl.BlockSpec(wua.shape, const),
        pl.BlockSpec(wur.shape, const),
        pl.BlockSpec(wo.shape, const),
        pl.BlockSpec((1, d), const),
    ]
    args = [ya, yr, g, x, mod, wua, wur, wo, nw2]
    out_specs = [pl.BlockSpec((tm, d), row), pl.BlockSpec((tm, d), row)]
    out_shape = [jax.ShapeDtypeStruct((n, d), F32), jax.ShapeDtypeStruct((n, d), BF16)]
    scratch = []
    if router:
        tri = jnp.asarray(np.tril(np.ones((tm, tm), np.float32), -1), BF16)
        in_specs += [pl.BlockSpec(rw.shape, lambda i: (0, 0, 0)), pl.BlockSpec(tri.shape, const)]
        args += [rw, tri]
        out_specs += [pl.BlockSpec((tm, LANES), row), pl.BlockSpec((tm, LANES), row),
                      pl.BlockSpec((None, 1, LANES), lambda i: (i // tiles_per_block, 0, 0))]
        out_shape += [jax.ShapeDtypeStruct((n, LANES), F32), jax.ShapeDtypeStruct((n, LANES), F32),
                      jax.ShapeDtypeStruct((n // route_block, 1, LANES), F32)]
        scratch = [pltpu.VMEM((1, LANES), F32)]
    return pl.pallas_call(
        functools.partial(_outproj_kernel, router=router, tiles_per_block=tiles_per_block),
        grid=(n // tm,),
        in_specs=in_specs,
        out_specs=out_specs,
        out_shape=out_shape,
        scratch_shapes=scratch,
        compiler_params=_cparams("arbitrary"),
        name="outproj_router" if router else "outproj",
    )(*args)


def _swiglu_chunk(xs, w1_ref, w3_ref, w2_ref):
    a = jnp.dot(xs, w1_ref[...], preferred_element_type=F32)
    b = jnp.dot(xs, w3_ref[...], preferred_element_type=F32)
    return jnp.dot((_silu(a) * b).astype(BF16), w2_ref[...], preferred_element_type=F32)


def _ffn_kernel(h_ref, x_ref, mod_ref, w1_ref, w3_ref, w2_ref, o_ref, acc_ref):
    f = pl.program_id(1)

    @pl.when(f == 0)
    def _():
        acc_ref[...] = jnp.zeros_like(acc_ref)

    acc_ref[...] += _swiglu_chunk(h_ref[...], w1_ref, w3_ref, w2_ref)

    @pl.when(f == pl.num_programs(1) - 1)
    def _():
        o_ref[...] = x_ref[...] + mod_ref[5:6, :] * acc_ref[...]


def _ffn_call(h, x, mod, w1, w3, w2, *, tm, tf, rows_per_mod):
    n, d = x.shape
    ff = w1.shape[1]
    row = lambda i, f: (i, 0)
    if rows_per_mod is None:
        mod_map = lambda i, f: (mod.shape[0] - 1, 0, 0)
    else:
        mod_map = lambda i, f: (i // (rows_per_mod // tm), 0, 0)
    return pl.pallas_call(
        _ffn_kernel,
        grid=(n // tm, ff // tf),
        in_specs=[
            pl.BlockSpec((tm, d), row), pl.BlockSpec((tm, d), row), pl.BlockSpec((None, 6, d), mod_map),
            pl.BlockSpec((d, tf), lambda i, f: (0, f)),
            pl.BlockSpec((d, tf), lambda i, f: (0, f)),
            pl.BlockSpec((tf, d), lambda i, f: (f, 0)),
        ],
        out_specs=pl.BlockSpec((tm, d), row),
        out_shape=jax.ShapeDtypeStruct((n, d), F32),
        scratch_shapes=[pltpu.VMEM((tm, d), F32)],
        compiler_params=_cparams("arbitrary", "arbitrary"),
        name="ffn",
    )(h, x, mod, w1, w3, w2)


def _moe_kernel(cnt_ref, h_ref, pos_ref, gate_ref, w1_ref, w3_ref, w2_ref, o_ref, xg_scr, y_scr, gr_scr):
    b, e, f = pl.program_id(0), pl.program_id(1), pl.program_id(2)
    n_e, n_f = pl.num_programs(1), pl.num_programs(2)
    tb = h_ref.shape[0]
    pos_row = pos_ref[pl.ds(e, 1), :]
    half = MOE_ROWS // 2
    n_half = (cnt_ref[b * n_e + e] + half - 1) // half
    n_full = n_half // 2
    has_half = n_half % 2 == 1
    tail = pl.multiple_of(n_full * MOE_ROWS, MOE_ROWS)

    def one_hot(start, r):
        slot = lax.broadcasted_iota(jnp.int32, (r, tb), 0) + start
        return pos_row == slot.astype(F32)

    def for_tiles(fn, rows_per_step):
        k = rows_per_step // MOE_ROWS
        n_steps = n_full // k

        def body(s, carry):
            fn(pl.multiple_of(s * rows_per_step, rows_per_step), rows_per_step)
            return carry

        lax.fori_loop(0, n_steps, body, 0)
        if k > 1:
            @pl.when(n_full % k == 1)
            def _():
                fn(pl.multiple_of(n_steps * rows_per_step, rows_per_step), MOE_ROWS)

        @pl.when(has_half)
        def _():
            fn(tail, half)

    @pl.when((e == 0) & (f == 0))
    def _():
        o_ref[...] = jnp.zeros_like(o_ref)

    @pl.when(f == 0)
    def _():
        gate_row = gate_ref[pl.ds(e, 1), :]

        def gather(start, r):
            hit = one_hot(start, r)
            rows = pl.ds(start, r)
            xg_scr[rows, :] = jnp.dot(jnp.where(hit, 1.0, 0.0).astype(BF16), h_ref[...],
                                      preferred_element_type=F32).astype(BF16)
            gr_scr[rows, :] = jnp.sum(jnp.where(hit, gate_row, 0.0), axis=-1, keepdims=True)
            y_scr[rows, :] = jnp.zeros((r, y_scr.shape[1]), F32)

        for_tiles(gather, MOE_ROWS)

    def expert(start, r):
        rows = pl.ds(start, r)
        y_scr[rows, :] += _swiglu_chunk(xg_scr[rows, :], w1_ref, w3_ref, w2_ref)

    for_tiles(expert, 2 * MOE_ROWS)

    @pl.when(f == n_f - 1)
    def _():
        def scatter(start, r):
            rows = pl.ds(start, r)
            y = (y_scr[rows, :] * gr_scr[rows, :]).astype(BF16)
            o_ref[...] += lax.dot_general(jnp.where(one_hot(start, r), 1.0, 0.0).astype(BF16), y, TN_DIMS,
                                          preferred_element_type=F32)

        for_tiles(scatter, MOE_ROWS)


def _moe_call(cnt, h, pos_t, gate_t, w1, w3, w2, *, tb, tf):
    n, d = h.shape
    n_e, _, ff = w1.shape
    grid_spec = pltpu.PrefetchScalarGridSpec(
        num_scalar_prefetch=1,
        grid=(n // tb, n_e, ff // tf),
        in_specs=[
            pl.BlockSpec((tb, d), lambda b, e, f, c: (b, 0)),
            pl.BlockSpec((n_e, tb), lambda b, e, f, c: (0, b)),
            pl.BlockSpec((n_e, tb), lambda b, e, f, c: (0, b)),
            pl.BlockSpec((None, d, tf), lambda b, e, f, c: (e, 0, f)),
            pl.BlockSpec((None, d, tf), lambda b, e, f, c: (e, 0, f)),
            pl.BlockSpec((None, tf, d), lambda b, e, f, c: (e, f, 0)),
        ],
        out_specs=pl.BlockSpec((tb, d), lambda b, e, f, c: (b, 0)),
        scratch_shapes=[pltpu.VMEM((tb, d), BF16), pltpu.VMEM((tb, d), F32), pltpu.VMEM((tb, 1), F32)],
    )
    return pl.pallas_call(
        _moe_kernel,
        grid_spec=grid_spec,
        out_shape=jax.ShapeDtypeStruct((n, d), F32),
        compiler_params=pltpu.CompilerParams(dimension_semantics=("arbitrary",) * 3,
                                             vmem_limit_bytes=MOE_VMEM_LIMIT),
        name="moe",
    )(cnt, h, pos_t, gate_t, w1, w3, w2)


def _resnorm_kernel(x_ref, y_ref, mod_ref, fnw_ref, o_ref):
    o_ref[...] = _rms(x_ref[...] + mod_ref[5:6, :] * y_ref[...]) * fnw_ref[...]


def _resnorm_call(x, y, mod, fnw, *, tm, rows_per_mod):
    n, d = x.shape
    row = lambda i: (i, 0)
    return pl.pallas_call(
        _resnorm_kernel,
        grid=(n // tm,),
        in_specs=[pl.BlockSpec((tm, d), row), pl.BlockSpec((tm, d), row),
                  pl.BlockSpec((None, 6, d), lambda i: (i // (rows_per_mod // tm), 0, 0)),
                  pl.BlockSpec((1, d), lambda i: (0, 0))],
        out_specs=pl.BlockSpec((tm, d), row),
        out_shape=jax.ShapeDtypeStruct((n, d), F32),
        compiler_params=_cparams("arbitrary"),
        name="resnorm",
    )(x, y, mod, fnw)


def _rope_tables(t_lat):
    rows = t_lat // GRID_W
    row = jnp.repeat(jnp.arange(rows, dtype=F32), GRID_W)
    col = jnp.tile(jnp.arange(GRID_W, dtype=F32), rows)
    n_freq = ATT_QK_DIM // 4
    inv_freq = ROPE_THETA ** (-jnp.arange(n_freq, dtype=F32) / n_freq)
    ang = jnp.concatenate([row[:, None] * inv_freq, col[:, None] * inv_freq], axis=-1)
    cos, sin = jnp.cos(ang), jnp.sin(ang)
    reps = LANES // ATT_QK_DIM
    return (jnp.tile(jnp.concatenate([cos, cos], axis=-1), (1, reps)),
            jnp.tile(jnp.concatenate([-sin, sin], axis=-1), (1, reps)))


def _layer_lower_bounds(lb_param):
    cs = jnp.cumsum(jax.nn.softmax(lb_param.astype(F32), axis=0), axis=0)
    return cs - cs[0:1]


def _win_columns(w_in_l):
    c = [w_in_l[:, i * 512:(i + 1) * 512] for i in range(8)]
    return jnp.concatenate([c[0], c[1], c[2], c[7], c[3], c[4], c[5], c[6], w_in_l[:, 4096:]], axis=1).astype(BF16)


def _pad_ff(w, axis, mult):
    ff = w.shape[axis]
    pad = (-ff) % mult
    if pad == 0:
        return w
    widths = [(0, 0)] * w.ndim
    widths[axis] = (0, pad)
    return jnp.pad(w, widths)


def kernel(x, c, ctx, c_ctx, w_ada, b_ada, norm_mix_w, norm_ffn_w, w_in, lambda_q1, lambda_k1, lambda_q2,
           lambda_k2, att_norm_w, rec_norm_w, lb_fwd, lb_bwd, w_up_att, w_up_rec, w_out, ffn_w1, ffn_w3,
           ffn_w2, router_w, moe_w1, moe_w3, moe_w2, final_norm_w):
    batch, t_lat, d = x.shape
    t_ctx = ctx.shape[1]
    depth = w_ada.shape[0]
    n_lat, n_ctx = batch * t_lat, batch * t_ctx
    tm = 512
    tm_in = min(1024, t_lat)
    tq = min(512, t_lat)

    xl = x.reshape(n_lat, d)
    xc = ctx.reshape(n_ctx, d)

    pad_rows = (-(batch + 1)) % 8
    cc = jnp.concatenate([c, jnp.zeros((pad_rows, d), F32), c_ctx[None, :]], axis=0)
    mod_all = _mod_call(cc, w_ada, b_ada).reshape(depth, cc.shape[0], 6, d)

    cos, sin = _rope_tables(t_lat)
    lbs_f = _layer_lower_bounds(lb_fwd)
    lbs_b = _layer_lower_bounds(lb_bwd)

    for l in range(depth):
        last = l == depth - 1
        mod = mod_all[l]
        lam_init = 0.8 - 0.6 * math.exp(-0.3 * l)
        lam = (jnp.exp(jnp.sum(lambda_q1[l] * lambda_k1[l])) - jnp.exp(jnp.sum(lambda_q2[l] * lambda_k2[l]))
               + lam_init).reshape(1).astype(F32)
        lb = jnp.concatenate([lbs_f[l], lbs_b[l]])[None, :]
        w_l = _win_columns(w_in[l])
        nw = norm_mix_w[l][None, :]

        pl_lat = _inproj_call(xl, mod, nw, w_l, cos, sin, lb, tm=tm_in, rows_per_mod=t_lat, rope=True)
        pl_ctx = _inproj_call(xc, mod, nw, w_l, cos, sin, lb, tm=min(tm_in, n_ctx), rows_per_mod=None,
                              rope=False)
        qk_l, vg_l, qi_l, lf_l, g_l = pl_lat
        qk_c, vg_c, qi_c, lf_c, g_c = pl_ctx

        anw = att_norm_w[l][None, :]
        ya_l = _attn_call(lam, qk_l, qk_l, vg_l, qk_c, vg_c, anw, batch=batch, tq=tq,
                          lam_init=lam_init, with_lat=True)
        yr_l, yr_c = _gla_call(qi_l, lf_l, vg_l, qi_c, lf_c, vg_c, rec_norm_w[l][None, :], batch=batch)

        wua = w_up_att[l].astype(BF16)
        wur = w_up_rec[l].astype(BF16)
        wo = w_out[l].astype(BF16)
        nw2 = norm_ffn_w[l][None, :]
        moe_layer = l % 2 == 1
        j = l // 2
        assert moe_layer == last
        rw = None
        if moe_layer:
            rw32 = jnp.pad(router_w[j], ((0, 0), (0, LANES - N_EXPERTS)))
            rw_hi = rw32.astype(BF16)
            rw = jnp.stack([rw_hi, (rw32 - rw_hi.astype(F32)).astype(BF16)])
        res = _outproj_call(ya_l, yr_l, g_l, xl, mod, wua, wur, wo, nw2, rw, tm=tm, rows_per_mod=t_lat,
                            route_block=t_lat)
        xl, h2_l = res[0], res[1]
        if not last:
            ya_c = _attn_call(lam, qk_c, None, None, qk_c, vg_c, anw, batch=batch, tq=t_ctx,
                              lam_init=lam_init, with_lat=False)
            xc, h2_c = _outproj_call(ya_c, yr_c, g_c, xc, mod, wua, wur, wo, nw2, None, tm=tm, rows_per_mod=None)

        if moe_layer:
            gates, pos, cnt = res[2], res[3], res[4]
            cnt = cnt[:, 0, :N_EXPERTS].astype(jnp.int32).reshape(-1)
            y = _moe_call(cnt, h2_l, pos[:, :N_EXPERTS].T, gates[:, :N_EXPERTS].T,
                          moe_w1[j].astype(BF16), moe_w3[j].astype(BF16), moe_w2[j].astype(BF16),
                          tb=t_lat, tf=512)
            xl = _resnorm_call(xl, y, mod, final_norm_w[None, :], tm=tm, rows_per_mod=t_lat)
        else:
            w1 = _pad_ff(ffn_w1[j], 1, 2 * LANES).astype(BF16)
            w3 = _pad_ff(ffn_w3[j], 1, 2 * LANES).astype(BF16)
            w2 = _pad_ff(ffn_w2[j], 0, 2 * LANES).astype(BF16)
            tf = w1.shape[1] // 2
            xl = _ffn_call(h2_l, xl, mod, w1, w3, w2, tm=tm, tf=tf, rows_per_mod=t_lat)
            xc = _ffn_call(h2_c, xc, mod, w1, w3, w2, tm=tm, tf=tf, rows_per_mod=None)

    return xl.reshape(batch, t_lat, d)
```

```python
import functools
import math

import numpy as np
import jax
import jax.numpy as jnp
from jax import lax
from jax.experimental import pallas as pl
from jax.experimental.pallas import tpu as pltpu

F32 = jnp.float32
BF16 = jnp.bfloat16
HIGHEST = lax.Precision.HIGHEST

EPS = 1e-6
GRID_W = 64
ROPE_THETA = 10000.0
ATT_HEADS = 4
ATT_QK_DIM = 64
REC_HEADS = 4
N_EXPERTS = 8
Q_SCALE = ATT_QK_DIM ** -0.5 * math.log2(math.e)

LANES = 128
GLA_BLOCK = 256
GLA_CHUNK = 256
GLA_LEVELS = tuple(2 ** i for i in range(GLA_CHUNK.bit_length() - 1))
VMEM_LIMIT = 56 * 1024 * 1024
MOE_VMEM_LIMIT = 60 * 1024 * 1024
MOE_ROWS = 256
ATT_KEYS = 256

NT_DIMS = (((1,), (1,)), ((), ()))
TN_DIMS = (((0,), (0,)), ((), ()))


def _cparams(*sem):
    return pltpu.CompilerParams(dimension_semantics=sem, vmem_limit_bytes=VMEM_LIMIT)


def _silu(a):
    return a * jax.nn.sigmoid(a)


def _rms(x):
    return x * lax.rsqrt(jnp.mean(x * x, axis=-1, keepdims=True) + EPS)


def _mod_kernel(c_ref, w_ref, b_ref, o_ref):
    s = _silu(c_ref[...])
    o_ref[...] = jnp.dot(s, w_ref[...], precision=HIGHEST, preferred_element_type=F32) + b_ref[...]


def _mod_call(cc, w_ada, b_ada):
    depth, d, n = w_ada.shape
    rows = cc.shape[0]
    tn = 1536
    return pl.pallas_call(
        _mod_kernel,
        grid=(depth, n // tn),
        in_specs=[
            pl.BlockSpec((rows, d), lambda l, j: (0, 0)),
            pl.BlockSpec((None, d, tn), lambda l, j: (l, 0, j)),
            pl.BlockSpec((None, 1, tn), lambda l, j: (l, 0, j)),
        ],
        out_specs=pl.BlockSpec((None, rows, tn), lambda l, j: (l, 0, j)),
        out_shape=jax.ShapeDtypeStruct((depth, rows, n), F32),
        compiler_params=_cparams("arbitrary", "arbitrary"),
        name="mod",
    )(cc, w_ada, b_ada.reshape(depth, 1, n))


def _log_forget(z, lb):
    t = jnp.exp(-jnp.abs(z))
    num = jnp.where(z >= 0.0, 1.0 + lb * t, lb + t)
    return jnp.where(num > 0.0, jnp.log(num / (1.0 + t)), z)


def _inproj_kernel(x_ref, mod_ref, nw_ref, w_ref, cos_ref, sin_ref, lb_ref,
                   qk_ref, vg_ref, qi_ref, lf_ref, g_ref, *, rope):
    h = _rms(x_ref[...]) * nw_ref[...]
    h = (h * (1.0 + mod_ref[1:2, :]) + mod_ref[0:1, :]).astype(BF16)
    ts = qk_ref.shape[1]
    half = ts // 2

    def proj(s):
        return jnp.dot(h, w_ref[:, s * ts:(s + 1) * ts], preferred_element_type=F32)

    p = proj(0)
    if rope:
        cos = cos_ref[...]
        sin = sin_ref[...]
    for j in range(ts // LANES):
        blk = p[:, j * LANES:(j + 1) * LANES]
        if rope:
            blk = blk * cos + pltpu.roll(blk, LANES // 2, 1) * sin
        if j * LANES < half:
            blk = blk * Q_SCALE
        qk_ref[:, j * LANES:(j + 1) * LANES] = blk.astype(BF16)

    p = proj(1)
    vg_ref[:, :half] = p[:, :half].astype(BF16)
    vg_ref[:, half:] = _silu(p[:, half:]).astype(BF16)

    p = proj(2)
    qi_ref[:, :half] = _silu(p[:, :half]).astype(BF16)
    qi_ref[:, half:] = p[:, half:].astype(BF16)

    lf_ref[...] = _log_forget(proj(3), lb_ref[...])

    g_ref[:, :ts] = jax.nn.sigmoid(proj(4)).astype(BF16)
    g_ref[:, ts:] = jax.nn.sigmoid(proj(5)).astype(BF16)


def _inproj_call(x, mod, nw, w, cos, sin, lb, *, tm, rows_per_mod, rope):
    n, d = x.shape
    ts = 1024
    assert w.shape[1] == 6 * ts
    pos_tiles = cos.shape[0] // tm
    row = lambda i: (i, 0)
    const = lambda i: (0, 0)
    if rows_per_mod is None:
        mod_map = lambda i: (mod.shape[0] - 1, 0, 0)
    else:
        mod_map = lambda i: (i // (rows_per_mod // tm), 0, 0)
    outs = pl.pallas_call(
        functools.partial(_inproj_kernel, rope=rope),
        grid=(n // tm,),
        in_specs=[
            pl.BlockSpec((tm, d), row),
            pl.BlockSpec((None, 6, d), mod_map),
            pl.BlockSpec((1, d), const),
            pl.BlockSpec(w.shape, const),
            pl.BlockSpec((tm, LANES), lambda i: (i % pos_tiles, 0)),
            pl.BlockSpec((tm, LANES), lambda i: (i % pos_tiles, 0)),
            pl.BlockSpec((1, ts), const),
        ],
        out_specs=[
            pl.BlockSpec((tm, ts), row),
            pl.BlockSpec((tm, ts), row),
            pl.BlockSpec((tm, ts), row),
            pl.BlockSpec((tm, ts), row),
            pl.BlockSpec((tm, 2 * ts), row),
        ],
        out_shape=[
            jax.ShapeDtypeStruct((n, ts), BF16),
            jax.ShapeDtypeStruct((n, ts), BF16),
            jax.ShapeDtypeStruct((n, ts), BF16),
            jax.ShapeDtypeStruct((n, ts), F32),
            jax.ShapeDtypeStruct((n, 2 * ts), BF16),
        ],
        compiler_params=_cparams("arbitrary"),
        name="inproj_rope" if rope else "inproj",
    )(x, mod, nw, w, cos, sin, lb)
    return outs


def _attn_kernel(lam_ref, q_ref, *refs, post_scale, with_lat):
    if with_lat:
        kl_ref, vl_ref, kc_ref, vc_ref, nw_ref, o_ref, vce_scr, vle_scr = refs
    else:
        kc_ref, vc_ref, nw_ref, o_ref, vce_scr = refs

    @pl.when(pl.program_id(2) == 0)
    def _():
        vce_scr[:, :LANES] = vc_ref[...]
        vce_scr[:, LANES:] = jnp.ones(vc_ref.shape, BF16)
        if with_lat:
            vle_scr[:, :LANES] = vl_ref[...]
            vle_scr[:, LANES:] = jnp.ones(vl_ref.shape, BF16)

    q = q_ref[...]
    rows = q.shape[0]
    lane = lax.broadcasted_iota(jnp.int32, q.shape, 1)
    zero = jnp.zeros_like(q)
    sub1 = (lane % ATT_QK_DIM) < (ATT_QK_DIM // 2)
    qq = jnp.concatenate([jnp.where(sub1, q, zero), jnp.where(sub1, zero, q)], axis=0)

    blocks = [(kc_ref, vce_scr, 0, kc_ref.shape[0])]
    if with_lat:
        blocks += [(kl_ref, vle_scr, j, ATT_KEYS) for j in range(0, kl_ref.shape[0], ATT_KEYS)]
    m = jnp.full((2 * rows, 1), -jnp.inf, F32)
    acc = jnp.zeros((2 * rows, 2 * LANES), F32)
    for k_ref, v_scr, start, size in blocks:
        s = lax.dot_general(qq, k_ref[start:start + size, :], NT_DIMS, preferred_element_type=F32)
        m_new = jnp.maximum(m, jnp.max(s, axis=-1, keepdims=True))
        p = jnp.exp2(s - m_new).astype(BF16)
        acc = acc * jnp.exp2(m - m_new) + jnp.dot(p, v_scr[start:start + size, :], preferred_element_type=F32)
        m = m_new
    on = acc[:, :LANES] * (1.0 / acc[:, LANES:])
    o = on[:rows] - lam_ref[0] * on[rows:]
    o_ref[...] = (_rms(o) * nw_ref[...] * post_scale).astype(BF16)


def _attn_call(lam, q_src, qk_lat, vg_lat, qk_ctx, vg_ctx, nw, *, batch, tq, lam_init, with_lat):
    n_q = q_src.shape[0]
    t_q = n_q // batch
    t_lat = qk_lat.shape[0] // batch if with_lat else 0
    t_ctx = qk_ctx.shape[0] // batch
    nq_tiles = t_q // tq
    h = ATT_HEADS
    in_specs = [
        pl.BlockSpec(memory_space=pltpu.SMEM),
        pl.BlockSpec((tq, LANES), lambda b, hh, i: (b * nq_tiles + i, hh)),
    ]
    args = [lam, q_src]
    if with_lat:
        in_specs += [
            pl.BlockSpec((t_lat, LANES), lambda b, hh, i: (b, h + hh)),
            pl.BlockSpec((t_lat, LANES), lambda b, hh, i: (b, hh)),
        ]
        args += [qk_lat, vg_lat]
    in_specs += [
        pl.BlockSpec((t_ctx, LANES), lambda b, hh, i: (b, h + hh)),
        pl.BlockSpec((t_ctx, LANES), lambda b, hh, i: (b, hh)),
        pl.BlockSpec((1, LANES), lambda b, hh, i: (0, 0)),
    ]
    args += [qk_ctx, vg_ctx, nw]
    return pl.pallas_call(
        functools.partial(_attn_kernel, post_scale=1.0 - lam_init, with_lat=with_lat),
        grid=(batch, h, nq_tiles),
        in_specs=in_specs,
        out_specs=pl.BlockSpec((tq, LANES), lambda b, hh, i: (b * nq_tiles + i, hh)),
        out_shape=jax.ShapeDtypeStruct((n_q, h * LANES), BF16),
        scratch_shapes=[pltpu.VMEM((t_ctx, 2 * LANES), BF16)]
        + ([pltpu.VMEM((t_lat, 2 * LANES), BF16)] if with_lat else []),
        compiler_params=_cparams("arbitrary", "arbitrary", "arbitrary"),
        name="attn_lat" if with_lat else "attn_ctx",
    )(*args)


def _gla_constants():
    n = GLA_BLOCK
    idx = np.arange(n)
    same_chunk = (idx[:, None] // GLA_CHUNK) == (idx[None, :] // GLA_CHUNK)
    stacks, masks = [], []
    for reverse in (False, True):
        order = (idx[None, :] >= idx[:, None]) if reverse else (idx[None, :] <= idx[:, None])
        cum = (same_chunk & order).astype(np.int32)
        groups, lvl_masks = [cum], []
        for m in GLA_LEVELS:
            ref = (idx // (2 * m)) * (2 * m) + (m if reverse else m - 1)
            groups.append(np.abs(cum - cum[ref]))
            upper = (idx % (2 * m)) >= m
            same = (idx[:, None] // (2 * m)) == (idx[None, :] // (2 * m))
            q_side, k_side = (~upper, upper) if reverse else (upper, ~upper)
            lvl_masks.append(same & q_side[:, None] & k_side[None, :])
        end = (idx // GLA_CHUNK) * GLA_CHUNK + (0 if reverse else GLA_CHUNK - 1)
        groups.append(np.abs(cum[end] - cum))
        stacks.append(np.concatenate(groups, axis=0))
        masks.append(np.stack(lvl_masks))
    return jnp.asarray(np.stack(stacks), BF16), jnp.asarray(np.stack(masks), F32)


def _gla_prepare(q, v, lf, stack, *, reverse):
    n = q.shape[0]
    qf = q.astype(F32)
    k = 1.0 - jnp.exp(lf)

    hi = lf.astype(BF16)
    mid = (lf - hi.astype(F32)).astype(BF16)
    expo = jnp.dot(stack, jnp.concatenate([hi, mid], axis=1), preferred_element_type=F32)

    def exponent(group):
        blk = expo[group * n:(group + 1) * n]
        return blk[:, :LANES] + blk[:, LANES:]

    scale = [jnp.exp(exponent(1 + li)) for li in range(len(GLA_LEVELS))]
    qs = jnp.concatenate([(qf * e).astype(BF16) for e in scale], axis=1)
    ks = jnp.concatenate([(k * e).astype(BF16) for e in scale], axis=1)
    c = exponent(0)
    io = jnp.concatenate([(qf * jnp.exp(c)).astype(BF16),
                          (k * jnp.exp(exponent(1 + len(GLA_LEVELS)))).astype(BF16)], axis=1)
    diag = jnp.sum(qf * k, axis=-1, keepdims=True) * v.astype(F32)
    last = 0 if reverse else GLA_CHUNK - 1
    n_chunks = n // GLA_CHUNK
    decay = jnp.concatenate([jnp.exp(c[ci * GLA_CHUNK + last:ci * GLA_CHUNK + last + 1, :])
                             for ci in range(n_chunks)] + [jnp.ones((8 - n_chunks, LANES), F32)], axis=0)
    return qs, ks, io, diag, decay


def _gla_scan(qs, ks, io, diag, decay, v, st, masks, *, reverse):
    n = v.shape[0]
    scores = jnp.zeros((n, n), F32)
    for li in range(len(GLA_LEVELS)):
        cols = slice(li * LANES, (li + 1) * LANES)
        s_l = lax.dot_general(qs[:, cols], ks[:, cols], NT_DIMS, preferred_element_type=F32)
        scores = scores + s_l * masks[li]
    o = jnp.dot(scores.astype(BF16), v, preferred_element_type=F32) + diag

    n_chunks = n // GLA_CHUNK
    outs = [None] * n_chunks
    order = range(n_chunks - 1, -1, -1) if reverse else range(n_chunks)
    for ci in order:
        r = slice(ci * GLA_CHUNK, (ci + 1) * GLA_CHUNK)
        outs[ci] = lax.dot_general(io[r, :LANES], st.astype(BF16), NT_DIMS, preferred_element_type=F32)
        upd = lax.dot_general(v[r], io[r, LANES:], TN_DIMS, preferred_element_type=F32)
        st = st * decay[ci:ci + 1, :] + upd
    return o + jnp.concatenate(outs, axis=0), st


def _gla_kernel(ql_ref, il_ref, ffl_ref, fbl_ref, gl_ref, qc_ref, ic_ref, ffc_ref, fbc_ref, gc_ref,
                nw_ref, stack_ref, mask_ref, ol_ref, oc_ref,
                qs_scr, ks_scr, io_scr, dg_scr, dc_scr, of_scr, ob_scr):
    n_blocks = ql_ref.shape[0] // GLA_BLOCK
    nw = nw_ref[...]

    def finish(o, g):
        return (_rms(o) * nw * g.astype(F32)).astype(BF16)

    def rows(i):
        return pl.ds(pl.multiple_of(i * GLA_BLOCK, GLA_BLOCK), GLA_BLOCK)

    def prepare(slot, q, v, lf_f, lf_b):
        for d, lf in enumerate((lf_f, lf_b)):
            qs, ks, io, diag, decay = _gla_prepare(q, v, lf, stack_ref[d], reverse=bool(d))
            qs_scr[d, slot] = qs
            ks_scr[d, slot] = ks
            io_scr[d, slot] = io
            dg_scr[d, slot] = diag
            dc_scr[d, slot] = decay

    def scan(d, slot, v, st):
        return _gla_scan(qs_scr[d, slot], ks_scr[d, slot], io_scr[d, slot], dg_scr[d, slot], dc_scr[d, slot],
                         v, st, mask_ref[d], reverse=bool(d))

    prepare(0, qc_ref[...], ic_ref[...], ffc_ref[...], fbc_ref[...])

    def prep(i, carry):
        r = rows(i)
        prepare(i + 1, ql_ref[r, :], il_ref[r, :], ffl_ref[r, :], fbl_ref[r, :])
        return carry

    lax.fori_loop(0, n_blocks, prep, 0)

    zero_state = jnp.zeros((LANES, LANES), F32)
    o_cf, st_f = scan(0, 0, ic_ref[...], zero_state)
    o_cb, st_b = scan(1, 0, ic_ref[...], zero_state)
    oc_ref[...] = finish(o_cf + o_cb, gc_ref[...])

    def step(i, carry):
        st_f, st_b = carry
        j = n_blocks - 1 - i
        o_f, st_f = scan(0, i + 1, il_ref[rows(i), :], st_f)
        o_b, st_b = scan(1, j + 1, il_ref[rows(j), :], st_b)
        of_scr[rows(i), :] = o_f
        ob_scr[rows(j), :] = o_b
        return st_f, st_b

    lax.fori_loop(0, n_blocks, step, (st_f, st_b))

    def emit(i, carry):
        r = rows(i)
        ol_ref[r, :] = finish(of_scr[r, :] + ob_scr[r, :], gl_ref[r, :])
        return carry

    lax.fori_loop(0, n_blocks, emit, 0)


def _gla_call(qi_lat, lf_lat, vg_lat, qi_ctx, lf_ctx, vg_ctx, nw, *, batch):
    t_lat = qi_lat.shape[0] // batch
    t_ctx = qi_ctx.shape[0] // batch
    assert t_ctx == GLA_BLOCK and t_lat % GLA_BLOCK == 0
    h = REC_HEADS
    stack, masks = _gla_constants()
    n_slots = 1 + t_lat // GLA_BLOCK
    n_lvl = len(GLA_LEVELS)

    def col(off):
        return lambda b, hh: (b, off + hh)

    def seq(t):
        return [pl.BlockSpec((t, LANES), col(0)), pl.BlockSpec((t, LANES), col(h)),
                pl.BlockSpec((t, LANES), col(0)), pl.BlockSpec((t, LANES), col(h)),
                pl.BlockSpec((t, LANES), col(h))]

    return pl.pallas_call(
        _gla_kernel,
        grid=(batch, h),
        in_specs=seq(t_lat) + seq(t_ctx) + [
            pl.BlockSpec((1, LANES), lambda b, hh: (0, 0)),
            pl.BlockSpec(stack.shape, lambda b, hh: (0, 0, 0)),
            pl.BlockSpec(masks.shape, lambda b, hh: (0, 0, 0, 0)),
        ],
        out_specs=[pl.BlockSpec((t_lat, LANES), col(0)), pl.BlockSpec((t_ctx, LANES), col(0))],
        out_shape=[jax.ShapeDtypeStruct((batch * t_lat, h * LANES), BF16),
                   jax.ShapeDtypeStruct((batch * t_ctx, h * LANES), BF16)],
        scratch_shapes=[
            pltpu.VMEM((2, n_slots, GLA_BLOCK, n_lvl * LANES), BF16),
            pltpu.VMEM((2, n_slots, GLA_BLOCK, n_lvl * LANES), BF16),
            pltpu.VMEM((2, n_slots, GLA_BLOCK, 2 * LANES), BF16),
            pltpu.VMEM((2, n_slots, GLA_BLOCK, LANES), F32),
            pltpu.VMEM((2, n_slots, 8, LANES), F32),
            pltpu.VMEM((t_lat, LANES), F32), pltpu.VMEM((t_lat, LANES), F32),
        ],
        compiler_params=_cparams("arbitrary", "arbitrary"),
        name="gla",
    )(qi_lat, qi_lat, lf_lat, lf_lat, vg_lat, qi_ctx, qi_ctx, lf_ctx, lf_ctx, vg_ctx, nw, stack, masks)


def _top2_gates(logits):
    lane = lax.broadcasted_iota(jnp.int32, logits.shape, 1).astype(F32)
    big = float(LANES)
    m1 = jnp.max(logits, axis=-1, keepdims=True)
    i1 = jnp.min(jnp.where(logits == m1, lane, big), axis=-1, keepdims=True)
    rest = jnp.where(lane == i1, -jnp.inf, logits)
    m2 = jnp.max(rest, axis=-1, keepdims=True)
    i2 = jnp.min(jnp.where(rest == m2, lane, big), axis=-1, keepdims=True)
    e = jnp.exp(m2 - m1)
    w1 = 1.0 / (1.0 + e)
    return jnp.where(lane == i1, w1, 0.0) + jnp.where(lane == i2, e * w1, 0.0)


def _outproj_kernel(ya_ref, yr_ref, g_ref, x_ref, mod_ref, wua_ref, wur_ref, wo_ref, nw2_ref, *refs,
                    router, tiles_per_block):
    if router:
        rw_ref, tri_ref, xo_ref, h2_ref, gate_ref, pos_ref, cnt_ref, carry_scr = refs
    else:
        xo_ref, h2_ref = refs
    d = x_ref.shape[1]
    ua = jnp.dot(ya_ref[...], wua_ref[...], preferred_element_type=F32)
    ur = jnp.dot(yr_ref[...], wur_ref[...], preferred_element_type=F32)
    u = g_ref[:, :d].astype(F32) * ua + g_ref[:, d:].astype(F32) * ur
    y = jnp.dot(u.astype(BF16), wo_ref[...], preferred_element_type=F32)
    xn = x_ref[...] + mod_ref[2:3, :] * y
    xo_ref[...] = xn
    h2 = (_rms(xn) * nw2_ref[...]) * (1.0 + mod_ref[4:5, :]) + mod_ref[3:4, :]
    h2_hi = h2.astype(BF16)
    h2_ref[...] = h2_hi
    if router:
        h2_lo = (h2 - h2_hi.astype(F32)).astype(BF16)
        logits = (jnp.dot(h2_hi, rw_ref[0], preferred_element_type=F32)
                  + jnp.dot(h2_lo, rw_ref[0], preferred_element_type=F32)
                  + jnp.dot(h2_hi, rw_ref[1], preferred_element_type=F32))
        lane = lax.broadcasted_iota(jnp.int32, logits.shape, 1)
        gates = _top2_gates(jnp.where(lane < N_EXPERTS, logits, -jnp.inf))
        gate_ref[...] = gates

        @pl.when(pl.program_id(0) % tiles_per_block == 0)
        def _():
            carry_scr[...] = jnp.zeros_like(carry_scr)

        sel = gates > 0.0
        sel_f = jnp.where(sel, 1.0, 0.0)
        rank = jnp.dot(tri_ref[...], sel_f.astype(BF16), preferred_element_type=F32) + carry_scr[...]
        pos_ref[...] = jnp.where(sel, rank, -1.0)
        total = carry_scr[...] + jnp.sum(sel_f, axis=0, keepdims=True)
        carry_scr[...] = total
        cnt_ref[...] = total


def _outproj_call(ya, yr, g, x, mod, wua, wur, wo, nw2, rw, *, tm, rows_per_mod, route_block=None):
    n, d = x.shape
    router = rw is not None
    row = lambda i: (i, 0)
    const = lambda i: (0, 0)
    tiles_per_block = route_block // tm if router else 1
    if rows_per_mod is None:
        mod_map = lambda i: (mod.shape[0] - 1, 0, 0)
    else:
        mod_map = lambda i: (i // (rows_per_mod // tm), 0, 0)
    in_specs = [
        pl.BlockSpec((tm, ya.shape[1]), row),
        pl.BlockSpec((tm, yr.shape[1]), row),
        pl.BlockSpec((tm, 2 * d), row),
        pl.BlockSpec((tm, d), row),
        pl.BlockSpec((None, 6, d), mod_map),
        pl.BlockSpec(wua.shape, const),
        pl.BlockSpec(wur.shape, const),
        pl.BlockSpec(wo.shape, const),
        pl.BlockSpec((1, d), const),
    ]
    args = [ya, yr, g, x, mod, wua, wur, wo, nw2]
    out_specs = [pl.BlockSpec((tm, d), row), pl.BlockSpec((tm, d), row)]
    out_shape = [jax.ShapeDtypeStruct((n, d), F32), jax.ShapeDtypeStruct((n, d), BF16)]
    scratch = []
    if router:
        tri = jnp.asarray(np.tril(np.ones((tm, tm), np.float32), -1), BF16)
        in_specs += [pl.BlockSpec(rw.shape, lambda i: (0, 0, 0)), pl.BlockSpec(tri.shape, const)]
        args += [rw, tri]
        out_specs += [pl.BlockSpec((tm, LANES), row), pl.BlockSpec((tm, LANES), row),
                      pl.BlockSpec((None, 1, LANES), lambda i: (i // tiles_per_block, 0, 0))]
        out_shape += [jax.ShapeDtypeStruct((n, LANES), F32), jax.ShapeDtypeStruct((n, LANES), F32),
                      jax.ShapeDtypeStruct((n // route_block, 1, LANES), F32)]
        scratch = [pltpu.VMEM((1, LANES), F32)]
    return pl.pallas_call(
        functools.partial(_outproj_kernel, router=router, tiles_per_block=tiles_per_block),
        grid=(n // tm,),
        in_specs=in_specs,
        out_specs=out_specs,
        out_shape=out_shape,
        scratch_shapes=scratch,
        compiler_params=_cparams("arbitrary"),
        name="outproj_router" if router else "outproj",
    )(*args)


def _swiglu_chunk(xs, w1_ref, w3_ref, w2_ref):
    a = jnp.dot(xs, w1_ref[...], preferred_element_type=F32)
    b = jnp.dot(xs, w3_ref[...], preferred_element_type=F32)
    return jnp.dot((_silu(a) * b).astype(BF16), w2_ref[...], preferred_element_type=F32)


def _ffn_kernel(h_ref, x_ref, mod_ref, w1_ref, w3_ref, w2_ref, o_ref, *, tf):
    h = h_ref[...]
    ff = w1_ref.shape[1]
    acc = jnp.zeros(o_ref.shape, F32)
    for lo in range(0, ff, tf):
        hi = min(lo + tf, ff)
        a = jnp.dot(h, w1_ref[:, lo:hi], preferred_element_type=F32)
        b = jnp.dot(h, w3_ref[:, lo:hi], preferred_element_type=F32)
        acc = acc + jnp.dot((_silu(a) * b).astype(BF16), w2_ref[lo:hi, :], preferred_element_type=F32)
    o_ref[...] = x_ref[...] + mod_ref[5:6, :] * acc


def _ffn_call(h, x, mod, w1, w3, w2, *, tm, tf, rows_per_mod):
    n, d = x.shape
    row = lambda i: (i, 0)
    const = lambda i: (0, 0)
    if rows_per_mod is None:
        mod_map = lambda i: (mod.shape[0] - 1, 0, 0)
    else:
        mod_map = lambda i: (i // (rows_per_mod // tm), 0, 0)
    return pl.pallas_call(
        functools.partial(_ffn_kernel, tf=tf),
        grid=(n // tm,),
        in_specs=[
            pl.BlockSpec((tm, d), row), pl.BlockSpec((tm, d), row), pl.BlockSpec((None, 6, d), mod_map),
            pl.BlockSpec(w1.shape, const), pl.BlockSpec(w3.shape, const), pl.BlockSpec(w2.shape, const),
        ],
        out_specs=pl.BlockSpec((tm, d), row),
        out_shape=jax.ShapeDtypeStruct((n, d), F32),
        compiler_params=_cparams("arbitrary"),
        name="ffn",
    )(h, x, mod, w1, w3, w2)


def _moe_kernel(cnt_ref, h_ref, pos_ref, gate_ref, w1_ref, w3_ref, w2_ref, o_ref, xg_scr, y_scr, gr_scr):
    b, e, f = pl.program_id(0), pl.program_id(1), pl.program_id(2)
    n_e, n_f = pl.num_programs(1), pl.num_programs(2)
    tb = h_ref.shape[0]
    pos_row = pos_ref[pl.ds(e, 1), :]
    half = MOE_ROWS // 2
    n_half = (cnt_ref[b * n_e + e] + half - 1) // half
    n_full = n_half // 2
    has_half = n_half % 2 == 1
    tail = pl.multiple_of(n_full * MOE_ROWS, MOE_ROWS)

    def one_hot(start, r):
        slot = lax.broadcasted_iota(jnp.int32, (r, tb), 0) + start
        return pos_row == slot.astype(F32)

    def for_tiles(fn, rows_per_step):
        k = rows_per_step // MOE_ROWS
        n_steps = n_full // k

        def body(s, carry):
            fn(pl.multiple_of(s * rows_per_step, rows_per_step), rows_per_step)
            return carry

        lax.fori_loop(0, n_steps, body, 0)
        if k > 1:
            @pl.when(n_full % k == 1)
            def _():
                fn(pl.multiple_of(n_steps * rows_per_step, rows_per_step), MOE_ROWS)

        @pl.when(has_half)
        def _():
            fn(tail, half)

    @pl.when((e == 0) & (f == 0))
    def _():
        o_ref[...] = jnp.zeros_like(o_ref)

    @pl.when(f == 0)
    def _():
        gate_row = gate_ref[pl.ds(e, 1), :]

        def gather(start, r):
            hit = one_hot(start, r)
            rows = pl.ds(start, r)
            xg_scr[rows, :] = jnp.dot(jnp.where(hit, 1.0, 0.0).astype(BF16), h_ref[...],
                                      preferred_element_type=F32).astype(BF16)
            gr_scr[rows, :] = jnp.sum(jnp.where(hit, gate_row, 0.0), axis=-1, keepdims=True)
            y_scr[rows, :] = jnp.zeros((r, y_scr.shape[1]), F32)

        for_tiles(gather, MOE_ROWS)

    def expert(start, r):
        rows = pl.ds(start, r)
        y_scr[rows, :] += _swiglu_chunk(xg_scr[rows, :], w1_ref, w3_ref, w2_ref)

    for_tiles(expert, 2 * MOE_ROWS)

    @pl.when(f == n_f - 1)
    def _():
        def scatter(start, r):
            rows = pl.ds(start, r)
            y = (y_scr[rows, :] * gr_scr[rows, :]).astype(BF16)
            o_ref[...] += lax.dot_general(jnp.where(one_hot(start, r), 1.0, 0.0).astype(BF16), y, TN_DIMS,
                                          preferred_element_type=F32)

        for_tiles(scatter, MOE_ROWS)


def _moe_call(cnt, h, pos_t, gate_t, w1, w3, w2, *, tb, tf):
    n, d = h.shape
    n_e, _, ff = w1.shape
    grid_spec = pltpu.PrefetchScalarGridSpec(
        num_scalar_prefetch=1,
        grid=(n // tb, n_e, ff // tf),
        in_specs=[
            pl.BlockSpec((tb, d), lambda b, e, f, c: (b, 0)),
            pl.BlockSpec((n_e, tb), lambda b, e, f, c: (0, b)),
            pl.BlockSpec((n_e, tb), lambda b, e, f, c: (0, b)),
            pl.BlockSpec((None, d, tf), lambda b, e, f, c: (e, 0, f)),
            pl.BlockSpec((None, d, tf), lambda b, e, f, c: (e, 0, f)),
            pl.BlockSpec((None, tf, d), lambda b, e, f, c: (e, f, 0)),
        ],
        out_specs=pl.BlockSpec((tb, d), lambda b, e, f, c: (b, 0)),
        scratch_shapes=[pltpu.VMEM((tb, d), BF16), pltpu.VMEM((tb, d), F32), pltpu.VMEM((tb, 1), F32)],
    )
    return pl.pallas_call(
        _moe_kernel,
        grid_spec=grid_spec,
        out_shape=jax.ShapeDtypeStruct((n, d), F32),
        compiler_params=pltpu.CompilerParams(dimension_semantics=("arbitrary",) * 3,
                                             vmem_limit_bytes=MOE_VMEM_LIMIT),
        name="moe",
    )(cnt, h, pos_t, gate_t, w1, w3, w2)


def _resnorm_kernel(x_ref, y_ref, mod_ref, fnw_ref, o_ref):
    o_ref[...] = _rms(x_ref[...] + mod_ref[5:6, :] * y_ref[...]) * fnw_ref[...]


def _resnorm_call(x, y, mod, fnw, *, tm, rows_per_mod):
    n, d = x.shape
    row = lambda i: (i, 0)
    return pl.pallas_call(
        _resnorm_kernel,
        grid=(n // tm,),
        in_specs=[pl.BlockSpec((tm, d), row), pl.BlockSpec((tm, d), row),
                  pl.BlockSpec((None, 6, d), lambda i: (i // (rows_per_mod // tm), 0, 0)),
                  pl.BlockSpec((1, d), lambda i: (0, 0))],
        out_specs=pl.BlockSpec((tm, d), row),
        out_shape=jax.ShapeDtypeStruct((n, d), F32),
        compiler_params=_cparams("arbitrary"),
        name="resnorm",
    )(x, y, mod, fnw)


def _rope_tables(t_lat):
    rows = t_lat // GRID_W
    row = jnp.repeat(jnp.arange(rows, dtype=F32), GRID_W)
    col = jnp.tile(jnp.arange(GRID_W, dtype=F32), rows)
    n_freq = ATT_QK_DIM // 4
    inv_freq = ROPE_THETA ** (-jnp.arange(n_freq, dtype=F32) / n_freq)
    ang = jnp.concatenate([row[:, None] * inv_freq, col[:, None] * inv_freq], axis=-1)
    cos, sin = jnp.cos(ang), jnp.sin(ang)
    return (jnp.concatenate([cos, cos, cos, cos], axis=-1), jnp.concatenate([-sin, -sin, sin, sin], axis=-1))


def _layer_lower_bounds(lb_param):
    cs = jnp.cumsum(jax.nn.softmax(lb_param.astype(F32), axis=0), axis=0)
    return cs - cs[0:1]


def _win_columns(w_in_l):
    c = [w_in_l[:, i * 512:(i + 1) * 512] for i in range(8)]
    half = ATT_QK_DIM // 2

    def pair_halves(w):
        w = w.reshape(w.shape[0], ATT_HEADS, 2, 2, half)
        return w.transpose(0, 1, 3, 2, 4).reshape(w.shape[0], -1)

    return jnp.concatenate([pair_halves(c[0]), pair_halves(c[1]), c[2], c[7], c[3], c[4], c[5], c[6],
                            w_in_l[:, 4096:]], axis=1).astype(BF16)


def _pad_ff(w, axis, mult):
    ff = w.shape[axis]
    pad = (-ff) % mult
    if pad == 0:
        return w
    widths = [(0, 0)] * w.ndim
    widths[axis] = (0, pad)
    return jnp.pad(w, widths)


def kernel(x, c, ctx, c_ctx, w_ada, b_ada, norm_mix_w, norm_ffn_w, w_in, lambda_q1, lambda_k1, lambda_q2,
           lambda_k2, att_norm_w, rec_norm_w, lb_fwd, lb_bwd, w_up_att, w_up_rec, w_out, ffn_w1, ffn_w3,
           ffn_w2, router_w, moe_w1, moe_w3, moe_w2, final_norm_w):
    batch, t_lat, d = x.shape
    t_ctx = ctx.shape[1]
    depth = w_ada.shape[0]
    n_lat, n_ctx = batch * t_lat, batch * t_ctx
    tm = 512
    tm_in = 512
    tq = min(512, t_lat)

    xl = x.reshape(n_lat, d)
    xc = ctx.reshape(n_ctx, d)

    pad_rows = (-(batch + 1)) % 8
    cc = jnp.concatenate([c, jnp.zeros((pad_rows, d), F32), c_ctx[None, :]], axis=0)
    mod_all = _mod_call(cc, w_ada, b_ada).reshape(depth, cc.shape[0], 6, d)

    cos, sin = _rope_tables(t_lat)
    lbs_f = _layer_lower_bounds(lb_fwd)
    lbs_b = _layer_lower_bounds(lb_bwd)

    for l in range(depth):
        last = l == depth - 1
        mod = mod_all[l]
        lam_init = 0.8 - 0.6 * math.exp(-0.3 * l)
        lam = (jnp.exp(jnp.sum(lambda_q1[l] * lambda_k1[l])) - jnp.exp(jnp.sum(lambda_q2[l] * lambda_k2[l]))
               + lam_init).reshape(1).astype(F32)
        lb = jnp.concatenate([lbs_f[l], lbs_b[l]])[None, :]
        w_l = _win_columns(w_in[l])
        nw = norm_mix_w[l][None, :]

        pl_lat = _inproj_call(xl, mod, nw, w_l, cos, sin, lb, tm=tm_in, rows_per_mod=t_lat, rope=True)
        pl_ctx = _inproj_call(xc, mod, nw, w_l, cos, sin, lb, tm=min(tm_in, n_ctx), rows_per_mod=None,
                              rope=False)
        qk_l, vg_l, qi_l, lf_l, g_l = pl_lat
        qk_c, vg_c, qi_c, lf_c, g_c = pl_ctx

        anw = att_norm_w[l][None, :]
        ya_l = _attn_call(lam, qk_l, qk_l, vg_l, qk_c, vg_c, anw, batch=batch, tq=tq,
                          lam_init=lam_init, with_lat=True)
        yr_l, yr_c = _gla_call(qi_l, lf_l, vg_l, qi_c, lf_c, vg_c, rec_norm_w[l][None, :], batch=batch)

        wua = w_up_att[l].astype(BF16)
        wur = w_up_rec[l].astype(BF16)
        wo = w_out[l].astype(BF16)
        nw2 = norm_ffn_w[l][None, :]
        moe_layer = l % 2 == 1
        j = l // 2
        assert moe_layer == last
        rw = None
        if moe_layer:
            rw32 = jnp.pad(router_w[j], ((0, 0), (0, LANES - N_EXPERTS)))
            rw_hi = rw32.astype(BF16)
            rw = jnp.stack([rw_hi, (rw32 - rw_hi.astype(F32)).astype(BF16)])
        res = _outproj_call(ya_l, yr_l, g_l, xl, mod, wua, wur, wo, nw2, rw, tm=tm, rows_per_mod=t_lat,
                            route_block=t_lat)
        xl, h2_l = res[0], res[1]
        if not last:
            ya_c = _attn_call(lam, qk_c, None, None, qk_c, vg_c, anw, batch=batch, tq=t_ctx,
                              lam_init=lam_init, with_lat=False)
            xc, h2_c = _outproj_call(ya_c, yr_c, g_c, xc, mod, wua, wur, wo, nw2, None, tm=tm, rows_per_mod=None)

        if moe_layer:
            gates, pos, cnt = res[2], res[3], res[4]
            cnt = cnt[:, 0, :N_EXPERTS].astype(jnp.int32).reshape(-1)
            y = _moe_call(cnt, h2_l, pos[:, :N_EXPERTS].T, gates[:, :N_EXPERTS].T,
                          moe_w1[j].astype(BF16), moe_w3[j].astype(BF16), moe_w2[j].astype(BF16),
                          tb=t_lat, tf=512)
            xl = _resnorm_call(xl, y, mod, final_norm_w[None, :], tm=tm, rows_per_mod=t_lat)
        else:
            w1 = _pad_ff(ffn_w1[j], 1, 2 * LANES).astype(BF16)
            w3 = _pad_ff(ffn_w3[j], 1, 2 * LANES).astype(BF16)
            w2 = _pad_ff(ffn_w2[j], 0, 2 * LANES).astype(BF16)
            tf = 512
            xl = _ffn_call(h2_l, xl, mod, w1, w3, w2, tm=tm, tf=tf, rows_per_mod=t_lat)
            xc = _ffn_call(h2_c, xc, mod, w1, w3, w2, tm=tm, tf=tf, rows_per_mod=None)

    return xl.reshape(batch, t_lat, d)
```

```python
import functools
import math

import numpy as np
import jax
import jax.numpy as jnp
from jax import lax
from jax.experimental import pallas as pl
from jax.experimental.pallas import tpu as pltpu
from jax.experimental.pallas import tpu_sc as plsc

F32 = jnp.float32
BF16 = jnp.bfloat16
HIGHEST = lax.Precision.HIGHEST

EPS = 1e-6
GRID_W = 64
ROPE_THETA = 10000.0
ATT_HEADS = 4
ATT_QK_DIM = 64
REC_HEADS = 4
N_EXPERTS = 8
TOP_K = 2
Q_SCALE = ATT_QK_DIM ** -0.5 * math.log2(math.e)

LANES = 128
GLA_BLOCK = 256
GLA_CHUNK = 256
GLA_UNROLL = 4
GLA_LEVELS = tuple(2 ** i for i in range(GLA_CHUNK.bit_length() - 1))
VMEM_LIMIT = 56 * 1024 * 1024
MOE_TILE = 1024
SC_WINDOW = 128
ATT_KEYS = 256
ATT_ROWS = 512

NT_DIMS = (((1,), (1,)), ((), ()))
TN_DIMS = (((0,), (0,)), ((), ()))


def _cparams(*sem):
    return pltpu.CompilerParams(dimension_semantics=sem, vmem_limit_bytes=VMEM_LIMIT)


def _silu(a):
    return a * jax.nn.sigmoid(a)


def _rms(x):
    return x * lax.rsqrt(jnp.mean(x * x, axis=-1, keepdims=True) + EPS)


def _mod_kernel(c_ref, w_ref, b_ref, o_ref):
    s = _silu(c_ref[...])
    o_ref[...] = jnp.dot(s, w_ref[...], precision=HIGHEST, preferred_element_type=F32) + b_ref[...]


def _mod_call(cc, w_ada, b_ada):
    depth, d, n = w_ada.shape
    rows = cc.shape[0]
    tn = 1536
    return pl.pallas_call(
        _mod_kernel,
        grid=(depth, n // tn),
        in_specs=[
            pl.BlockSpec((rows, d), lambda l, j: (0, 0)),
            pl.BlockSpec((None, d, tn), lambda l, j: (l, 0, j)),
            pl.BlockSpec((None, 1, tn), lambda l, j: (l, 0, j)),
        ],
        out_specs=pl.BlockSpec((None, rows, tn), lambda l, j: (l, 0, j)),
        out_shape=jax.ShapeDtypeStruct((depth, rows, n), F32),
        compiler_params=_cparams("arbitrary", "arbitrary"),
        name="mod",
    )(cc, w_ada, b_ada.reshape(depth, 1, n))


def _log_forget(z, lb):
    t = jnp.exp(-jnp.abs(z))
    num = jnp.where(z >= 0.0, 1.0 + lb * t, lb + t)
    return jnp.where(num > 0.0, jnp.log(num / (1.0 + t)), z)


def _inproj_kernel(x_ref, mod_ref, nw_ref, w_ref, cos_ref, sin_ref, lb_ref,
                   qk_ref, vg_ref, qi_ref, lf_ref, g_ref, *, rope):
    h = _rms(x_ref[...]) * nw_ref[...]
    h = (h * (1.0 + mod_ref[1:2, :]) + mod_ref[0:1, :]).astype(BF16)
    ts = qk_ref.shape[1]
    half = ts // 2

    def proj(s):
        return jnp.dot(h, w_ref[:, s * ts:(s + 1) * ts], preferred_element_type=F32)

    p = proj(0)
    if rope:
        cos = cos_ref[...]
        sin = sin_ref[...]
    for j in range(ts // LANES):
        blk = p[:, j * LANES:(j + 1) * LANES]
        if rope:
            blk = blk * cos + pltpu.roll(blk, LANES // 2, 1) * sin
        if j * LANES < half:
            blk = blk * Q_SCALE
        qk_ref[:, j * LANES:(j + 1) * LANES] = blk.astype(BF16)

    p = proj(1)
    vg_ref[:, :half] = p[:, :half].astype(BF16)
    vg_ref[:, half:] = _silu(p[:, half:]).astype(BF16)

    p = proj(2)
    qi_ref[:, :half] = _silu(p[:, :half]).astype(BF16)
    qi_ref[:, half:] = p[:, half:].astype(BF16)

    lf_ref[...] = _log_forget(proj(3), lb_ref[...])

    g_ref[:, :ts] = jax.nn.sigmoid(proj(4)).astype(BF16)
    g_ref[:, ts:] = jax.nn.sigmoid(proj(5)).astype(BF16)


def _inproj_call(x, mod, nw, w, cos, sin, lb, *, tm, rows_per_mod, rope):
    n, d = x.shape
    assert n % tm == 0, (n, tm)
    ts = 1024
    assert w.shape[1] == 6 * ts
    pos_tiles = cos.shape[0] // tm
    row = lambda i: (i, 0)
    const = lambda i: (0, 0)
    if rows_per_mod is None:
        mod_map = lambda i: (mod.shape[0] - 1, 0, 0)
    else:
        mod_map = lambda i: (i // (rows_per_mod // tm), 0, 0)
    outs = pl.pallas_call(
        functools.partial(_inproj_kernel, rope=rope),
        grid=(n // tm,),
        in_specs=[
            pl.BlockSpec((tm, d), row),
            pl.BlockSpec((None, 6, d), mod_map),
            pl.BlockSpec((1, d), const),
            pl.BlockSpec(w.shape, const),
            pl.BlockSpec((tm, LANES), lambda i: (i % pos_tiles, 0)),
            pl.BlockSpec((tm, LANES), lambda i: (i % pos_tiles, 0)),
            pl.BlockSpec((1, ts), const),
        ],
        out_specs=[
            pl.BlockSpec((tm, ts), row),
            pl.BlockSpec((tm, ts), row),
            pl.BlockSpec((tm, ts), row),
            pl.BlockSpec((tm, ts), row),
            pl.BlockSpec((tm, 2 * ts), row),
        ],
        out_shape=[
            jax.ShapeDtypeStruct((n, ts), BF16),
            jax.ShapeDtypeStruct((n, ts), BF16),
            jax.ShapeDtypeStruct((n, ts), BF16),
            jax.ShapeDtypeStruct((n, ts), F32),
            jax.ShapeDtypeStruct((n, 2 * ts), BF16),
        ],
        compiler_params=_cparams("arbitrary"),
        name="inproj_rope" if rope else "inproj",
    )(x, mod, nw, w, cos, sin, lb)
    return outs


def _attn_kernel(lam_ref, q_ref, *refs, post_scale, with_lat):
    if with_lat:
        kl_ref, vl_ref, kc_ref, vc_ref, nw_ref, o_ref, vce_scr, vle_scr = refs
    else:
        kc_ref, vc_ref, nw_ref, o_ref, vce_scr = refs

    @pl.when(pl.program_id(2) == 0)
    def _():
        vce_scr[:, :LANES] = vc_ref[...]
        vce_scr[:, LANES:] = jnp.ones(vc_ref.shape, BF16)
        if with_lat:
            vle_scr[:, :LANES] = vl_ref[...]
            vle_scr[:, LANES:] = jnp.ones(vl_ref.shape, BF16)

    n_parts = max(1, q_ref.shape[0] // ATT_ROWS)
    rows = q_ref.shape[0] // n_parts
    qq = []
    for part in range(n_parts):
        q = q_ref[part * rows:(part + 1) * rows, :]
        lane = lax.broadcasted_iota(jnp.int32, q.shape, 1)
        zero = jnp.zeros_like(q)
        sub1 = (lane % ATT_QK_DIM) < (ATT_QK_DIM // 2)
        qq.append(jnp.concatenate([jnp.where(sub1, q, zero), jnp.where(sub1, zero, q)], axis=0))

    blocks = [(kc_ref, vce_scr, 0, kc_ref.shape[0])]
    if with_lat:
        blocks += [(kl_ref, vle_scr, j, ATT_KEYS) for j in range(0, kl_ref.shape[0], ATT_KEYS)]
    m = [jnp.full((2 * rows, 1), -jnp.inf, F32)] * n_parts
    acc = [jnp.zeros((2 * rows, 2 * LANES), F32)] * n_parts
    for k_ref, v_scr, start, size in blocks:
        for part in range(n_parts):
            s = lax.dot_general(qq[part], k_ref[start:start + size, :], NT_DIMS, preferred_element_type=F32)
            m_new = jnp.maximum(m[part], jnp.max(s, axis=-1, keepdims=True))
            p = jnp.exp2(s - m_new).astype(BF16)
            acc[part] = acc[part] * jnp.exp2(m[part] - m_new) + jnp.dot(
                p, v_scr[start:start + size, :], preferred_element_type=F32)
            m[part] = m_new
    for part in range(n_parts):
        on = acc[part][:, :LANES] * (1.0 / acc[part][:, LANES:])
        o = on[:rows] - lam_ref[0] * on[rows:]
        o_ref[part * rows:(part + 1) * rows, :] = (_rms(o) * nw_ref[...] * post_scale).astype(BF16)


def _attn_call(lam, q_src, qk_lat, vg_lat, qk_ctx, vg_ctx, nw, *, batch, tq, lam_init, with_lat):
    n_q = q_src.shape[0]
    t_q = n_q // batch
    t_lat = qk_lat.shape[0] // batch if with_lat else 0
    t_ctx = qk_ctx.shape[0] // batch
    nq_tiles = t_q // tq
    h = ATT_HEADS
    in_specs = [
        pl.BlockSpec(memory_space=pltpu.SMEM),
        pl.BlockSpec((tq, LANES), lambda b, hh, i: (b * nq_tiles + i, hh)),
    ]
    args = [lam, q_src]
    if with_lat:
        in_specs += [
            pl.BlockSpec((t_lat, LANES), lambda b, hh, i: (b, h + hh)),
            pl.BlockSpec((t_lat, LANES), lambda b, hh, i: (b, hh)),
        ]
        args += [qk_lat, vg_lat]
    in_specs += [
        pl.BlockSpec((t_ctx, LANES), lambda b, hh, i: (b, h + hh)),
        pl.BlockSpec((t_ctx, LANES), lambda b, hh, i: (b, hh)),
        pl.BlockSpec((1, LANES), lambda b, hh, i: (0, 0)),
    ]
    args += [qk_ctx, vg_ctx, nw]
    return pl.pallas_call(
        functools.partial(_attn_kernel, post_scale=1.0 - lam_init, with_lat=with_lat),
        grid=(batch, h, nq_tiles),
        in_specs=in_specs,
        out_specs=pl.BlockSpec((tq, LANES), lambda b, hh, i: (b * nq_tiles + i, hh)),
        out_shape=jax.ShapeDtypeStruct((n_q, h * LANES), BF16),
        scratch_shapes=[pltpu.VMEM((t_ctx, 2 * LANES), BF16)]
        + ([pltpu.VMEM((t_lat, 2 * LANES), BF16)] if with_lat else []),
        compiler_params=_cparams("arbitrary", "arbitrary", "arbitrary"),
        name="attn_lat" if with_lat else "attn_ctx",
    )(*args)


def _gla_constants():
    n = GLA_BLOCK
    idx = np.arange(n)
    same_chunk = (idx[:, None] // GLA_CHUNK) == (idx[None, :] // GLA_CHUNK)
    stacks, masks = [], []
    for reverse in (False, True):
        order = (idx[None, :] >= idx[:, None]) if reverse else (idx[None, :] <= idx[:, None])
        cum = (same_chunk & order).astype(np.int32)
        groups, lvl_masks = [cum], []
        for m in GLA_LEVELS:
            ref = (idx // (2 * m)) * (2 * m) + (m if reverse else m - 1)
            groups.append(np.abs(cum - cum[ref]))
            upper = (idx % (2 * m)) >= m
            same = (idx[:, None] // (2 * m)) == (idx[None, :] // (2 * m))
            q_side, k_side = (~upper, upper) if reverse else (upper, ~upper)
            lvl_masks.append(same & q_side[:, None] & k_side[None, :])
        end = (idx // GLA_CHUNK) * GLA_CHUNK + (0 if reverse else GLA_CHUNK - 1)
        groups.append(np.abs(cum[end] - cum))
        stacks.append(np.concatenate(groups, axis=0))
        masks.append(np.stack(lvl_masks))
    return jnp.asarray(np.stack(stacks), BF16), jnp.asarray(np.stack(masks), F32)


def _gla_prepare(q, v, lf, stack, *, reverse):
    n = q.shape[0]
    qf = q.astype(F32)
    k = 1.0 - jnp.exp(lf)

    hi = lf.astype(BF16)
    mid = (lf - hi.astype(F32)).astype(BF16)
    expo = jnp.dot(stack, jnp.concatenate([hi, mid], axis=1), preferred_element_type=F32)

    def exponent(group):
        blk = expo[group * n:(group + 1) * n]
        return blk[:, :LANES] + blk[:, LANES:]

    scale = [jnp.exp(exponent(1 + li)) for li in range(len(GLA_LEVELS))]
    qs = jnp.concatenate([(qf * e).astype(BF16) for e in scale], axis=1)
    ks = jnp.concatenate([(k * e).astype(BF16) for e in scale], axis=1)
    c = exponent(0)
    io = jnp.concatenate([(qf * jnp.exp(c)).astype(BF16),
                          (k * jnp.exp(exponent(1 + len(GLA_LEVELS)))).astype(BF16)], axis=1)
    diag = jnp.sum(qf * k, axis=-1, keepdims=True) * v.astype(F32)
    last = 0 if reverse else GLA_CHUNK - 1
    n_chunks = n // GLA_CHUNK
    decay = jnp.concatenate([jnp.exp(c[ci * GLA_CHUNK + last:ci * GLA_CHUNK + last + 1, :])
                             for ci in range(n_chunks)] + [jnp.ones((8 - n_chunks, LANES), F32)], axis=0)
    return qs, ks, io, diag, decay


def _gla_scan(qs, ks, io, diag, decay, v, st, masks, *, reverse):
    n = v.shape[0]
    scores = jnp.zeros((n, n), F32)
    for li in range(len(GLA_LEVELS)):
        cols = slice(li * LANES, (li + 1) * LANES)
        s_l = lax.dot_general(qs[:, cols], ks[:, cols], NT_DIMS, preferred_element_type=F32)
        scores = scores + s_l * masks[li]
    o = jnp.dot(scores.astype(BF16), v, preferred_element_type=F32) + diag

    n_chunks = n // GLA_CHUNK
    outs = [None] * n_chunks
    order = range(n_chunks - 1, -1, -1) if reverse else range(n_chunks)
    for ci in order:
        r = slice(ci * GLA_CHUNK, (ci + 1) * GLA_CHUNK)
        outs[ci] = lax.dot_general(io[r, :LANES], st.astype(BF16), NT_DIMS, preferred_element_type=F32)
        upd = lax.dot_general(v[r], io[r, LANES:], TN_DIMS, preferred_element_type=F32)
        st = st * decay[ci:ci + 1, :] + upd
    return o + jnp.concatenate(outs, axis=0), st


def _gla_kernel(ql_ref, il_ref, ffl_ref, fbl_ref, gl_ref, qc_ref, ic_ref, ffc_ref, fbc_ref, gc_ref,
                nw_ref, stack_ref, mask_ref, ol_ref, oc_ref,
                qs_scr, ks_scr, io_scr, dg_scr, dc_scr, of_scr, ob_scr):
    n_blocks = ql_ref.shape[0] // GLA_BLOCK
    nw = nw_ref[...]

    def finish(o, g):
        return (_rms(o) * nw * g.astype(F32)).astype(BF16)

    def rows(i):
        return pl.ds(pl.multiple_of(i * GLA_BLOCK, GLA_BLOCK), GLA_BLOCK)

    def prepare(slot, q, v, lf_f, lf_b):
        for d, lf in enumerate((lf_f, lf_b)):
            qs, ks, io, diag, decay = _gla_prepare(q, v, lf, stack_ref[d], reverse=bool(d))
            qs_scr[d, slot] = qs
            ks_scr[d, slot] = ks
            io_scr[d, slot] = io
            dg_scr[d, slot] = diag
            dc_scr[d, slot] = decay

    def scan(d, slot, v, st):
        return _gla_scan(qs_scr[d, slot], ks_scr[d, slot], io_scr[d, slot], dg_scr[d, slot], dc_scr[d, slot],
                         v, st, mask_ref[d], reverse=bool(d))

    prepare(0, qc_ref[...], ic_ref[...], ffc_ref[...], fbc_ref[...])

    def prep(i, carry):
        r = rows(i)
        prepare(i + 1, ql_ref[r, :], il_ref[r, :], ffl_ref[r, :], fbl_ref[r, :])
        return carry

    unroll = min(GLA_UNROLL, n_blocks)
    lax.fori_loop(0, n_blocks, prep, 0, unroll=unroll)

    zero_state = jnp.zeros((LANES, LANES), F32)
    o_cf, st_f = scan(0, 0, ic_ref[...], zero_state)
    o_cb, st_b = scan(1, 0, ic_ref[...], zero_state)
    oc_ref[...] = finish(o_cf + o_cb, gc_ref[...])

    def step(i, carry):
        st_f, st_b = carry
        j = n_blocks - 1 - i
        o_f, st_f = scan(0, i + 1, il_ref[rows(i), :], st_f)
        o_b, st_b = scan(1, j + 1, il_ref[rows(j), :], st_b)
        of_scr[rows(i), :] = o_f
        ob_scr[rows(j), :] = o_b
        return st_f, st_b

    lax.fori_loop(0, n_blocks, step, (st_f, st_b), unroll=unroll)

    def emit(i, carry):
        r = rows(i)
        ol_ref[r, :] = finish(of_scr[r, :] + ob_scr[r, :], gl_ref[r, :])
        return carry

    lax.fori_loop(0, n_blocks, emit, 0)


def _gla_call(qi_lat, lf_lat, vg_lat, qi_ctx, lf_ctx, vg_ctx, nw, *, batch):
    t_lat = qi_lat.shape[0] // batch
    t_ctx = qi_ctx.shape[0] // batch
    assert t_ctx == GLA_BLOCK and t_lat % GLA_BLOCK == 0
    h = REC_HEADS
    stack, masks = _gla_constants()
    n_slots = 1 + t_lat // GLA_BLOCK
    n_lvl = len(GLA_LEVELS)

    def col(off):
        return lambda b, hh: (b, off + hh)

    def seq(t):
        return [pl.BlockSpec((t, LANES), col(0)), pl.BlockSpec((t, LANES), col(h)),
                pl.BlockSpec((t, LANES), col(0)), pl.BlockSpec((t, LANES), col(h)),
                pl.BlockSpec((t, LANES), col(h))]

    return pl.pallas_call(
        _gla_kernel,
        grid=(batch, h),
        in_specs=seq(t_lat) + seq(t_ctx) + [
            pl.BlockSpec((1, LANES), lambda b, hh: (0, 0)),
            pl.BlockSpec(stack.shape, lambda b, hh: (0, 0, 0)),
            pl.BlockSpec(masks.shape, lambda b, hh: (0, 0, 0, 0)),
        ],
        out_specs=[pl.BlockSpec((t_lat, LANES), col(0)), pl.BlockSpec((t_ctx, LANES), col(0))],
        out_shape=[jax.ShapeDtypeStruct((batch * t_lat, h * LANES), BF16),
                   jax.ShapeDtypeStruct((batch * t_ctx, h * LANES), BF16)],
        scratch_shapes=[
            pltpu.VMEM((2, n_slots, GLA_BLOCK, n_lvl * LANES), BF16),
            pltpu.VMEM((2, n_slots, GLA_BLOCK, n_lvl * LANES), BF16),
            pltpu.VMEM((2, n_slots, GLA_BLOCK, 2 * LANES), BF16),
            pltpu.VMEM((2, n_slots, GLA_BLOCK, LANES), F32),
            pltpu.VMEM((2, n_slots, 8, LANES), F32),
            pltpu.VMEM((t_lat, LANES), F32), pltpu.VMEM((t_lat, LANES), F32),
        ],
        compiler_params=_cparams("arbitrary", "arbitrary"),
        name="gla",
    )(qi_lat, qi_lat, lf_lat, lf_lat, vg_lat, qi_ctx, qi_ctx, lf_ctx, lf_ctx, vg_ctx, nw, stack, masks)


def _top2_gates(logits):
    lane = lax.broadcasted_iota(jnp.int32, logits.shape, 1).astype(F32)
    big = float(LANES)
    m1 = jnp.max(logits, axis=-1, keepdims=True)
    i1 = jnp.min(jnp.where(logits == m1, lane, big), axis=-1, keepdims=True)
    rest = jnp.where(lane == i1, -jnp.inf, logits)
    m2 = jnp.max(rest, axis=-1, keepdims=True)
    i2 = jnp.min(jnp.where(rest == m2, lane, big), axis=-1, keepdims=True)
    e = jnp.exp(m2 - m1)
    w1 = 1.0 / (1.0 + e)
    return jnp.where(lane == i1, w1, 0.0) + jnp.where(lane == i2, e * w1, 0.0)


def _outproj_kernel(ya_ref, yr_ref, g_ref, x_ref, mod_ref, wua_ref, wur_ref, wo_ref, nw2_ref, *refs,
                    router, tiles_per_block):
    if router:
        rw_ref, tri_ref, xo_ref, h2a_ref, h2b_ref, route_ref, cnt_ref, carry_scr = refs
    else:
        xo_ref, h2_ref = refs
    d = x_ref.shape[1]
    ua = jnp.dot(ya_ref[...], wua_ref[...], preferred_element_type=F32)
    ur = jnp.dot(yr_ref[...], wur_ref[...], preferred_element_type=F32)
    u = g_ref[:, :d].astype(F32) * ua + g_ref[:, d:].astype(F32) * ur
    y = jnp.dot(u.astype(BF16), wo_ref[...], preferred_element_type=F32)
    xn = x_ref[...] + mod_ref[2:3, :] * y
    xo_ref[...] = xn
    h2 = (_rms(xn) * nw2_ref[...]) * (1.0 + mod_ref[4:5, :]) + mod_ref[3:4, :]
    h2_hi = h2.astype(BF16)
    if not router:
        h2_ref[...] = h2_hi
    if router:
        h2a_ref[...], h2b_ref[...] = _pack_row(h2)
        h2_lo = (h2 - h2_hi.astype(F32)).astype(BF16)
        logits = (jnp.dot(h2_hi, rw_ref[0], preferred_element_type=F32)
                  + jnp.dot(h2_lo, rw_ref[0], preferred_element_type=F32)
                  + jnp.dot(h2_hi, rw_ref[1], preferred_element_type=F32))
        lane = lax.broadcasted_iota(jnp.int32, logits.shape, 1)
        gates = _top2_gates(jnp.where(lane < N_EXPERTS, logits, -jnp.inf))

        @pl.when(pl.program_id(0) % tiles_per_block == 0)
        def _():
            carry_scr[...] = jnp.zeros_like(carry_scr)

        sel = gates > 0.0
        sel_f = jnp.where(sel, 1.0, 0.0)
        rank = jnp.dot(tri_ref[...], sel_f.astype(BF16), preferred_element_type=F32) + carry_scr[...]
        total = carry_scr[...] + jnp.sum(sel_f, axis=0, keepdims=True)
        carry_scr[...] = total
        cnt_ref[...] = total

        lanef = lane.astype(F32)
        e_a = jnp.min(jnp.where(sel, lanef, float(LANES)), axis=-1, keepdims=True)
        e_b = jnp.max(jnp.where(sel, lanef, -1.0), axis=-1, keepdims=True)
        is_a = lanef == e_a
        is_b = lanef == e_b

        def pick(mask, v):
            return jnp.sum(jnp.where(mask, v, 0.0), axis=-1, keepdims=True)

        fields = (pick(is_a, rank), pick(is_b, rank), e_a, e_b, pick(is_a, gates),
                  jnp.where(e_b == e_a, 0.0, pick(is_b, gates)))
        route = jnp.zeros_like(gates)
        for k, v in enumerate(fields):
            route = jnp.where(lane == k, v, route)
        route_ref[...] = route


def _outproj_call(ya, yr, g, x, mod, wua, wur, wo, nw2, rw, *, tm, rows_per_mod, route_block=None):
    n, d = x.shape
    assert n % tm == 0, (n, tm)
    router = rw is not None
    row = lambda i: (i, 0)
    const = lambda i: (0, 0)
    tiles_per_block = route_block // tm if router else 1
    if rows_per_mod is None:
        mod_map = lambda i: (mod.shape[0] - 1, 0, 0)
    else:
        mod_map = lambda i: (i // (rows_per_mod // tm), 0, 0)
    in_specs = [
        pl.BlockSpec((tm, ya.shape[1]), row),
        pl.BlockSpec((tm, yr.shape[1]), row),
        pl.BlockSpec((tm, 2 * d), row),
        pl.BlockSpec((tm, d), row),
        pl.BlockSpec((None, 6, d), mod_map),
        pl.BlockSpec(wua.shape, const),
        pl.BlockSpec(wur.shape, const),
        pl.BlockSpec(wo.shape, const),
        pl.BlockSpec((1, d), const),
    ]
    args = [ya, yr, g, x, mod, wua, wur, wo, nw2]
    out_specs = [pl.BlockSpec((tm, d), row)]
    out_shape = [jax.ShapeDtypeStruct((n, d), F32)]
    scratch = []
    if router:
        tri = jnp.asarray(np.tril(np.ones((tm, tm), np.float32), -1), BF16)
        in_specs += [pl.BlockSpec(rw.shape, lambda i: (0, 0, 0)), pl.BlockSpec(tri.shape, const)]
        args += [rw, tri]
        out_specs += [pl.BlockSpec((tm, d // 4), row), pl.BlockSpec((tm, d // 4), row),
                      pl.BlockSpec((tm, LANES), row),
                      pl.BlockSpec((None, 1, LANES), lambda i: (i // tiles_per_block, 0, 0))]
        out_shape += [jax.ShapeDtypeStruct((n, d // 4), jnp.uint32), jax.ShapeDtypeStruct((n, d // 4), jnp.uint32),
                      jax.ShapeDtypeStruct((n, LANES), F32),
                      jax.ShapeDtypeStruct((n // route_block, 1, LANES), F32)]
        scratch = [pltpu.VMEM((1, LANES), F32)]
    else:
        out_specs.append(pl.BlockSpec((tm, d), row))
        out_shape.append(jax.ShapeDtypeStruct((n, d), BF16))
    return pl.pallas_call(
        functools.partial(_outproj_kernel, router=router, tiles_per_block=tiles_per_block),
        grid=(n // tm,),
        in_specs=in_specs,
        out_specs=out_specs,
        out_shape=out_shape,
        scratch_shapes=scratch,
        compiler_params=_cparams("arbitrary"),
        name="outproj_router" if router else "outproj",
    )(*args)


def _swiglu_chunk(xs, w1_ref, w3_ref, w2_ref):
    a = jnp.dot(xs, w1_ref[...], preferred_element_type=F32)
    b = jnp.dot(xs, w3_ref[...], preferred_element_type=F32)
    return jnp.dot((_silu(a) * b).astype(BF16), w2_ref[...], preferred_element_type=F32)


def _ffn_kernel(h_ref, x_ref, mod_ref, w1_ref, w3_ref, w2_ref, o_ref, *, tf):
    h = h_ref[...]
    ff = w1_ref.shape[1]
    acc = jnp.zeros(o_ref.shape, F32)
    for lo in range(0, ff, tf):
        hi = min(lo + tf, ff)
        a = jnp.dot(h, w1_ref[:, lo:hi], preferred_element_type=F32)
        b = jnp.dot(h, w3_ref[:, lo:hi], preferred_element_type=F32)
        acc = acc + jnp.dot((_silu(a) * b).astype(BF16), w2_ref[lo:hi, :], preferred_element_type=F32)
    o_ref[...] = x_ref[...] + mod_ref[5:6, :] * acc


def _ffn_call(h, x, mod, w1, w3, w2, *, tm, tf, rows_per_mod):
    n, d = x.shape
    assert n % tm == 0, (n, tm)
    row = lambda i: (i, 0)
    const = lambda i: (0, 0)
    if rows_per_mod is None:
        mod_map = lambda i: (mod.shape[0] - 1, 0, 0)
    else:
        mod_map = lambda i: (i // (rows_per_mod // tm), 0, 0)
    return pl.pallas_call(
        functools.partial(_ffn_kernel, tf=tf),
        grid=(n // tm,),
        in_specs=[
            pl.BlockSpec((tm, d), row), pl.BlockSpec((tm, d), row), pl.BlockSpec((None, 6, d), mod_map),
            pl.BlockSpec(w1.shape, const), pl.BlockSpec(w3.shape, const), pl.BlockSpec(w2.shape, const),
        ],
        out_specs=pl.BlockSpec((tm, d), row),
        out_shape=jax.ShapeDtypeStruct((n, d), F32),
        compiler_params=_cparams("arbitrary"),
        name="ffn",
    )(h, x, mod, w1, w3, w2)


def _pack_halves(x):
    n = x.shape[1] // 2
    lo = lax.bitcast_convert_type(x[:, :n].astype(BF16).astype(F32), jnp.uint32)
    hi = lax.bitcast_convert_type(x[:, n:].astype(BF16).astype(F32), jnp.uint32)
    return (hi & jnp.uint32(0xFFFF0000)) | (lo >> 16)


def _unpack_halves(w):
    lo = lax.bitcast_convert_type(w << 16, F32)
    hi = lax.bitcast_convert_type(w & jnp.uint32(0xFFFF0000), F32)
    return jnp.concatenate([lo, hi], axis=1)


def _sc_gather_rows(tables, idxs, *, window):
    n_jobs = len(tables)
    mesh = plsc.VectorSubcoreMesh(core_axis_name="c", subcore_axis_name="s")
    out_type = [jax.ShapeDtypeStruct((idx.shape[0], t.shape[1]), t.dtype) for t, idx in zip(tables, idxs)]
    for idx in idxs:
        assert idx.shape[0] % window == 0, (idx.shape, window)

    @pl.kernel(out_type=out_type, mesh=mesh)
    def gather(*refs):
        for j in range(n_jobs):
            table_hbm, idx_hbm, out_hbm = refs[j], refs[n_jobs + j], refs[2 * n_jobs + j]

            def body(idx_vmem, out_vmem, table_hbm=table_hbm):
                pltpu.sync_copy(table_hbm.at[idx_vmem.at[0]], out_vmem)

            pltpu.emit_pipeline(
                body,
                grid=(idx_hbm.shape[1] // window,),
                in_specs=[pl.BlockSpec((1, window), lambda i: (0, i))],
                out_specs=[pl.BlockSpec((window, table_hbm.shape[1]), lambda i: (i, 0))],
                core_axis_name=("c", "s"),
                dimension_semantics=(pltpu.PARALLEL,),
            )(idx_hbm, out_hbm)

    return gather(*tables, *[idx.reshape(1, -1) for idx in idxs])


def _packed_row(pair):
    return jnp.concatenate([_unpack_halves(pair[0][...]), _unpack_halves(pair[1][...])], axis=1)


def _pack_row(x):
    half = x.shape[1] // 2
    return _pack_halves(x[:, :half]), _pack_halves(x[:, half:])


def _experts_kernel(tile_expert_ref, n_active_ref, xa_ref, xb_ref, w1_ref, w3_ref, w2_ref, oa_ref, ob_ref,
                    x_scr, acc_ref):
    i, f = pl.program_id(0), pl.program_id(1)

    @pl.when(i < n_active_ref[0])
    def _():
        @pl.when(f == 0)
        def _():
            x_scr[...] = _packed_row((xa_ref, xb_ref)).astype(BF16)
            acc_ref[...] = jnp.zeros_like(acc_ref)

        acc_ref[...] += _swiglu_chunk(x_scr[...], w1_ref, w3_ref, w2_ref)

        @pl.when(f == pl.num_programs(1) - 1)
        def _():
            oa_ref[...], ob_ref[...] = _pack_row(acc_ref[...])


def _experts_call(tile_expert, n_active, xa, xb, w1, w3, w2, *, tm, tf):
    n, dp = xa.shape
    assert n % tm == 0, (n, tm)
    _, d, ff = w1.shape
    rows = pl.BlockSpec((tm, dp), lambda i, f, te, na: (i, 0))
    grid_spec = pltpu.PrefetchScalarGridSpec(
        num_scalar_prefetch=2,
        grid=(n // tm, ff // tf),
        in_specs=[
            rows, rows,
            pl.BlockSpec((None, d, tf), lambda i, f, te, na: (te[i], 0, f)),
            pl.BlockSpec((None, d, tf), lambda i, f, te, na: (te[i], 0, f)),
            pl.BlockSpec((None, tf, d), lambda i, f, te, na: (te[i], f, 0)),
        ],
        out_specs=[rows, rows],
        scratch_shapes=[pltpu.VMEM((tm, d), BF16), pltpu.VMEM((tm, d), F32)],
    )
    return pl.pallas_call(
        _experts_kernel,
        grid_spec=grid_spec,
        out_shape=[jax.ShapeDtypeStruct((n, dp), jnp.uint32)] * 2,
        compiler_params=_cparams("arbitrary", "arbitrary"),
        name="experts",
    )(tile_expert, n_active, xa, xb, w1, w3, w2)


def _combine_kernel(x_ref, yaa_ref, yab_ref, yba_ref, ybb_ref, route_ref, mod_ref, fnw_ref, o_ref):
    y = (route_ref[:, 4:5] * _packed_row((yaa_ref, yab_ref))
         + route_ref[:, 5:6] * _packed_row((yba_ref, ybb_ref)))
    o_ref[...] = _rms(x_ref[...] + mod_ref[5:6, :] * y) * fnw_ref[...]


def _combine_call(x, ys, route, mod, fnw, *, tm, rows_per_mod):
    n, d = x.shape
    assert n % tm == 0, (n, tm)
    row = lambda i: (i, 0)
    return pl.pallas_call(
        _combine_kernel,
        grid=(n // tm,),
        in_specs=[pl.BlockSpec((tm, d), row)] + [pl.BlockSpec((tm, d // 4), row)] * 4 + [
            pl.BlockSpec((tm, LANES), row),
            pl.BlockSpec((None, 6, d), lambda i: (i // (rows_per_mod // tm), 0, 0)),
            pl.BlockSpec((1, d), lambda i: (0, 0))],
        out_specs=pl.BlockSpec((tm, d), row),
        out_shape=jax.ShapeDtypeStruct((n, d), F32),
        compiler_params=_cparams("arbitrary"),
        name="combine",
    )(x, *ys, route, mod, fnw)


def _dispatch_plan(route, counts, *, tm):
    n = route.shape[0]
    n_e = counts.shape[0]
    n_slots = TOP_K * n + n_e * tm
    seg = (counts + tm - 1) // tm * tm
    ends = jnp.cumsum(seg)
    offs = ends - seg
    rec = route[:, :4].astype(jnp.int32)
    slot_a = jnp.take(offs, rec[:, 2]) + rec[:, 0]
    slot_b = jnp.take(offs, rec[:, 3]) + rec[:, 1]
    tok = jnp.arange(n, dtype=jnp.int32)
    token_of_slot = jnp.zeros((n_slots,), jnp.int32).at[jnp.concatenate([slot_a, slot_b])].set(
        jnp.concatenate([tok, tok]))
    n_tiles = n_slots // tm
    n_active = (ends[-1] // tm).astype(jnp.int32)
    tile_start = jnp.arange(n_tiles, dtype=jnp.int32) * tm
    tile_start = jnp.minimum(tile_start, jnp.maximum(ends[-1] - tm, 0))
    tile_expert = jnp.sum(tile_start[:, None] >= ends[None, :], axis=1).astype(jnp.int32)
    return token_of_slot, slot_a, slot_b, jnp.minimum(tile_expert, n_e - 1), n_active.reshape(1)


def _rope_tables(t_lat):
    rows = t_lat // GRID_W
    row = jnp.repeat(jnp.arange(rows, dtype=F32), GRID_W)
    col = jnp.tile(jnp.arange(GRID_W, dtype=F32), rows)
    n_freq = ATT_QK_DIM // 4
    inv_freq = ROPE_THETA ** (-jnp.arange(n_freq, dtype=F32) / n_freq)
    ang = jnp.concatenate([row[:, None] * inv_freq, col[:, None] * inv_freq], axis=-1)
    cos, sin = jnp.cos(ang), jnp.sin(ang)
    return (jnp.concatenate([cos, cos, cos, cos], axis=-1), jnp.concatenate([-sin, -sin, sin, sin], axis=-1))


def _layer_lower_bounds(lb_param):
    cs = jnp.cumsum(jax.nn.softmax(lb_param.astype(F32), axis=0), axis=0)
    return cs - cs[0:1]


def _win_columns(w_in_l):
    c = [w_in_l[:, i * 512:(i + 1) * 512] for i in range(8)]
    half = ATT_QK_DIM // 2

    def pair_halves(w):
        w = w.reshape(w.shape[0], ATT_HEADS, 2, 2, half)
        return w.transpose(0, 1, 3, 2, 4).reshape(w.shape[0], -1)

    return jnp.concatenate([pair_halves(c[0]), pair_halves(c[1]), c[2], c[7], c[3], c[4], c[5], c[6],
                            w_in_l[:, 4096:]], axis=1).astype(BF16)


def _pad_ff(w, axis, mult):
    ff = w.shape[axis]
    pad = (-ff) % mult
    if pad == 0:
        return w
    widths = [(0, 0)] * w.ndim
    widths[axis] = (0, pad)
    return jnp.pad(w, widths)


def kernel(x, c, ctx, c_ctx, w_ada, b_ada, norm_mix_w, norm_ffn_w, w_in, lambda_q1, lambda_k1, lambda_q2,
           lambda_k2, att_norm_w, rec_norm_w, lb_fwd, lb_bwd, w_up_att, w_up_rec, w_out, ffn_w1, ffn_w3,
           ffn_w2, router_w, moe_w1, moe_w3, moe_w2, final_norm_w):
    batch, t_lat, d = x.shape
    t_ctx = ctx.shape[1]
    depth = w_ada.shape[0]
    n_lat, n_ctx = batch * t_lat, batch * t_ctx
    tm = 512
    tm_in = 512
    tq = min(1024, t_lat)

    xl = x.reshape(n_lat, d)
    xc = ctx.reshape(n_ctx, d)

    pad_rows = (-(batch + 1)) % 8
    cc = jnp.concatenate([c, jnp.zeros((pad_rows, d), F32), c_ctx[None, :]], axis=0)
    mod_all = _mod_call(cc, w_ada, b_ada).reshape(depth, cc.shape[0], 6, d)

    cos, sin = _rope_tables(t_lat)
    lbs_f = _layer_lower_bounds(lb_fwd)
    lbs_b = _layer_lower_bounds(lb_bwd)

    for l in range(depth):
        last = l == depth - 1
        mod = mod_all[l]
        lam_init = 0.8 - 0.6 * math.exp(-0.3 * l)
        lam = (jnp.exp(jnp.sum(lambda_q1[l] * lambda_k1[l])) - jnp.exp(jnp.sum(lambda_q2[l] * lambda_k2[l]))
               + lam_init).reshape(1).astype(F32)
        lb = jnp.concatenate([lbs_f[l], lbs_b[l]])[None, :]
        w_l = _win_columns(w_in[l])
        nw = norm_mix_w[l][None, :]

        pl_lat = _inproj_call(xl, mod, nw, w_l, cos, sin, lb, tm=tm_in, rows_per_mod=t_lat, rope=True)
        pl_ctx = _inproj_call(xc, mod, nw, w_l, cos, sin, lb, tm=min(tm_in, n_ctx), rows_per_mod=None,
                              rope=False)
        qk_l, vg_l, qi_l, lf_l, g_l = pl_lat
        qk_c, vg_c, qi_c, lf_c, g_c = pl_ctx

        anw = att_norm_w[l][None, :]
        ya_l = _attn_call(lam, qk_l, qk_l, vg_l, qk_c, vg_c, anw, batch=batch, tq=tq,
                          lam_init=lam_init, with_lat=True)
        yr_l, yr_c = _gla_call(qi_l, lf_l, vg_l, qi_c, lf_c, vg_c, rec_norm_w[l][None, :], batch=batch)

        wua = w_up_att[l].astype(BF16)
        wur = w_up_rec[l].astype(BF16)
        wo = w_out[l].astype(BF16)
        nw2 = norm_ffn_w[l][None, :]
        moe_layer = l % 2 == 1
        j = l // 2
        assert moe_layer == last
        rw = None
        if moe_layer:
            rw32 = jnp.pad(router_w[j], ((0, 0), (0, LANES - N_EXPERTS)))
            rw_hi = rw32.astype(BF16)
            rw = jnp.stack([rw_hi, (rw32 - rw_hi.astype(F32)).astype(BF16)])
        res = _outproj_call(ya_l, yr_l, g_l, xl, mod, wua, wur, wo, nw2, rw, tm=tm, rows_per_mod=t_lat,
                            route_block=n_lat)
        xl = res[0]
        if not last:
            ya_c = _attn_call(lam, qk_c, None, None, qk_c, vg_c, anw, batch=batch, tq=t_ctx,
                              lam_init=lam_init, with_lat=False)
            xc, h2_c = _outproj_call(ya_c, yr_c, g_c, xc, mod, wua, wur, wo, nw2, None, tm=tm, rows_per_mod=None)

        if moe_layer:
            h2a, h2b, route, cnt = res[1:]
            counts = cnt[0, 0, :N_EXPERTS].astype(jnp.int32)
            mt = min(MOE_TILE, n_lat)
            token_of_slot, slot_a, slot_b, tile_expert, n_active = _dispatch_plan(route, counts, tm=mt)
            xa, xb = _sc_gather_rows([h2a, h2b], [token_of_slot] * 2, window=SC_WINDOW)
            oa, ob = _experts_call(tile_expert, n_active, xa, xb, moe_w1[j].astype(BF16),
                                   moe_w3[j].astype(BF16), moe_w2[j].astype(BF16), tm=mt, tf=512)
            ys = _sc_gather_rows([oa, ob, oa, ob], [slot_a, slot_a, slot_b, slot_b], window=SC_WINDOW)
            xl = _combine_call(xl, ys, route, mod, final_norm_w[None, :], tm=tm, rows_per_mod=t_lat)
        else:
            h2_l = res[1]
            w1 = _pad_ff(ffn_w1[j], 1, 2 * LANES).astype(BF16)
            w3 = _pad_ff(ffn_w3[j], 1, 2 * LANES).astype(BF16)
            w2 = _pad_ff(ffn_w2[j], 0, 2 * LANES).astype(BF16)
            tf = 512
            xl = _ffn_call(h2_l, xl, mod, w1, w3, w2, tm=tm, tf=tf, rows_per_mod=t_lat)
            xc = _ffn_call(h2_c, xc, mod, w1, w3, w2, tm=tm, tf=tf, rows_per_mod=None)

    return xl.reshape(batch, t_lat, d)
```

```python
import functools
import math

import numpy as np
import jax
import jax.numpy as jnp
from jax import lax
from jax.experimental import pallas as pl
from jax.experimental.pallas import tpu as pltpu
from jax.experimental.pallas import tpu_sc as plsc

F32 = jnp.float32
BF16 = jnp.bfloat16
HIGHEST = lax.Precision.HIGHEST

EPS = 1e-6
GRID_W = 64
ROPE_THETA = 10000.0
ATT_HEADS = 4
ATT_QK_DIM = 64
REC_HEADS = 4
N_EXPERTS = 8
TOP_K = 2
Q_SCALE = ATT_QK_DIM ** -0.5 * math.log2(math.e)

LANES = 128
GLA_BLOCK = 256
GLA_CHUNK = 256
GLA_UNROLL = 4
GLA_LEVELS = tuple(2 ** i for i in range(GLA_CHUNK.bit_length() - 1))
VMEM_LIMIT = 56 * 1024 * 1024
MOE_TILE = 1024
SC_WINDOW = 128
ATT_KEYS = 256
ATT_ROWS = 512

NT_DIMS = (((1,), (1,)), ((), ()))
TN_DIMS = (((0,), (0,)), ((), ()))


def _cparams(*sem):
    return pltpu.CompilerParams(dimension_semantics=sem, vmem_limit_bytes=VMEM_LIMIT)


def _silu(a):
    return a * jax.nn.sigmoid(a)


def _rms(x):
    return x * lax.rsqrt(jnp.mean(x * x, axis=-1, keepdims=True) + EPS)


def _mod_kernel(c_ref, w_ref, b_ref, o_ref):
    s = _silu(c_ref[...])
    o_ref[...] = jnp.dot(s, w_ref[...], precision=HIGHEST, preferred_element_type=F32) + b_ref[...]


def _mod_call(cc, w_ada, b_ada):
    depth, d, n = w_ada.shape
    rows = cc.shape[0]
    tn = 1536
    return pl.pallas_call(
        _mod_kernel,
        grid=(depth, n // tn),
        in_specs=[
            pl.BlockSpec((rows, d), lambda l, j: (0, 0)),
            pl.BlockSpec((None, d, tn), lambda l, j: (l, 0, j)),
            pl.BlockSpec((None, 1, tn), lambda l, j: (l, 0, j)),
        ],
        out_specs=pl.BlockSpec((None, rows, tn), lambda l, j: (l, 0, j)),
        out_shape=jax.ShapeDtypeStruct((depth, rows, n), F32),
        compiler_params=_cparams("arbitrary", "arbitrary"),
        name="mod",
    )(cc, w_ada, b_ada.reshape(depth, 1, n))


def _log_forget(z, lb):
    t = jnp.exp(-jnp.abs(z))
    num = jnp.where(z >= 0.0, 1.0 + lb * t, lb + t)
    return jnp.where(num > 0.0, jnp.log(num / (1.0 + t)), z)


def _inproj_kernel(x_ref, mod_ref, nw_ref, w_ref, cos_ref, sin_ref, lb_ref,
                   qk_ref, vg_ref, qi_ref, lf_ref, g_ref, *, rope):
    h = _rms(x_ref[...]) * nw_ref[...]
    h = (h * (1.0 + mod_ref[1:2, :]) + mod_ref[0:1, :]).astype(BF16)
    ts = qk_ref.shape[1]
    half = ts // 2

    def proj(s):
        return jnp.dot(h, w_ref[:, s * ts:(s + 1) * ts], preferred_element_type=F32)

    p = proj(0)
    if rope:
        cos = cos_ref[...]
        sin = sin_ref[...]
    for j in range(ts // LANES):
        blk = p[:, j * LANES:(j + 1) * LANES]
        if rope:
            blk = blk * cos + pltpu.roll(blk, LANES // 2, 1) * sin
        if j * LANES < half:
            blk = blk * Q_SCALE
        qk_ref[:, j * LANES:(j + 1) * LANES] = blk.astype(BF16)

    p = proj(1)
    vg_ref[:, :half] = p[:, :half].astype(BF16)
    vg_ref[:, half:] = _silu(p[:, half:]).astype(BF16)

    p = proj(2)
    qi_ref[:, :half] = _silu(p[:, :half]).astype(BF16)
    qi_ref[:, half:] = p[:, half:].astype(BF16)

    lf_ref[...] = _log_forget(proj(3), lb_ref[...])

    g_ref[:, :ts] = jax.nn.sigmoid(proj(4)).astype(BF16)
    g_ref[:, ts:] = jax.nn.sigmoid(proj(5)).astype(BF16)


def _inproj_call(x, mod, nw, w, cos, sin, lb, *, tm, rows_per_mod, rope):
    n, d = x.shape
    assert n % tm == 0, (n, tm)
    ts = 1024
    assert w.shape[1] == 6 * ts
    pos_tiles = cos.shape[0] // tm
    row = lambda i: (i, 0)
    const = lambda i: (0, 0)
    if rows_per_mod is None:
        mod_map = lambda i: (mod.shape[0] - 1, 0, 0)
    else:
        mod_map = lambda i: (i // (rows_per_mod // tm), 0, 0)
    outs = pl.pallas_call(
        functools.partial(_inproj_kernel, rope=rope),
        grid=(n // tm,),
        in_specs=[
            pl.BlockSpec((tm, d), row),
            pl.BlockSpec((None, 6, d), mod_map),
            pl.BlockSpec((1, d), const),
            pl.BlockSpec(w.shape, const),
            pl.BlockSpec((tm, LANES), lambda i: (i % pos_tiles, 0)),
            pl.BlockSpec((tm, LANES), lambda i: (i % pos_tiles, 0)),
            pl.BlockSpec((1, ts), const),
        ],
        out_specs=[
            pl.BlockSpec((tm, ts), row),
            pl.BlockSpec((tm, ts), row),
            pl.BlockSpec((tm, ts), row),
            pl.BlockSpec((tm, ts), row),
            pl.BlockSpec((tm, 2 * ts), row),
        ],
        out_shape=[
            jax.ShapeDtypeStruct((n, ts), BF16),
            jax.ShapeDtypeStruct((n, ts), BF16),
            jax.ShapeDtypeStruct((n, ts), BF16),
            jax.ShapeDtypeStruct((n, ts), F32),
            jax.ShapeDtypeStruct((n, 2 * ts), BF16),
        ],
        compiler_params=_cparams("arbitrary"),
        name="inproj_rope" if rope else "inproj",
    )(x, mod, nw, w, cos, sin, lb)
    return outs


def _attn_kernel(lam_ref, q_ref, *refs, post_scale, with_lat):
    if with_lat:
        kl_ref, vl_ref, kc_ref, vc_ref, nw_ref, o_ref, vce_scr, vle_scr = refs
    else:
        kc_ref, vc_ref, nw_ref, o_ref, vce_scr = refs

    @pl.when(pl.program_id(2) == 0)
    def _():
        vce_scr[:, :LANES] = vc_ref[...]
        vce_scr[:, LANES:] = jnp.ones(vc_ref.shape, BF16)
        if with_lat:
            vle_scr[:, :LANES] = vl_ref[...]
            vle_scr[:, LANES:] = jnp.ones(vl_ref.shape, BF16)

    n_parts = max(1, q_ref.shape[0] // ATT_ROWS)
    rows = q_ref.shape[0] // n_parts
    qq = []
    for part in range(n_parts):
        q = q_ref[part * rows:(part + 1) * rows, :]
        lane = lax.broadcasted_iota(jnp.int32, q.shape, 1)
        zero = jnp.zeros_like(q)
        sub1 = (lane % ATT_QK_DIM) < (ATT_QK_DIM // 2)
        qq.append(jnp.concatenate([jnp.where(sub1, q, zero), jnp.where(sub1, zero, q)], axis=0))

    blocks = [(kc_ref, vce_scr, 0, kc_ref.shape[0])]
    if with_lat:
        blocks += [(kl_ref, vle_scr, j, ATT_KEYS) for j in range(0, kl_ref.shape[0], ATT_KEYS)]
    m = [jnp.full((2 * rows, 1), -jnp.inf, F32)] * n_parts
    acc = [jnp.zeros((2 * rows, 2 * LANES), F32)] * n_parts
    for k_ref, v_scr, start, size in blocks:
        for part in range(n_parts):
            s = lax.dot_general(qq[part], k_ref[start:start + size, :], NT_DIMS, preferred_element_type=F32)
            m_new = jnp.maximum(m[part], jnp.max(s, axis=-1, keepdims=True))
            p = jnp.exp2(s - m_new).astype(BF16)
            acc[part] = acc[part] * jnp.exp2(m[part] - m_new) + jnp.dot(
                p, v_scr[start:start + size, :], preferred_element_type=F32)
            m[part] = m_new
    for part in range(n_parts):
        on = acc[part][:, :LANES] * (1.0 / acc[part][:, LANES:])
        o = on[:rows] - lam_ref[0] * on[rows:]
        o_ref[part * rows:(part + 1) * rows, :] = (_rms(o) * nw_ref[...] * post_scale).astype(BF16)


def _attn_call(lam, q_src, qk_lat, vg_lat, qk_ctx, vg_ctx, nw, *, batch, tq, lam_init, with_lat):
    n_q = q_src.shape[0]
    t_q = n_q // batch
    t_lat = qk_lat.shape[0] // batch if with_lat else 0
    t_ctx = qk_ctx.shape[0] // batch
    nq_tiles = t_q // tq
    h = ATT_HEADS
    in_specs = [
        pl.BlockSpec(memory_space=pltpu.SMEM),
        pl.BlockSpec((tq, LANES), lambda b, hh, i: (b * nq_tiles + i, hh)),
    ]
    args = [lam, q_src]
    if with_lat:
        in_specs += [
            pl.BlockSpec((t_lat, LANES), lambda b, hh, i: (b, h + hh)),
            pl.BlockSpec((t_lat, LANES), lambda b, hh, i: (b, hh)),
        ]
        args += [qk_lat, vg_lat]
    in_specs += [
        pl.BlockSpec((t_ctx, LANES), lambda b, hh, i: (b, h + hh)),
        pl.BlockSpec((t_ctx, LANES), lambda b, hh, i: (b, hh)),
        pl.BlockSpec((1, LANES), lambda b, hh, i: (0, 0)),
    ]
    args += [qk_ctx, vg_ctx, nw]
    return pl.pallas_call(
        functools.partial(_attn_kernel, post_scale=1.0 - lam_init, with_lat=with_lat),
        grid=(batch, h, nq_tiles),
        in_specs=in_specs,
        out_specs=pl.BlockSpec((tq, LANES), lambda b, hh, i: (b * nq_tiles + i, hh)),
        out_shape=jax.ShapeDtypeStruct((n_q, h * LANES), BF16),
        scratch_shapes=[pltpu.VMEM((t_ctx, 2 * LANES), BF16)]
        + ([pltpu.VMEM((t_lat, 2 * LANES), BF16)] if with_lat else []),
        compiler_params=_cparams("arbitrary", "arbitrary", "arbitrary"),
        name="attn_lat" if with_lat else "attn_ctx",
    )(*args)


def _gla_constants():
    n = GLA_BLOCK
    idx = np.arange(n)
    same_chunk = (idx[:, None] // GLA_CHUNK) == (idx[None, :] // GLA_CHUNK)
    stacks, masks = [], []
    for reverse in (False, True):
        order = (idx[None, :] >= idx[:, None]) if reverse else (idx[None, :] <= idx[:, None])
        cum = (same_chunk & order).astype(np.int32)
        groups, lvl_masks = [cum], []
        for m in GLA_LEVELS:
            ref = (idx // (2 * m)) * (2 * m) + (m if reverse else m - 1)
            groups.append(np.abs(cum - cum[ref]))
            upper = (idx % (2 * m)) >= m
            same = (idx[:, None] // (2 * m)) == (idx[None, :] // (2 * m))
            q_side, k_side = (~upper, upper) if reverse else (upper, ~upper)
            lvl_masks.append(same & q_side[:, None] & k_side[None, :])
        end = (idx // GLA_CHUNK) * GLA_CHUNK + (0 if reverse else GLA_CHUNK - 1)
        groups.append(np.abs(cum[end] - cum))
        stacks.append(np.concatenate(groups, axis=0))
        masks.append(np.stack(lvl_masks))
    return jnp.asarray(np.stack(stacks), BF16), jnp.asarray(np.stack(masks), F32)


def _gla_prepare(q, v, lf, stack, *, reverse):
    n = q.shape[0]
    qf = q.astype(F32)
    k = 1.0 - jnp.exp(lf)

    hi = lf.astype(BF16)
    mid = (lf - hi.astype(F32)).astype(BF16)
    expo = jnp.dot(stack, jnp.concatenate([hi, mid], axis=1), preferred_element_type=F32)

    def exponent(group):
        blk = expo[group * n:(group + 1) * n]
        return blk[:, :LANES] + blk[:, LANES:]

    scale = [jnp.exp(exponent(1 + li)) for li in range(len(GLA_LEVELS))]
    qs = jnp.concatenate([(qf * e).astype(BF16) for e in scale], axis=1)
    ks = jnp.concatenate([(k * e).astype(BF16) for e in scale], axis=1)
    c = exponent(0)
    io = jnp.concatenate([(qf * jnp.exp(c)).astype(BF16),
                          (k * jnp.exp(exponent(1 + len(GLA_LEVELS)))).astype(BF16)], axis=1)
    diag = jnp.sum(qf * k, axis=-1, keepdims=True) * v.astype(F32)
    last = 0 if reverse else GLA_CHUNK - 1
    n_chunks = n // GLA_CHUNK
    decay = jnp.concatenate([jnp.exp(c[ci * GLA_CHUNK + last:ci * GLA_CHUNK + last + 1, :])
                             for ci in range(n_chunks)] + [jnp.ones((8 - n_chunks, LANES), F32)], axis=0)
    return qs, ks, io, diag, decay


def _gla_scan(qs, ks, io, diag, decay, v, st, masks, *, reverse):
    n = v.shape[0]
    scores = jnp.zeros((n, n), F32)
    for li in range(len(GLA_LEVELS)):
        cols = slice(li * LANES, (li + 1) * LANES)
        s_l = lax.dot_general(qs[:, cols], ks[:, cols], NT_DIMS, preferred_element_type=F32)
        scores = scores + s_l * masks[li]
    o = jnp.dot(scores.astype(BF16), v, preferred_element_type=F32) + diag

    n_chunks = n // GLA_CHUNK
    outs = [None] * n_chunks
    order = range(n_chunks - 1, -1, -1) if reverse else range(n_chunks)
    for ci in order:
        r = slice(ci * GLA_CHUNK, (ci + 1) * GLA_CHUNK)
        outs[ci] = lax.dot_general(io[r, :LANES], st.astype(BF16), NT_DIMS, preferred_element_type=F32)
        upd = lax.dot_general(v[r], io[r, LANES:], TN_DIMS, preferred_element_type=F32)
        st = st * decay[ci:ci + 1, :] + upd
    return o + jnp.concatenate(outs, axis=0), st


def _gla_kernel(ql_ref, il_ref, ffl_ref, fbl_ref, gl_ref, qc_ref, ic_ref, ffc_ref, fbc_ref, gc_ref,
                nw_ref, stack_ref, mask_ref, ol_ref, oc_ref,
                qs_scr, ks_scr, io_scr, dg_scr, dc_scr, of_scr, ob_scr):
    n_blocks = ql_ref.shape[0] // GLA_BLOCK
    nw = nw_ref[...]

    def finish(o, g):
        return (_rms(o) * nw * g.astype(F32)).astype(BF16)

    def rows(i):
        return pl.ds(pl.multiple_of(i * GLA_BLOCK, GLA_BLOCK), GLA_BLOCK)

    def prepare(slot, q, v, lf_f, lf_b):
        for d, lf in enumerate((lf_f, lf_b)):
            qs, ks, io, diag, decay = _gla_prepare(q, v, lf, stack_ref[d], reverse=bool(d))
            qs_scr[d, slot] = qs
            ks_scr[d, slot] = ks
            io_scr[d, slot] = io
            dg_scr[d, slot] = diag
            dc_scr[d, slot] = decay

    def scan(d, slot, v, st):
        return _gla_scan(qs_scr[d, slot], ks_scr[d, slot], io_scr[d, slot], dg_scr[d, slot], dc_scr[d, slot],
                         v, st, mask_ref[d], reverse=bool(d))

    prepare(0, qc_ref[...], ic_ref[...], ffc_ref[...], fbc_ref[...])

    def prep(i, carry):
        r = rows(i)
        prepare(i + 1, ql_ref[r, :], il_ref[r, :], ffl_ref[r, :], fbl_ref[r, :])
        return carry

    unroll = min(GLA_UNROLL, n_blocks)
    lax.fori_loop(0, n_blocks, prep, 0, unroll=unroll)

    zero_state = jnp.zeros((LANES, LANES), F32)
    o_cf, st_f = scan(0, 0, ic_ref[...], zero_state)
    o_cb, st_b = scan(1, 0, ic_ref[...], zero_state)
    oc_ref[...] = finish(o_cf + o_cb, gc_ref[...])

    def step(i, carry):
        st_f, st_b = carry
        j = n_blocks - 1 - i
        o_f, st_f = scan(0, i + 1, il_ref[rows(i), :], st_f)
        o_b, st_b = scan(1, j + 1, il_ref[rows(j), :], st_b)
        of_scr[rows(i), :] = o_f
        ob_scr[rows(j), :] = o_b
        return st_f, st_b

    lax.fori_loop(0, n_blocks, step, (st_f, st_b), unroll=unroll)

    def emit(i, carry):
        r = rows(i)
        ol_ref[r, :] = finish(of_scr[r, :] + ob_scr[r, :], gl_ref[r, :])
        return carry

    lax.fori_loop(0, n_blocks, emit, 0)


def _gla_call(qi_lat, lf_lat, vg_lat, qi_ctx, lf_ctx, vg_ctx, nw, *, batch):
    t_lat = qi_lat.shape[0] // batch
    t_ctx = qi_ctx.shape[0] // batch
    assert t_ctx == GLA_BLOCK and t_lat % GLA_BLOCK == 0
    h = REC_HEADS
    stack, masks = _gla_constants()
    n_slots = 1 + t_lat // GLA_BLOCK
    n_lvl = len(GLA_LEVELS)

    def col(off):
        return lambda b, hh: (b, off + hh)

    def seq(t):
        return [pl.BlockSpec((t, LANES), col(0)), pl.BlockSpec((t, LANES), col(h)),
                pl.BlockSpec((t, LANES), col(0)), pl.BlockSpec((t, LANES), col(h)),
                pl.BlockSpec((t, LANES), col(h))]

    return pl.pallas_call(
        _gla_kernel,
        grid=(batch, h),
        in_specs=seq(t_lat) + seq(t_ctx) + [
            pl.BlockSpec((1, LANES), lambda b, hh: (0, 0)),
            pl.BlockSpec(stack.shape, lambda b, hh: (0, 0, 0)),
            pl.BlockSpec(masks.shape, lambda b, hh: (0, 0, 0, 0)),
        ],
        out_specs=[pl.BlockSpec((t_lat, LANES), col(0)), pl.BlockSpec((t_ctx, LANES), col(0))],
        out_shape=[jax.ShapeDtypeStruct((batch * t_lat, h * LANES), BF16),
                   jax.ShapeDtypeStruct((batch * t_ctx, h * LANES), BF16)],
        scratch_shapes=[
            pltpu.VMEM((2, n_slots, GLA_BLOCK, n_lvl * LANES), BF16),
            pltpu.VMEM((2, n_slots, GLA_BLOCK, n_lvl * LANES), BF16),
            pltpu.VMEM((2, n_slots, GLA_BLOCK, 2 * LANES), BF16),
            pltpu.VMEM((2, n_slots, GLA_BLOCK, LANES), F32),
            pltpu.VMEM((2, n_slots, 8, LANES), F32),
            pltpu.VMEM((t_lat, LANES), F32), pltpu.VMEM((t_lat, LANES), F32),
        ],
        compiler_params=_cparams("arbitrary", "arbitrary"),
        name="gla",
    )(qi_lat, qi_lat, lf_lat, lf_lat, vg_lat, qi_ctx, qi_ctx, lf_ctx, lf_ctx, vg_ctx, nw, stack, masks)


def _top2_gates(logits):
    lane = lax.broadcasted_iota(jnp.int32, logits.shape, 1).astype(F32)
    big = float(LANES)
    m1 = jnp.max(logits, axis=-1, keepdims=True)
    i1 = jnp.min(jnp.where(logits == m1, lane, big), axis=-1, keepdims=True)
    rest = jnp.where(lane == i1, -jnp.inf, logits)
    m2 = jnp.max(rest, axis=-1, keepdims=True)
    i2 = jnp.min(jnp.where(rest == m2, lane, big), axis=-1, keepdims=True)
    e = jnp.exp(m2 - m1)
    w1 = 1.0 / (1.0 + e)
    return jnp.where(lane == i1, w1, 0.0) + jnp.where(lane == i2, e * w1, 0.0)


def _outproj_kernel(ya_ref, yr_ref, g_ref, x_ref, mod_ref, wua_ref, wur_ref, wo_ref, nw2_ref, *refs,
                    router, tiles_per_block):
    if router:
        rw_ref, tri_ref, xo_ref, h2a_ref, h2b_ref, route_ref, cnt_ref, carry_scr = refs
    else:
        xo_ref, h2_ref = refs
    d = x_ref.shape[1]
    ua = jnp.dot(ya_ref[...], wua_ref[...], preferred_element_type=F32)
    ur = jnp.dot(yr_ref[...], wur_ref[...], preferred_element_type=F32)
    u = g_ref[:, :d].astype(F32) * ua + g_ref[:, d:].astype(F32) * ur
    y = jnp.dot(u.astype(BF16), wo_ref[...], preferred_element_type=F32)
    xn = x_ref[...] + mod_ref[2:3, :] * y
    xo_ref[...] = xn
    h2 = (_rms(xn) * nw2_ref[...]) * (1.0 + mod_ref[4:5, :]) + mod_ref[3:4, :]
    h2_hi = h2.astype(BF16)
    if not router:
        h2_ref[...] = h2_hi
    if router:
        h2a_ref[...], h2b_ref[...] = _pack_row(h2)
        h2_lo = (h2 - h2_hi.astype(F32)).astype(BF16)
        logits = (jnp.dot(h2_hi, rw_ref[0], preferred_element_type=F32)
                  + jnp.dot(h2_lo, rw_ref[0], preferred_element_type=F32)
                  + jnp.dot(h2_hi, rw_ref[1], preferred_element_type=F32))
        lane = lax.broadcasted_iota(jnp.int32, logits.shape, 1)
        gates = _top2_gates(jnp.where(lane < N_EXPERTS, logits, -jnp.inf))

        @pl.when(pl.program_id(0) % tiles_per_block == 0)
        def _():
            carry_scr[...] = jnp.zeros_like(carry_scr)

        sel = gates > 0.0
        sel_f = jnp.where(sel, 1.0, 0.0)
        rank = jnp.dot(tri_ref[...], sel_f.astype(BF16), preferred_element_type=F32) + carry_scr[...]
        total = carry_scr[...] + jnp.sum(sel_f, axis=0, keepdims=True)
        carry_scr[...] = total
        cnt_ref[...] = total

        lanef = lane.astype(F32)
        e_a = jnp.min(jnp.where(sel, lanef, float(LANES)), axis=-1, keepdims=True)
        e_b = jnp.max(jnp.where(sel, lanef, -1.0), axis=-1, keepdims=True)
        is_a = lanef == e_a
        is_b = lanef == e_b

        def pick(mask, v):
            return jnp.sum(jnp.where(mask, v, 0.0), axis=-1, keepdims=True)

        fields = (pick(is_a, rank), pick(is_b, rank), e_a, e_b, pick(is_a, gates),
                  jnp.where(e_b == e_a, 0.0, pick(is_b, gates)))
        route = jnp.zeros_like(gates)
        for k, v in enumerate(fields):
            route = jnp.where(lane == k, v, route)
        route_ref[...] = route


def _outproj_call(ya, yr, g, x, mod, wua, wur, wo, nw2, rw, *, tm, rows_per_mod, route_block=None):
    n, d = x.shape
    assert n % tm == 0, (n, tm)
    router = rw is not None
    row = lambda i: (i, 0)
    const = lambda i: (0, 0)
    tiles_per_block = route_block // tm if router else 1
    if rows_per_mod is None:
        mod_map = lambda i: (mod.shape[0] - 1, 0, 0)
    else:
        mod_map = lambda i: (i // (rows_per_mod // tm), 0, 0)
    in_specs = [
        pl.BlockSpec((tm, ya.shape[1]), row),
        pl.BlockSpec((tm, yr.shape[1]), row),
        pl.BlockSpec((tm, 2 * d), row),
        pl.BlockSpec((tm, d), row),
        pl.BlockSpec((None, 6, d), mod_map),
        pl.BlockSpec(wua.shape, const),
        pl.BlockSpec(wur.shape, const),
        pl.BlockSpec(wo.shape, const),
        pl.BlockSpec((1, d), const),
    ]
    args = [ya, yr, g, x, mod, wua, wur, wo, nw2]
    out_specs = [pl.BlockSpec((tm, d), row)]
    out_shape = [jax.ShapeDtypeStruct((n, d), F32)]
    scratch = []
    if router:
        tri = jnp.asarray(np.tril(np.ones((tm, tm), np.float32), -1), BF16)
        in_specs += [pl.BlockSpec(rw.shape, lambda i: (0, 0, 0)), pl.BlockSpec(tri.shape, const)]
        args += [rw, tri]
        out_specs += [pl.BlockSpec((tm, d // 4), row), pl.BlockSpec((tm, d // 4), row),
                      pl.BlockSpec((tm, LANES), row),
                      pl.BlockSpec((None, 1, LANES), lambda i: (i // tiles_per_block, 0, 0))]
        out_shape += [jax.ShapeDtypeStruct((n, d // 4), jnp.uint32), jax.ShapeDtypeStruct((n, d // 4), jnp.uint32),
                      jax.ShapeDtypeStruct((n, LANES), F32),
                      jax.ShapeDtypeStruct((n // route_block, 1, LANES), F32)]
        scratch = [pltpu.VMEM((1, LANES), F32)]
    else:
        out_specs.append(pl.BlockSpec((tm, d), row))
        out_shape.append(jax.ShapeDtypeStruct((n, d), BF16))
    return pl.pallas_call(
        functools.partial(_outproj_kernel, router=router, tiles_per_block=tiles_per_block),
        grid=(n // tm,),
        in_specs=in_specs,
        out_specs=out_specs,
        out_shape=out_shape,
        scratch_shapes=scratch,
        compiler_params=_cparams("arbitrary"),
        name="outproj_router" if router else "outproj",
    )(*args)


def _swiglu_chunk(xs, w1_ref, w3_ref, w2_ref):
    a = jnp.dot(xs, w1_ref[...].astype(BF16), preferred_element_type=F32)
    b = jnp.dot(xs, w3_ref[...].astype(BF16), preferred_element_type=F32)
    return jnp.dot((_silu(a) * b).astype(BF16), w2_ref[...].astype(BF16), preferred_element_type=F32)


def _ffn_kernel(h_ref, x_ref, mod_ref, w1_ref, w3_ref, w2_ref, o_ref, *, tf):
    h = h_ref[...]
    ff = w1_ref.shape[1]
    acc = jnp.zeros(o_ref.shape, F32)
    for lo in range(0, ff, tf):
        hi = min(lo + tf, ff)
        a = jnp.dot(h, w1_ref[:, lo:hi], preferred_element_type=F32)
        b = jnp.dot(h, w3_ref[:, lo:hi], preferred_element_type=F32)
        acc = acc + jnp.dot((_silu(a) * b).astype(BF16), w2_ref[lo:hi, :], preferred_element_type=F32)
    o_ref[...] = x_ref[...] + mod_ref[5:6, :] * acc


def _ffn_call(h, x, mod, w1, w3, w2, *, tm, tf, rows_per_mod):
    n, d = x.shape
    assert n % tm == 0, (n, tm)
    row = lambda i: (i, 0)
    const = lambda i: (0, 0)
    if rows_per_mod is None:
        mod_map = lambda i: (mod.shape[0] - 1, 0, 0)
    else:
        mod_map = lambda i: (i // (rows_per_mod // tm), 0, 0)
    return pl.pallas_call(
        functools.partial(_ffn_kernel, tf=tf),
        grid=(n // tm,),
        in_specs=[
            pl.BlockSpec((tm, d), row), pl.BlockSpec((tm, d), row), pl.BlockSpec((None, 6, d), mod_map),
            pl.BlockSpec(w1.shape, const), pl.BlockSpec(w3.shape, const), pl.BlockSpec(w2.shape, const),
        ],
        out_specs=pl.BlockSpec((tm, d), row),
        out_shape=jax.ShapeDtypeStruct((n, d), F32),
        compiler_params=_cparams("arbitrary"),
        name="ffn",
    )(h, x, mod, w1, w3, w2)


def _pack_halves(x):
    n = x.shape[1] // 2
    lo = lax.bitcast_convert_type(x[:, :n].astype(BF16).astype(F32), jnp.uint32)
    hi = lax.bitcast_convert_type(x[:, n:].astype(BF16).astype(F32), jnp.uint32)
    return (hi & jnp.uint32(0xFFFF0000)) | (lo >> 16)


def _unpack_halves(w):
    lo = lax.bitcast_convert_type(w << 16, F32)
    hi = lax.bitcast_convert_type(w & jnp.uint32(0xFFFF0000), F32)
    return jnp.concatenate([lo, hi], axis=1)


def _sc_gather_rows(tables, idxs, *, window):
    n_jobs = len(tables)
    mesh = plsc.VectorSubcoreMesh(core_axis_name="c", subcore_axis_name="s")
    out_type = [jax.ShapeDtypeStruct((idx.shape[0], t.shape[1]), t.dtype) for t, idx in zip(tables, idxs)]
    for idx in idxs:
        assert idx.shape[0] % window == 0, (idx.shape, window)

    @pl.kernel(out_type=out_type, mesh=mesh)
    def gather(*refs):
        for j in range(n_jobs):
            table_hbm, idx_hbm, out_hbm = refs[j], refs[n_jobs + j], refs[2 * n_jobs + j]

            def body(idx_vmem, out_vmem, table_hbm=table_hbm):
                pltpu.sync_copy(table_hbm.at[idx_vmem.at[0]], out_vmem)

            pltpu.emit_pipeline(
                body,
                grid=(idx_hbm.shape[1] // window,),
                in_specs=[pl.BlockSpec((1, window), lambda i: (0, i))],
                out_specs=[pl.BlockSpec((window, table_hbm.shape[1]), lambda i: (i, 0))],
                core_axis_name=("c", "s"),
                dimension_semantics=(pltpu.PARALLEL,),
            )(idx_hbm, out_hbm)

    return gather(*tables, *[idx.reshape(1, -1) for idx in idxs])


def _packed_row(pair):
    return jnp.concatenate([_unpack_halves(pair[0][...]), _unpack_halves(pair[1][...])], axis=1)


def _pack_row(x):
    half = x.shape[1] // 2
    return _pack_halves(x[:, :half]), _pack_halves(x[:, half:])


def _experts_kernel(tile_expert_ref, n_active_ref, xa_ref, xb_ref, w1_ref, w3_ref, w2_ref, oa_ref, ob_ref,
                    x_scr, acc_ref):
    i, f = pl.program_id(0), pl.program_id(1)

    @pl.when(i < n_active_ref[0])
    def _():
        @pl.when(f == 0)
        def _():
            x_scr[...] = _packed_row((xa_ref, xb_ref)).astype(BF16)
            acc_ref[...] = jnp.zeros_like(acc_ref)

        acc_ref[...] += _swiglu_chunk(x_scr[...], w1_ref, w3_ref, w2_ref)

        @pl.when(f == pl.num_programs(1) - 1)
        def _():
            oa_ref[...], ob_ref[...] = _pack_row(acc_ref[...])


def _experts_call(tile_expert, n_active, xa, xb, w1, w3, w2, *, tm, tf):
    n, dp = xa.shape
    assert n % tm == 0, (n, tm)
    _, d, ff = w1.shape
    rows = pl.BlockSpec((tm, dp), lambda i, f, te, na: (i, 0))
    grid_spec = pltpu.PrefetchScalarGridSpec(
        num_scalar_prefetch=2,
        grid=(n // tm, ff // tf),
        in_specs=[
            rows, rows,
            pl.BlockSpec((None, d, tf), lambda i, f, te, na: (te[i], 0, f)),
            pl.BlockSpec((None, d, tf), lambda i, f, te, na: (te[i], 0, f)),
            pl.BlockSpec((None, tf, d), lambda i, f, te, na: (te[i], f, 0)),
        ],
        out_specs=[rows, rows],
        scratch_shapes=[pltpu.VMEM((tm, d), BF16), pltpu.VMEM((tm, d), F32)],
    )
    return pl.pallas_call(
        _experts_kernel,
        grid_spec=grid_spec,
        out_shape=[jax.ShapeDtypeStruct((n, dp), jnp.uint32)] * 2,
        compiler_params=_cparams("arbitrary", "arbitrary"),
        name="experts",
    )(tile_expert, n_active, xa, xb, w1, w3, w2)


def _combine_kernel(x_ref, yaa_ref, yab_ref, yba_ref, ybb_ref, route_ref, mod_ref, fnw_ref, o_ref):
    y = (route_ref[:, 4:5] * _packed_row((yaa_ref, yab_ref))
         + route_ref[:, 5:6] * _packed_row((yba_ref, ybb_ref)))
    o_ref[...] = _rms(x_ref[...] + mod_ref[5:6, :] * y) * fnw_ref[...]


def _combine_call(x, ys, route, mod, fnw, *, tm, rows_per_mod):
    n, d = x.shape
    assert n % tm == 0, (n, tm)
    row = lambda i: (i, 0)
    return pl.pallas_call(
        _combine_kernel,
        grid=(n // tm,),
        in_specs=[pl.BlockSpec((tm, d), row)] + [pl.BlockSpec((tm, d // 4), row)] * 4 + [
            pl.BlockSpec((tm, LANES), row),
            pl.BlockSpec((None, 6, d), lambda i: (i // (rows_per_mod // tm), 0, 0)),
            pl.BlockSpec((1, d), lambda i: (0, 0))],
        out_specs=pl.BlockSpec((tm, d), row),
        out_shape=jax.ShapeDtypeStruct((n, d), F32),
        compiler_params=_cparams("arbitrary"),
        name="combine",
    )(x, *ys, route, mod, fnw)


def _dispatch_plan(route, counts, *, tm):
    n = route.shape[0]
    n_e = counts.shape[0]
    n_slots = TOP_K * n + n_e * tm
    seg = (counts + tm - 1) // tm * tm
    ends = jnp.cumsum(seg)
    offs = ends - seg
    rec = route[:, :4].astype(jnp.int32)
    slot_a = jnp.take(offs, rec[:, 2]) + rec[:, 0]
    slot_b = jnp.take(offs, rec[:, 3]) + rec[:, 1]
    n_tiles = n_slots // tm
    n_active = (ends[-1] // tm).astype(jnp.int32)
    tile_start = jnp.arange(n_tiles, dtype=jnp.int32) * tm
    tile_start = jnp.minimum(tile_start, jnp.maximum(ends[-1] - tm, 0))
    tile_expert = jnp.minimum(jnp.sum(tile_start[:, None] >= ends[None, :], axis=1).astype(jnp.int32), n_e - 1)

    tok = jnp.arange(n, dtype=jnp.int32)
    key_b = jnp.where(rec[:, 3] == rec[:, 2], n_slots + tok, slot_b)
    _, by_slot = lax.sort_key_val(jnp.concatenate([slot_a, key_b]), jnp.concatenate([tok, tok]))
    e_of_slot = jnp.repeat(tile_expert, tm)
    packed_before = jnp.cumsum(counts) - counts
    src = jnp.arange(n_slots, dtype=jnp.int32) - jnp.take(offs, e_of_slot) + jnp.take(packed_before, e_of_slot)
    token_of_slot = jnp.take(by_slot, jnp.clip(src, 0, TOP_K * n - 1))
    return token_of_slot, slot_a, slot_b, tile_expert, n_active.reshape(1)


def _rope_tables(t_lat):
    rows = t_lat // GRID_W
    row = jnp.repeat(jnp.arange(rows, dtype=F32), GRID_W)
    col = jnp.tile(jnp.arange(GRID_W, dtype=F32), rows)
    n_freq = ATT_QK_DIM // 4
    inv_freq = ROPE_THETA ** (-jnp.arange(n_freq, dtype=F32) / n_freq)
    ang = jnp.concatenate([row[:, None] * inv_freq, col[:, None] * inv_freq], axis=-1)
    cos, sin = jnp.cos(ang), jnp.sin(ang)
    return (jnp.concatenate([cos, cos, cos, cos], axis=-1), jnp.concatenate([-sin, -sin, sin, sin], axis=-1))


def _layer_lower_bounds(lb_param):
    cs = jnp.cumsum(jax.nn.softmax(lb_param.astype(F32), axis=0), axis=0)
    return cs - cs[0:1]


def _win_columns(w_in_l):
    c = [w_in_l[:, i * 512:(i + 1) * 512] for i in range(8)]
    half = ATT_QK_DIM // 2

    def pair_halves(w):
        w = w.reshape(w.shape[0], ATT_HEADS, 2, 2, half)
        return w.transpose(0, 1, 3, 2, 4).reshape(w.shape[0], -1)

    return jnp.concatenate([pair_halves(c[0]), pair_halves(c[1]), c[2], c[7], c[3], c[4], c[5], c[6],
                            w_in_l[:, 4096:]], axis=1).astype(BF16)


def _pad_ff(w, axis, mult):
    ff = w.shape[axis]
    pad = (-ff) % mult
    if pad == 0:
        return w
    widths = [(0, 0)] * w.ndim
    widths[axis] = (0, pad)
    return jnp.pad(w, widths)


def kernel(x, c, ctx, c_ctx, w_ada, b_ada, norm_mix_w, norm_ffn_w, w_in, lambda_q1, lambda_k1, lambda_q2,
           lambda_k2, att_norm_w, rec_norm_w, lb_fwd, lb_bwd, w_up_att, w_up_rec, w_out, ffn_w1, ffn_w3,
           ffn_w2, router_w, moe_w1, moe_w3, moe_w2, final_norm_w):
    batch, t_lat, d = x.shape
    t_ctx = ctx.shape[1]
    depth = w_ada.shape[0]
    n_lat, n_ctx = batch * t_lat, batch * t_ctx
    tm = 512
    tm_in = 512
    tq = min(1024, t_lat)

    xl = x.reshape(n_lat, d)
    xc = ctx.reshape(n_ctx, d)

    pad_rows = (-(batch + 1)) % 8
    cc = jnp.concatenate([c, jnp.zeros((pad_rows, d), F32), c_ctx[None, :]], axis=0)
    mod_all = _mod_call(cc, w_ada, b_ada).reshape(depth, cc.shape[0], 6, d)

    cos, sin = _rope_tables(t_lat)
    lbs_f = _layer_lower_bounds(lb_fwd)
    lbs_b = _layer_lower_bounds(lb_bwd)

    for l in range(depth):
        last = l == depth - 1
        mod = mod_all[l]
        lam_init = 0.8 - 0.6 * math.exp(-0.3 * l)
        lam = (jnp.exp(jnp.sum(lambda_q1[l] * lambda_k1[l])) - jnp.exp(jnp.sum(lambda_q2[l] * lambda_k2[l]))
               + lam_init).reshape(1).astype(F32)
        lb = jnp.concatenate([lbs_f[l], lbs_b[l]])[None, :]
        w_l = _win_columns(w_in[l])
        nw = norm_mix_w[l][None, :]

        pl_lat = _inproj_call(xl, mod, nw, w_l, cos, sin, lb, tm=tm_in, rows_per_mod=t_lat, rope=True)
        pl_ctx = _inproj_call(xc, mod, nw, w_l, cos, sin, lb, tm=min(tm_in, n_ctx), rows_per_mod=None,
                              rope=False)
        qk_l, vg_l, qi_l, lf_l, g_l = pl_lat
        qk_c, vg_c, qi_c, lf_c, g_c = pl_ctx

        anw = att_norm_w[l][None, :]
        ya_l = _attn_call(lam, qk_l, qk_l, vg_l, qk_c, vg_c, anw, batch=batch, tq=tq,
                          lam_init=lam_init, with_lat=True)
        yr_l, yr_c = _gla_call(qi_l, lf_l, vg_l, qi_c, lf_c, vg_c, rec_norm_w[l][None, :], batch=batch)

        wua = w_up_att[l].astype(BF16)
        wur = w_up_rec[l].astype(BF16)
        wo = w_out[l].astype(BF16)
        nw2 = norm_ffn_w[l][None, :]
        moe_layer = l % 2 == 1
        j = l // 2
        assert moe_layer == last
        rw = None
        if moe_layer:
            rw32 = jnp.pad(router_w[j], ((0, 0), (0, LANES - N_EXPERTS)))
            rw_hi = rw32.astype(BF16)
            rw = jnp.stack([rw_hi, (rw32 - rw_hi.astype(F32)).astype(BF16)])
        res = _outproj_call(ya_l, yr_l, g_l, xl, mod, wua, wur, wo, nw2, rw, tm=tm, rows_per_mod=t_lat,
                            route_block=n_lat)
        xl = res[0]
        if not last:
            ya_c = _attn_call(lam, qk_c, None, None, qk_c, vg_c, anw, batch=batch, tq=t_ctx,
                              lam_init=lam_init, with_lat=False)
            xc, h2_c = _outproj_call(ya_c, yr_c, g_c, xc, mod, wua, wur, wo, nw2, None, tm=tm, rows_per_mod=None)

        if moe_layer:
            h2a, h2b, route, cnt = res[1:]
            counts = cnt[0, 0, :N_EXPERTS].astype(jnp.int32)
            mt = min(MOE_TILE, n_lat)
            token_of_slot, slot_a, slot_b, tile_expert, n_active = _dispatch_plan(route, counts, tm=mt)
            xa, xb = _sc_gather_rows([h2a, h2b], [token_of_slot] * 2, window=SC_WINDOW)
            oa, ob = _experts_call(tile_expert, n_active, xa, xb, moe_w1[j], moe_w3[j], moe_w2[j], tm=mt, tf=512)
            ys = _sc_gather_rows([oa, ob, oa, ob], [slot_a, slot_a, slot_b, slot_b], window=SC_WINDOW)
            xl = _combine_call(xl, ys, route, mod, final_norm_w[None, :], tm=tm, rows_per_mod=t_lat)
        else:
            h2_l = res[1]
            w1 = _pad_ff(ffn_w1[j], 1, 2 * LANES).astype(BF16)
            w3 = _pad_ff(ffn_w3[j], 1, 2 * LANES).astype(BF16)
            w2 = _pad_ff(ffn_w2[j], 0, 2 * LANES).astype(BF16)
            tf = 512
            xl = _ffn_call(h2_l, xl, mod, w1, w3, w2, tm=tm, tf=tf, rows_per_mod=t_lat)
            xc = _ffn_call(h2_c, xc, mod, w1, w3, w2, tm=tm, tf=tf, rows_per_mod=None)

    return xl.reshape(batch, t_lat, d)
```

```python
import functools
import math

import numpy as np
import jax
import jax.numpy as jnp
from jax import lax
from jax.experimental import pallas as pl
from jax.experimental.pallas import tpu as pltpu
from jax.experimental.pallas import tpu_sc as plsc

F32 = jnp.float32
BF16 = jnp.bfloat16
HIGHEST = lax.Precision.HIGHEST

EPS = 1e-6
GRID_W = 64
ROPE_THETA = 10000.0
ATT_HEADS = 4
ATT_QK_DIM = 64
REC_HEADS = 4
N_EXPERTS = 8
TOP_K = 2
Q_SCALE = ATT_QK_DIM ** -0.5 * math.log2(math.e)

LANES = 128
GLA_BLOCK = 256
GLA_CHUNK = 256
GLA_UNROLL = 4
GLA_LEVELS = tuple(2 ** i for i in range(GLA_CHUNK.bit_length() - 1))
VMEM_LIMIT = 56 * 1024 * 1024
MOE_TILE = 1024
SC_WINDOW = 128
MOE_PARTS = 2
ATT_KEYS = 256
ATT_ROWS = 512

NT_DIMS = (((1,), (1,)), ((), ()))
TN_DIMS = (((0,), (0,)), ((), ()))


def _cparams(*sem):
    return pltpu.CompilerParams(dimension_semantics=sem, vmem_limit_bytes=VMEM_LIMIT)


def _silu(a):
    return a * jax.nn.sigmoid(a)


def _rms(x):
    return x * lax.rsqrt(jnp.mean(x * x, axis=-1, keepdims=True) + EPS)


def _mod_kernel(c_ref, w_ref, b_ref, o_ref):
    s = _silu(c_ref[...])
    o_ref[...] = jnp.dot(s, w_ref[...], precision=HIGHEST, preferred_element_type=F32) + b_ref[...]


def _mod_call(cc, w_ada, b_ada):
    depth, d, n = w_ada.shape
    rows = cc.shape[0]
    tn = 1536
    return pl.pallas_call(
        _mod_kernel,
        grid=(depth, n // tn),
        in_specs=[
            pl.BlockSpec((rows, d), lambda l, j: (0, 0)),
            pl.BlockSpec((None, d, tn), lambda l, j: (l, 0, j)),
            pl.BlockSpec((None, 1, tn), lambda l, j: (l, 0, j)),
        ],
        out_specs=pl.BlockSpec((None, rows, tn), lambda l, j: (l, 0, j)),
        out_shape=jax.ShapeDtypeStruct((depth, rows, n), F32),
        compiler_params=_cparams("arbitrary", "arbitrary"),
        name="mod",
    )(cc, w_ada, b_ada.reshape(depth, 1, n))


def _log_forget(z, lb):
    t = jnp.exp(-jnp.abs(z))
    num = jnp.where(z >= 0.0, 1.0 + lb * t, lb + t)
    return jnp.where(num > 0.0, jnp.log(num / (1.0 + t)), z)


def _inproj_kernel(x_ref, mod_ref, nw_ref, w_ref, cos_ref, sin_ref, lb_ref,
                   qk_ref, vg_ref, qi_ref, lf_ref, g_ref, *, rope):
    h = _rms(x_ref[...]) * nw_ref[...]
    h = (h * (1.0 + mod_ref[1:2, :]) + mod_ref[0:1, :]).astype(BF16)
    ts = qk_ref.shape[1]
    half = ts // 2

    def proj(s):
        return jnp.dot(h, w_ref[:, s * ts:(s + 1) * ts], preferred_element_type=F32)

    p = proj(0)
    if rope:
        cos = cos_ref[...]
        sin = sin_ref[...]
    for j in range(ts // LANES):
        blk = p[:, j * LANES:(j + 1) * LANES]
        if rope:
            blk = blk * cos + pltpu.roll(blk, LANES // 2, 1) * sin
        if j * LANES < half:
            blk = blk * Q_SCALE
        qk_ref[:, j * LANES:(j + 1) * LANES] = blk.astype(BF16)

    p = proj(1)
    vg_ref[:, :half] = p[:, :half].astype(BF16)
    vg_ref[:, half:] = _silu(p[:, half:]).astype(BF16)

    p = proj(2)
    qi_ref[:, :half] = _silu(p[:, :half]).astype(BF16)
    qi_ref[:, half:] = p[:, half:].astype(BF16)

    lf_ref[...] = _log_forget(proj(3), lb_ref[...])

    g_ref[:, :ts] = jax.nn.sigmoid(proj(4)).astype(BF16)
    g_ref[:, ts:] = jax.nn.sigmoid(proj(5)).astype(BF16)


def _inproj_call(x, mod, nw, w, cos, sin, lb, *, tm, rows_per_mod, rope):
    n, d = x.shape
    assert n % tm == 0, (n, tm)
    ts = 1024
    assert w.shape[1] == 6 * ts
    pos_tiles = cos.shape[0] // tm
    row = lambda i: (i, 0)
    const = lambda i: (0, 0)
    if rows_per_mod is None:
        mod_map = lambda i: (mod.shape[0] - 1, 0, 0)
    else:
        mod_map = lambda i: (i // (rows_per_mod // tm), 0, 0)
    outs = pl.pallas_call(
        functools.partial(_inproj_kernel, rope=rope),
        grid=(n // tm,),
        in_specs=[
            pl.BlockSpec((tm, d), row),
            pl.BlockSpec((None, 6, d), mod_map),
            pl.BlockSpec((1, d), const),
            pl.BlockSpec(w.shape, const),
            pl.BlockSpec((tm, LANES), lambda i: (i % pos_tiles, 0)),
            pl.BlockSpec((tm, LANES), lambda i: (i % pos_tiles, 0)),
            pl.BlockSpec((1, ts), const),
        ],
        out_specs=[
            pl.BlockSpec((tm, ts), row),
            pl.BlockSpec((tm, ts), row),
            pl.BlockSpec((tm, ts), row),
            pl.BlockSpec((tm, ts), row),
            pl.BlockSpec((tm, 2 * ts), row),
        ],
        out_shape=[
            jax.ShapeDtypeStruct((n, ts), BF16),
            jax.ShapeDtypeStruct((n, ts), BF16),
            jax.ShapeDtypeStruct((n, ts), BF16),
            jax.ShapeDtypeStruct((n, ts), F32),
            jax.ShapeDtypeStruct((n, 2 * ts), BF16),
        ],
        compiler_params=_cparams("arbitrary"),
        name="inproj_rope" if rope else "inproj",
    )(x, mod, nw, w, cos, sin, lb)
    return outs


def _attn_kernel(lam_ref, q_ref, *refs, post_scale, with_lat):
    if with_lat:
        kl_ref, vl_ref, kc_ref, vc_ref, nw_ref, o_ref, vce_scr, vle_scr = refs
    else:
        kc_ref, vc_ref, nw_ref, o_ref, vce_scr = refs

    @pl.when(pl.program_id(2) == 0)
    def _():
        vce_scr[:, :LANES] = vc_ref[...]
        vce_scr[:, LANES:] = jnp.ones(vc_ref.shape, BF16)
        if with_lat:
            vle_scr[:, :LANES] = vl_ref[...]
            vle_scr[:, LANES:] = jnp.ones(vl_ref.shape, BF16)

    n_parts = max(1, q_ref.shape[0] // ATT_ROWS)
    rows = q_ref.shape[0] // n_parts
    qq = []
    for part in range(n_parts):
        q = q_ref[part * rows:(part + 1) * rows, :]
        lane = lax.broadcasted_iota(jnp.int32, q.shape, 1)
        zero = jnp.zeros_like(q)
        sub1 = (lane % ATT_QK_DIM) < (ATT_QK_DIM // 2)
        qq.append(jnp.concatenate([jnp.where(sub1, q, zero), jnp.where(sub1, zero, q)], axis=0))

    blocks = [(kc_ref, vce_scr, 0, kc_ref.shape[0])]
    if with_lat:
        blocks += [(kl_ref, vle_scr, j, ATT_KEYS) for j in range(0, kl_ref.shape[0], ATT_KEYS)]
    m = [jnp.full((2 * rows, 1), -jnp.inf, F32)] * n_parts
    acc = [jnp.zeros((2 * rows, 2 * LANES), F32)] * n_parts
    for k_ref, v_scr, start, size in blocks:
        for part in range(n_parts):
            s = lax.dot_general(qq[part], k_ref[start:start + size, :], NT_DIMS, preferred_element_type=F32)
            m_new = jnp.maximum(m[part], jnp.max(s, axis=-1, keepdims=True))
            p = jnp.exp2(s - m_new).astype(BF16)
            acc[part] = acc[part] * jnp.exp2(m[part] - m_new) + jnp.dot(
                p, v_scr[start:start + size, :], preferred_element_type=F32)
            m[part] = m_new
    for part in range(n_parts):
        on = acc[part][:, :LANES] * (1.0 / acc[part][:, LANES:])
        o = on[:rows] - lam_ref[0] * on[rows:]
        o_ref[part * rows:(part + 1) * rows, :] = (_rms(o) * nw_ref[...] * post_scale).astype(BF16)


def _attn_call(lam, q_src, qk_lat, vg_lat, qk_ctx, vg_ctx, nw, *, batch, tq, lam_init, with_lat):
    n_q = q_src.shape[0]
    t_q = n_q // batch
    t_lat = qk_lat.shape[0] // batch if with_lat else 0
    t_ctx = qk_ctx.shape[0] // batch
    nq_tiles = t_q // tq
    h = ATT_HEADS
    in_specs = [
        pl.BlockSpec(memory_space=pltpu.SMEM),
        pl.BlockSpec((tq, LANES), lambda b, hh, i: (b * nq_tiles + i, hh)),
    ]
    args = [lam, q_src]
    if with_lat:
        in_specs += [
            pl.BlockSpec((t_lat, LANES), lambda b, hh, i: (b, h + hh)),
            pl.BlockSpec((t_lat, LANES), lambda b, hh, i: (b, hh)),
        ]
        args += [qk_lat, vg_lat]
    in_specs += [
        pl.BlockSpec((t_ctx, LANES), lambda b, hh, i: (b, h + hh)),
        pl.BlockSpec((t_ctx, LANES), lambda b, hh, i: (b, hh)),
        pl.BlockSpec((1, LANES), lambda b, hh, i: (0, 0)),
    ]
    args += [qk_ctx, vg_ctx, nw]
    return pl.pallas_call(
        functools.partial(_attn_kernel, post_scale=1.0 - lam_init, with_lat=with_lat),
        grid=(batch, h, nq_tiles),
        in_specs=in_specs,
        out_specs=pl.BlockSpec((tq, LANES), lambda b, hh, i: (b * nq_tiles + i, hh)),
        out_shape=jax.ShapeDtypeStruct((n_q, h * LANES), BF16),
        scratch_shapes=[pltpu.VMEM((t_ctx, 2 * LANES), BF16)]
        + ([pltpu.VMEM((t_lat, 2 * LANES), BF16)] if with_lat else []),
        compiler_params=_cparams("arbitrary", "arbitrary", "arbitrary"),
        name="attn_lat" if with_lat else "attn_ctx",
    )(*args)


def _gla_constants():
    n = GLA_BLOCK
    idx = np.arange(n)
    same_chunk = (idx[:, None] // GLA_CHUNK) == (idx[None, :] // GLA_CHUNK)
    stacks, masks = [], []
    for reverse in (False, True):
        order = (idx[None, :] >= idx[:, None]) if reverse else (idx[None, :] <= idx[:, None])
        cum = (same_chunk & order).astype(np.int32)
        groups, lvl_masks = [cum], []
        for m in GLA_LEVELS:
            ref = (idx // (2 * m)) * (2 * m) + (m if reverse else m - 1)
            groups.append(np.abs(cum - cum[ref]))
            upper = (idx % (2 * m)) >= m
            same = (idx[:, None] // (2 * m)) == (idx[None, :] // (2 * m))
            q_side, k_side = (~upper, upper) if reverse else (upper, ~upper)
            lvl_masks.append(same & q_side[:, None] & k_side[None, :])
        end = (idx // GLA_CHUNK) * GLA_CHUNK + (0 if reverse else GLA_CHUNK - 1)
        groups.append(np.abs(cum[end] - cum))
        stacks.append(np.concatenate(groups, axis=0))
        masks.append(np.stack(lvl_masks))
    return jnp.asarray(np.stack(stacks), BF16), jnp.asarray(np.stack(masks), F32)


def _gla_prepare(q, v, lf, stack, *, reverse):
    n = q.shape[0]
    qf = q.astype(F32)
    k = 1.0 - jnp.exp(lf)

    hi = lf.astype(BF16)
    mid = (lf - hi.astype(F32)).astype(BF16)
    expo = jnp.dot(stack, jnp.concatenate([hi, mid], axis=1), preferred_element_type=F32)

    def exponent(group):
        blk = expo[group * n:(group + 1) * n]
        return blk[:, :LANES] + blk[:, LANES:]

    scale = [jnp.exp(exponent(1 + li)) for li in range(len(GLA_LEVELS))]
    qs = jnp.concatenate([(qf * e).astype(BF16) for e in scale], axis=1)
    ks = jnp.concatenate([(k * e).astype(BF16) for e in scale], axis=1)
    c = exponent(0)
    io = jnp.concatenate([(qf * jnp.exp(c)).astype(BF16),
                          (k * jnp.exp(exponent(1 + len(GLA_LEVELS)))).astype(BF16)], axis=1)
    diag = jnp.sum(qf * k, axis=-1, keepdims=True) * v.astype(F32)
    last = 0 if reverse else GLA_CHUNK - 1
    n_chunks = n // GLA_CHUNK
    decay = jnp.concatenate([jnp.exp(c[ci * GLA_CHUNK + last:ci * GLA_CHUNK + last + 1, :])
                             for ci in range(n_chunks)] + [jnp.ones((8 - n_chunks, LANES), F32)], axis=0)
    return qs, ks, io, diag, decay


def _gla_scan(qs, ks, io, diag, decay, v, st, masks, *, reverse):
    n = v.shape[0]
    scores = jnp.zeros((n, n), F32)
    for li in range(len(GLA_LEVELS)):
        cols = slice(li * LANES, (li + 1) * LANES)
        s_l = lax.dot_general(qs[:, cols], ks[:, cols], NT_DIMS, preferred_element_type=F32)
        scores = scores + s_l * masks[li]
    o = jnp.dot(scores.astype(BF16), v, preferred_element_type=F32) + diag

    n_chunks = n // GLA_CHUNK
    outs = [None] * n_chunks
    order = range(n_chunks - 1, -1, -1) if reverse else range(n_chunks)
    for ci in order:
        r = slice(ci * GLA_CHUNK, (ci + 1) * GLA_CHUNK)
        outs[ci] = lax.dot_general(io[r, :LANES], st.astype(BF16), NT_DIMS, preferred_element_type=F32)
        upd = lax.dot_general(v[r], io[r, LANES:], TN_DIMS, preferred_element_type=F32)
        st = st * decay[ci:ci + 1, :] + upd
    return o + jnp.concatenate(outs, axis=0), st


def _gla_kernel(ql_ref, il_ref, ffl_ref, fbl_ref, gl_ref, qc_ref, ic_ref, ffc_ref, fbc_ref, gc_ref,
                nw_ref, stack_ref, mask_ref, ol_ref, oc_ref,
                qs_scr, ks_scr, io_scr, dg_scr, dc_scr, of_scr, ob_scr):
    n_blocks = ql_ref.shape[0] // GLA_BLOCK
    nw = nw_ref[...]

    def finish(o, g):
        return (_rms(o) * nw * g.astype(F32)).astype(BF16)

    def rows(i):
        return pl.ds(pl.multiple_of(i * GLA_BLOCK, GLA_BLOCK), GLA_BLOCK)

    def prepare(slot, q, v, lf_f, lf_b):
        for d, lf in enumerate((lf_f, lf_b)):
            qs, ks, io, diag, decay = _gla_prepare(q, v, lf, stack_ref[d], reverse=bool(d))
            qs_scr[d, slot] = qs
            ks_scr[d, slot] = ks
            io_scr[d, slot] = io
            dg_scr[d, slot] = diag
            dc_scr[d, slot] = decay

    def scan(d, slot, v, st):
        return _gla_scan(qs_scr[d, slot], ks_scr[d, slot], io_scr[d, slot], dg_scr[d, slot], dc_scr[d, slot],
                         v, st, mask_ref[d], reverse=bool(d))

    prepare(0, qc_ref[...], ic_ref[...], ffc_ref[...], fbc_ref[...])

    def prep(i, carry):
        r = rows(i)
        prepare(i + 1, ql_ref[r, :], il_ref[r, :], ffl_ref[r, :], fbl_ref[r, :])
        return carry

    unroll = min(GLA_UNROLL, n_blocks)
    lax.fori_loop(0, n_blocks, prep, 0, unroll=unroll)

    zero_state = jnp.zeros((LANES, LANES), F32)
    o_cf, st_f = scan(0, 0, ic_ref[...], zero_state)
    o_cb, st_b = scan(1, 0, ic_ref[...], zero_state)
    oc_ref[...] = finish(o_cf + o_cb, gc_ref[...])

    def step(i, carry):
        st_f, st_b = carry
        j = n_blocks - 1 - i
        o_f, st_f = scan(0, i + 1, il_ref[rows(i), :], st_f)
        o_b, st_b = scan(1, j + 1, il_ref[rows(j), :], st_b)
        of_scr[rows(i), :] = o_f
        ob_scr[rows(j), :] = o_b
        return st_f, st_b

    lax.fori_loop(0, n_blocks, step, (st_f, st_b), unroll=unroll)

    def emit(i, carry):
        r = rows(i)
        ol_ref[r, :] = finish(of_scr[r, :] + ob_scr[r, :], gl_ref[r, :])
        return carry

    lax.fori_loop(0, n_blocks, emit, 0)


def _gla_call(qi_lat, lf_lat, vg_lat, qi_ctx, lf_ctx, vg_ctx, nw, *, batch):
    t_lat = qi_lat.shape[0] // batch
    t_ctx = qi_ctx.shape[0] // batch
    assert t_ctx == GLA_BLOCK and t_lat % GLA_BLOCK == 0
    h = REC_HEADS
    stack, masks = _gla_constants()
    n_slots = 1 + t_lat // GLA_BLOCK
    n_lvl = len(GLA_LEVELS)

    def col(off):
        return lambda b, hh: (b, off + hh)

    def seq(t):
        return [pl.BlockSpec((t, LANES), col(0)), pl.BlockSpec((t, LANES), col(h)),
                pl.BlockSpec((t, LANES), col(0)), pl.BlockSpec((t, LANES), col(h)),
                pl.BlockSpec((t, LANES), col(h))]

    return pl.pallas_call(
        _gla_kernel,
        grid=(batch, h),
        in_specs=seq(t_lat) + seq(t_ctx) + [
            pl.BlockSpec((1, LANES), lambda b, hh: (0, 0)),
            pl.BlockSpec(stack.shape, lambda b, hh: (0, 0, 0)),
            pl.BlockSpec(masks.shape, lambda b, hh: (0, 0, 0, 0)),
        ],
        out_specs=[pl.BlockSpec((t_lat, LANES), col(0)), pl.BlockSpec((t_ctx, LANES), col(0))],
        out_shape=[jax.ShapeDtypeStruct((batch * t_lat, h * LANES), BF16),
                   jax.ShapeDtypeStruct((batch * t_ctx, h * LANES), BF16)],
        scratch_shapes=[
            pltpu.VMEM((2, n_slots, GLA_BLOCK, n_lvl * LANES), BF16),
            pltpu.VMEM((2, n_slots, GLA_BLOCK, n_lvl * LANES), BF16),
            pltpu.VMEM((2, n_slots, GLA_BLOCK, 2 * LANES), BF16),
            pltpu.VMEM((2, n_slots, GLA_BLOCK, LANES), F32),
            pltpu.VMEM((2, n_slots, 8, LANES), F32),
            pltpu.VMEM((t_lat, LANES), F32), pltpu.VMEM((t_lat, LANES), F32),
        ],
        compiler_params=_cparams("arbitrary", "arbitrary"),
        name="gla",
    )(qi_lat, qi_lat, lf_lat, lf_lat, vg_lat, qi_ctx, qi_ctx, lf_ctx, lf_ctx, vg_ctx, nw, stack, masks)


def _top2_gates(logits):
    lane = lax.broadcasted_iota(jnp.int32, logits.shape, 1).astype(F32)
    big = float(LANES)
    m1 = jnp.max(logits, axis=-1, keepdims=True)
    i1 = jnp.min(jnp.where(logits == m1, lane, big), axis=-1, keepdims=True)
    rest = jnp.where(lane == i1, -jnp.inf, logits)
    m2 = jnp.max(rest, axis=-1, keepdims=True)
    i2 = jnp.min(jnp.where(rest == m2, lane, big), axis=-1, keepdims=True)
    e = jnp.exp(m2 - m1)
    w1 = 1.0 / (1.0 + e)
    return jnp.where(lane == i1, w1, 0.0) + jnp.where(lane == i2, e * w1, 0.0)


def _outproj_kernel(ya_ref, yr_ref, g_ref, x_ref, mod_ref, wua_ref, wur_ref, wo_ref, nw2_ref, *refs,
                    router, tiles_per_block):
    if router:
        rw_ref, tri_ref, xo_ref, h2a_ref, h2b_ref, route_ref, cnt_ref, carry_scr = refs
    else:
        xo_ref, h2_ref = refs
    d = x_ref.shape[1]
    ua = jnp.dot(ya_ref[...], wua_ref[...], preferred_element_type=F32)
    ur = jnp.dot(yr_ref[...], wur_ref[...], preferred_element_type=F32)
    u = g_ref[:, :d].astype(F32) * ua + g_ref[:, d:].astype(F32) * ur
    y = jnp.dot(u.astype(BF16), wo_ref[...], preferred_element_type=F32)
    xn = x_ref[...] + mod_ref[2:3, :] * y
    xo_ref[...] = xn
    h2 = (_rms(xn) * nw2_ref[...]) * (1.0 + mod_ref[4:5, :]) + mod_ref[3:4, :]
    h2_hi = h2.astype(BF16)
    if not router:
        h2_ref[...] = h2_hi
    if router:
        h2a_ref[...], h2b_ref[...] = _pack_row(h2)
        h2_lo = (h2 - h2_hi.astype(F32)).astype(BF16)
        logits = (jnp.dot(h2_hi, rw_ref[0], preferred_element_type=F32)
                  + jnp.dot(h2_lo, rw_ref[0], preferred_element_type=F32)
                  + jnp.dot(h2_hi, rw_ref[1], preferred_element_type=F32))
        lane = lax.broadcasted_iota(jnp.int32, logits.shape, 1)
        gates = _top2_gates(jnp.where(lane < N_EXPERTS, logits, -jnp.inf))

        @pl.when(pl.program_id(0) % tiles_per_block == 0)
        def _():
            carry_scr[...] = jnp.zeros_like(carry_scr)

        sel = gates > 0.0
        sel_f = jnp.where(sel, 1.0, 0.0)
        rank = jnp.dot(tri_ref[...], sel_f.astype(BF16), preferred_element_type=F32) + carry_scr[...]
        total = carry_scr[...] + jnp.sum(sel_f, axis=0, keepdims=True)
        carry_scr[...] = total
        cnt_ref[...] = total

        lanef = lane.astype(F32)
        e_a = jnp.min(jnp.where(sel, lanef, float(LANES)), axis=-1, keepdims=True)
        e_b = jnp.max(jnp.where(sel, lanef, -1.0), axis=-1, keepdims=True)
        is_a = lanef == e_a
        is_b = lanef == e_b

        def pick(mask, v):
            return jnp.sum(jnp.where(mask, v, 0.0), axis=-1, keepdims=True)

        fields = (pick(is_a, rank), pick(is_b, rank), e_a, e_b, pick(is_a, gates),
                  jnp.where(e_b == e_a, 0.0, pick(is_b, gates)))
        route = jnp.zeros_like(gates)
        for k, v in enumerate(fields):
            route = jnp.where(lane == k, v, route)
        route_ref[...] = route


def _outproj_call(ya, yr, g, x, mod, wua, wur, wo, nw2, rw, *, tm, rows_per_mod, route_block=None):
    n, d = x.shape
    assert n % tm == 0, (n, tm)
    router = rw is not None
    row = lambda i: (i, 0)
    const = lambda i: (0, 0)
    tiles_per_block = route_block // tm if router else 1
    if rows_per_mod is None:
        mod_map = lambda i: (mod.shape[0] - 1, 0, 0)
    else:
        mod_map = lambda i: (i // (rows_per_mod // tm), 0, 0)
    in_specs = [
        pl.BlockSpec((tm, ya.shape[1]), row),
        pl.BlockSpec((tm, yr.shape[1]), row),
        pl.BlockSpec((tm, 2 * d), row),
        pl.BlockSpec((tm, d), row),
        pl.BlockSpec((None, 6, d), mod_map),
        pl.BlockSpec(wua.shape, const),
        pl.BlockSpec(wur.shape, const),
        pl.BlockSpec(wo.shape, const),
        pl.BlockSpec((1, d), const),
    ]
    args = [ya, yr, g, x, mod, wua, wur, wo, nw2]
    out_specs = [pl.BlockSpec((tm, d), row)]
    out_shape = [jax.ShapeDtypeStruct((n, d), F32)]
    scratch = []
    if router:
        tri = jnp.asarray(np.tril(np.ones((tm, tm), np.float32), -1), BF16)
        in_specs += [pl.BlockSpec(rw.shape, lambda i: (0, 0, 0)), pl.BlockSpec(tri.shape, const)]
        args += [rw, tri]
        out_specs += [pl.BlockSpec((tm, d // 4), row), pl.BlockSpec((tm, d // 4), row),
                      pl.BlockSpec((tm, LANES), row),
                      pl.BlockSpec((None, 1, LANES), lambda i: (i // tiles_per_block, 0, 0))]
        out_shape += [jax.ShapeDtypeStruct((n, d // 4), jnp.uint32), jax.ShapeDtypeStruct((n, d // 4), jnp.uint32),
                      jax.ShapeDtypeStruct((n, LANES), F32),
                      jax.ShapeDtypeStruct((n // route_block, 1, LANES), F32)]
        scratch = [pltpu.VMEM((1, LANES), F32)]
    else:
        out_specs.append(pl.BlockSpec((tm, d), row))
        out_shape.append(jax.ShapeDtypeStruct((n, d), BF16))
    return pl.pallas_call(
        functools.partial(_outproj_kernel, router=router, tiles_per_block=tiles_per_block),
        grid=(n // tm,),
        in_specs=in_specs,
        out_specs=out_specs,
        out_shape=out_shape,
        scratch_shapes=scratch,
        compiler_params=_cparams("arbitrary"),
        name="outproj_router" if router else "outproj",
    )(*args)


def _swiglu_chunk(xs, w1_ref, w3_ref, w2_ref):
    a = jnp.dot(xs, w1_ref[...].astype(BF16), preferred_element_type=F32)
    b = jnp.dot(xs, w3_ref[...].astype(BF16), preferred_element_type=F32)
    return jnp.dot((_silu(a) * b).astype(BF16), w2_ref[...].astype(BF16), preferred_element_type=F32)


def _ffn_kernel(h_ref, x_ref, mod_ref, w1_ref, w3_ref, w2_ref, o_ref, *, tf):
    h = h_ref[...]
    ff = w1_ref.shape[1]
    acc = jnp.zeros(o_ref.shape, F32)
    for lo in range(0, ff, tf):
        hi = min(lo + tf, ff)
        a = jnp.dot(h, w1_ref[:, lo:hi], preferred_element_type=F32)
        b = jnp.dot(h, w3_ref[:, lo:hi], preferred_element_type=F32)
        acc = acc + jnp.dot((_silu(a) * b).astype(BF16), w2_ref[lo:hi, :], preferred_element_type=F32)
    o_ref[...] = x_ref[...] + mod_ref[5:6, :] * acc


def _ffn_call(h, x, mod, w1, w3, w2, *, tm, tf, rows_per_mod):
    n, d = x.shape
    assert n % tm == 0, (n, tm)
    row = lambda i: (i, 0)
    const = lambda i: (0, 0)
    if rows_per_mod is None:
        mod_map = lambda i: (mod.shape[0] - 1, 0, 0)
    else:
        mod_map = lambda i: (i // (rows_per_mod // tm), 0, 0)
    return pl.pallas_call(
        functools.partial(_ffn_kernel, tf=tf),
        grid=(n // tm,),
        in_specs=[
            pl.BlockSpec((tm, d), row), pl.BlockSpec((tm, d), row), pl.BlockSpec((None, 6, d), mod_map),
            pl.BlockSpec(w1.shape, const), pl.BlockSpec(w3.shape, const), pl.BlockSpec(w2.shape, const),
        ],
        out_specs=pl.BlockSpec((tm, d), row),
        out_shape=jax.ShapeDtypeStruct((n, d), F32),
        compiler_params=_cparams("arbitrary"),
        name="ffn",
    )(h, x, mod, w1, w3, w2)


def _pack_halves(x):
    n = x.shape[1] // 2
    lo = lax.bitcast_convert_type(x[:, :n].astype(BF16).astype(F32), jnp.uint32)
    hi = lax.bitcast_convert_type(x[:, n:].astype(BF16).astype(F32), jnp.uint32)
    return (hi & jnp.uint32(0xFFFF0000)) | (lo >> 16)


def _unpack_halves(w):
    lo = lax.bitcast_convert_type(w << 16, F32)
    hi = lax.bitcast_convert_type(w & jnp.uint32(0xFFFF0000), F32)
    return jnp.concatenate([lo, hi], axis=1)


def _sc_gather_rows(tables, idxs, *, window):
    n_jobs = len(tables)
    mesh = plsc.VectorSubcoreMesh(core_axis_name="c", subcore_axis_name="s")
    out_type = [jax.ShapeDtypeStruct((idx.shape[0], t.shape[1]), t.dtype) for t, idx in zip(tables, idxs)]
    for idx in idxs:
        assert idx.shape[0] % window == 0, (idx.shape, window)

    @pl.kernel(out_type=out_type, mesh=mesh)
    def gather(*refs):
        for j in range(n_jobs):
            table_hbm, idx_hbm, out_hbm = refs[j], refs[n_jobs + j], refs[2 * n_jobs + j]

            def body(idx_vmem, out_vmem, table_hbm=table_hbm):
                pltpu.sync_copy(table_hbm.at[idx_vmem.at[0]], out_vmem)

            pltpu.emit_pipeline(
                body,
                grid=(idx_hbm.shape[1] // window,),
                in_specs=[pl.BlockSpec((1, window), lambda i: (0, i))],
                out_specs=[pl.BlockSpec((window, table_hbm.shape[1]), lambda i: (i, 0))],
                core_axis_name=("c", "s"),
                dimension_semantics=(pltpu.PARALLEL,),
            )(idx_hbm, out_hbm)

    return gather(*tables, *[idx.reshape(1, -1) for idx in idxs])


def _packed_row(pair):
    return jnp.concatenate([_unpack_halves(pair[0][...]), _unpack_halves(pair[1][...])], axis=1)


def _pack_row(x):
    half = x.shape[1] // 2
    return _pack_halves(x[:, :half]), _pack_halves(x[:, half:])


def _experts_kernel(tile_expert_ref, n_active_ref, xa_ref, xb_ref, w1_ref, w3_ref, w2_ref, oa_ref, ob_ref,
                    x_scr, acc_ref):
    i, f = pl.program_id(0), pl.program_id(1)

    @pl.when(i < n_active_ref[0])
    def _():
        @pl.when(f == 0)
        def _():
            x_scr[...] = _packed_row((xa_ref, xb_ref)).astype(BF16)
            acc_ref[...] = jnp.zeros_like(acc_ref)

        acc_ref[...] += _swiglu_chunk(x_scr[...], w1_ref, w3_ref, w2_ref)

        @pl.when(f == pl.num_programs(1) - 1)
        def _():
            oa_ref[...], ob_ref[...] = _pack_row(acc_ref[...])


def _experts_call(tile_expert, n_active, xa, xb, w1, w3, w2, *, tm, tf):
    n, dp = xa.shape
    assert n % tm == 0, (n, tm)
    _, d, ff = w1.shape
    rows = pl.BlockSpec((tm, dp), lambda i, f, te, na: (i, 0))
    grid_spec = pltpu.PrefetchScalarGridSpec(
        num_scalar_prefetch=2,
        grid=(n // tm, ff // tf),
        in_specs=[
            rows, rows,
            pl.BlockSpec((None, d, tf), lambda i, f, te, na: (te[i], 0, f)),
            pl.BlockSpec((None, d, tf), lambda i, f, te, na: (te[i], 0, f)),
            pl.BlockSpec((None, tf, d), lambda i, f, te, na: (te[i], f, 0)),
        ],
        out_specs=[rows, rows],
        scratch_shapes=[pltpu.VMEM((tm, d), BF16), pltpu.VMEM((tm, d), F32)],
    )
    return pl.pallas_call(
        _experts_kernel,
        grid_spec=grid_spec,
        out_shape=[jax.ShapeDtypeStruct((n, dp), jnp.uint32)] * 2,
        compiler_params=_cparams("arbitrary", "arbitrary"),
        name="experts",
    )(tile_expert, n_active, xa, xb, w1, w3, w2)


def _combine_kernel(x_ref, yaa_ref, yab_ref, yba_ref, ybb_ref, route_ref, mod_ref, fnw_ref, *refs):
    o_ref = refs[-1]
    y = (route_ref[:, 4:5] * _packed_row((yaa_ref, yab_ref))
         + route_ref[:, 5:6] * _packed_row((yba_ref, ybb_ref)))
    o_ref[...] = _rms(x_ref[...] + mod_ref[5:6, :] * y) * fnw_ref[...]


def _combine_call(x, ys, route, mod, fnw, prev, *, tm, rows_per_mod, part, n_parts):
    n, d = x.shape
    rows = n // n_parts
    assert rows % tm == 0, (n, n_parts, tm)
    first = part * (rows // tm)
    row = lambda i: (first + i, 0)
    part_row = lambda i: (i, 0)
    in_specs = [pl.BlockSpec((tm, d), row)] + [pl.BlockSpec((tm, d // 4), part_row)] * 4 + [
        pl.BlockSpec((tm, LANES), row),
        pl.BlockSpec((None, 6, d), lambda i: ((first + i) // (rows_per_mod // tm), 0, 0)),
        pl.BlockSpec((1, d), lambda i: (0, 0))]
    args = [x, *ys, route, mod, fnw]
    aliases = {}
    if prev is not None:
        in_specs.append(pl.BlockSpec(memory_space=pl.ANY))
        aliases = {len(args): 0}
        args.append(prev)
    return pl.pallas_call(
        _combine_kernel,
        grid=(rows // tm,),
        in_specs=in_specs,
        out_specs=pl.BlockSpec((tm, d), row),
        out_shape=jax.ShapeDtypeStruct((n, d), F32),
        input_output_aliases=aliases,
        compiler_params=_cparams("arbitrary"),
        name="combine",
    )(*args)


def _dispatch_plan(route, counts, *, tm):
    n = route.shape[0]
    n_e = counts.shape[0]
    n_slots = TOP_K * n + n_e * tm
    seg = (counts + tm - 1) // tm * tm
    ends = jnp.cumsum(seg)
    offs = ends - seg
    rec = route[:, :4].astype(jnp.int32)
    slot_a = jnp.take(offs, rec[:, 2]) + rec[:, 0]
    slot_b = jnp.take(offs, rec[:, 3]) + rec[:, 1]
    n_tiles = n_slots // tm
    n_active = (ends[-1] // tm).astype(jnp.int32)
    tile_start = jnp.arange(n_tiles, dtype=jnp.int32) * tm
    tile_start = jnp.minimum(tile_start, jnp.maximum(ends[-1] - tm, 0))
    tile_expert = jnp.minimum(jnp.sum(tile_start[:, None] >= ends[None, :], axis=1).astype(jnp.int32), n_e - 1)

    tok = jnp.arange(n, dtype=jnp.int32)
    key_b = jnp.where(rec[:, 3] == rec[:, 2], n_slots + tok, slot_b)
    _, by_slot = lax.sort_key_val(jnp.concatenate([slot_a, key_b]), jnp.concatenate([tok, tok]))
    e_of_slot = jnp.repeat(tile_expert, tm)
    packed_before = jnp.cumsum(counts) - counts
    src = jnp.arange(n_slots, dtype=jnp.int32) - jnp.take(offs, e_of_slot) + jnp.take(packed_before, e_of_slot)
    token_of_slot = jnp.take(by_slot, jnp.clip(src, 0, TOP_K * n - 1))
    return token_of_slot, slot_a, slot_b, tile_expert, n_active.reshape(1)


def _rope_tables(t_lat):
    rows = t_lat // GRID_W
    row = jnp.repeat(jnp.arange(rows, dtype=F32), GRID_W)
    col = jnp.tile(jnp.arange(GRID_W, dtype=F32), rows)
    n_freq = ATT_QK_DIM // 4
    inv_freq = ROPE_THETA ** (-jnp.arange(n_freq, dtype=F32) / n_freq)
    ang = jnp.concatenate([row[:, None] * inv_freq, col[:, None] * inv_freq], axis=-1)
    cos, sin = jnp.cos(ang), jnp.sin(ang)
    return (jnp.concatenate([cos, cos, cos, cos], axis=-1), jnp.concatenate([-sin, -sin, sin, sin], axis=-1))


def _layer_lower_bounds(lb_param):
    cs = jnp.cumsum(jax.nn.softmax(lb_param.astype(F32), axis=0), axis=0)
    return cs - cs[0:1]


def _win_columns(w_in_l):
    c = [w_in_l[:, i * 512:(i + 1) * 512] for i in range(8)]
    half = ATT_QK_DIM // 2

    def pair_halves(w):
        w = w.reshape(w.shape[0], ATT_HEADS, 2, 2, half)
        return w.transpose(0, 1, 3, 2, 4).reshape(w.shape[0], -1)

    return jnp.concatenate([pair_halves(c[0]), pair_halves(c[1]), c[2], c[7], c[3], c[4], c[5], c[6],
                            w_in_l[:, 4096:]], axis=1).astype(BF16)


def _pad_ff(w, axis, mult):
    ff = w.shape[axis]
    pad = (-ff) % mult
    if pad == 0:
        return w
    widths = [(0, 0)] * w.ndim
    widths[axis] = (0, pad)
    return jnp.pad(w, widths)


def kernel(x, c, ctx, c_ctx, w_ada, b_ada, norm_mix_w, norm_ffn_w, w_in, lambda_q1, lambda_k1, lambda_q2,
           lambda_k2, att_norm_w, rec_norm_w, lb_fwd, lb_bwd, w_up_att, w_up_rec, w_out, ffn_w1, ffn_w3,
           ffn_w2, router_w, moe_w1, moe_w3, moe_w2, final_norm_w):
    batch, t_lat, d = x.shape
    t_ctx = ctx.shape[1]
    depth = w_ada.shape[0]
    n_lat, n_ctx = batch * t_lat, batch * t_ctx
    tm = 512
    tm_in = 512
    tq = min(1024, t_lat)

    xl = x.reshape(n_lat, d)
    xc = ctx.reshape(n_ctx, d)

    pad_rows = (-(batch + 1)) % 8
    cc = jnp.concatenate([c, jnp.zeros((pad_rows, d), F32), c_ctx[None, :]], axis=0)
    mod_all = _mod_call(cc, w_ada, b_ada).reshape(depth, cc.shape[0], 6, d)

    cos, sin = _rope_tables(t_lat)
    lbs_f = _layer_lower_bounds(lb_fwd)
    lbs_b = _layer_lower_bounds(lb_bwd)

    for l in range(depth):
        last = l == depth - 1
        mod = mod_all[l]
        lam_init = 0.8 - 0.6 * math.exp(-0.3 * l)
        lam = (jnp.exp(jnp.sum(lambda_q1[l] * lambda_k1[l])) - jnp.exp(jnp.sum(lambda_q2[l] * lambda_k2[l]))
               + lam_init).reshape(1).astype(F32)
        lb = jnp.concatenate([lbs_f[l], lbs_b[l]])[None, :]
        w_l = _win_columns(w_in[l])
        nw = norm_mix_w[l][None, :]

        pl_lat = _inproj_call(xl, mod, nw, w_l, cos, sin, lb, tm=tm_in, rows_per_mod=t_lat, rope=True)
        pl_ctx = _inproj_call(xc, mod, nw, w_l, cos, sin, lb, tm=min(tm_in, n_ctx), rows_per_mod=None,
                              rope=False)
        qk_l, vg_l, qi_l, lf_l, g_l = pl_lat
        qk_c, vg_c, qi_c, lf_c, g_c = pl_ctx

        anw = att_norm_w[l][None, :]
        ya_l = _attn_call(lam, qk_l, qk_l, vg_l, qk_c, vg_c, anw, batch=batch, tq=tq,
                          lam_init=lam_init, with_lat=True)
        yr_l, yr_c = _gla_call(qi_l, lf_l, vg_l, qi_c, lf_c, vg_c, rec_norm_w[l][None, :], batch=batch)

        wua = w_up_att[l].astype(BF16)
        wur = w_up_rec[l].astype(BF16)
        wo = w_out[l].astype(BF16)
        nw2 = norm_ffn_w[l][None, :]
        moe_layer = l % 2 == 1
        j = l // 2
        assert moe_layer == last
        rw = None
        if moe_layer:
            rw32 = jnp.pad(router_w[j], ((0, 0), (0, LANES - N_EXPERTS)))
            rw_hi = rw32.astype(BF16)
            rw = jnp.stack([rw_hi, (rw32 - rw_hi.astype(F32)).astype(BF16)])
        res = _outproj_call(ya_l, yr_l, g_l, xl, mod, wua, wur, wo, nw2, rw, tm=tm, rows_per_mod=t_lat,
                            route_block=n_lat)
        xl = res[0]
        if not last:
            ya_c = _attn_call(lam, qk_c, None, None, qk_c, vg_c, anw, batch=batch, tq=t_ctx,
                              lam_init=lam_init, with_lat=False)
            xc, h2_c = _outproj_call(ya_c, yr_c, g_c, xc, mod, wua, wur, wo, nw2, None, tm=tm, rows_per_mod=None)

        if moe_layer:
            h2a, h2b, route, cnt = res[1:]
            counts = cnt[0, 0, :N_EXPERTS].astype(jnp.int32)
            mt = min(MOE_TILE, n_lat)
            token_of_slot, slot_a, slot_b, tile_expert, n_active = _dispatch_plan(route, counts, tm=mt)
            xa, xb = _sc_gather_rows([h2a, h2b], [token_of_slot] * 2, window=SC_WINDOW)
            oa, ob = _experts_call(tile_expert, n_active, xa, xb, moe_w1[j], moe_w3[j], moe_w2[j], tm=mt, tf=512)
            out = None
            rows = n_lat // MOE_PARTS
            for part in range(MOE_PARTS):
                sa = slot_a[part * rows:(part + 1) * rows]
                sb = slot_b[part * rows:(part + 1) * rows]
                ys = _sc_gather_rows([oa, ob, oa, ob], [sa, sa, sb, sb], window=SC_WINDOW)
                out = _combine_call(xl, ys, route, mod, final_norm_w[None, :], out, tm=tm, rows_per_mod=t_lat,
                                    part=part, n_parts=MOE_PARTS)
            xl = out
        else:
            h2_l = res[1]
            w1 = _pad_ff(ffn_w1[j], 1, 2 * LANES).astype(BF16)
            w3 = _pad_ff(ffn_w3[j], 1, 2 * LANES).astype(BF16)
            w2 = _pad_ff(ffn_w2[j], 0, 2 * LANES).astype(BF16)
            tf = 512
            xl = _ffn_call(h2_l, xl, mod, w1, w3, w2, tm=tm, tf=tf, rows_per_mod=t_lat)
            xc = _ffn_call(h2_c, xc, mod, w1, w3, w2, tm=tm, tf=tf, rows_per_mod=None)

    return xl.reshape(batch, t_lat, d)
```

```python
import functools
import math

import numpy as np
import jax
import jax.numpy as jnp
from jax import lax
from jax.experimental import pallas as pl
from jax.experimental.pallas import tpu as pltpu
from jax.experimental.pallas import tpu_sc as plsc

F32 = jnp.float32
BF16 = jnp.bfloat16
HIGHEST = lax.Precision.HIGHEST

EPS = 1e-6
GRID_W = 64
ROPE_THETA = 10000.0
ATT_HEADS = 4
ATT_QK_DIM = 64
REC_HEADS = 4
N_EXPERTS = 8
TOP_K = 2
Q_SCALE = ATT_QK_DIM ** -0.5 * math.log2(math.e)

LANES = 128
GLA_BLOCK = 256
GLA_CHUNK = 256
GLA_UNROLL = 4
GLA_LEVELS = tuple(2 ** i for i in range(GLA_CHUNK.bit_length() - 1))
VMEM_LIMIT = 56 * 1024 * 1024
MOE_TILE = 1024
SC_WINDOW = 128
ATT_KEYS = 256
ATT_ROWS = 512

NT_DIMS = (((1,), (1,)), ((), ()))
TN_DIMS = (((0,), (0,)), ((), ()))


def _cparams(*sem):
    return pltpu.CompilerParams(dimension_semantics=sem, vmem_limit_bytes=VMEM_LIMIT)


def _silu(a):
    return a * jax.nn.sigmoid(a)


def _rms(x):
    return x * lax.rsqrt(jnp.mean(x * x, axis=-1, keepdims=True) + EPS)


def _mod_kernel(c_ref, w_ref, b_ref, o_ref):
    s = _silu(c_ref[...])
    o_ref[...] = jnp.dot(s, w_ref[...], precision=HIGHEST, preferred_element_type=F32) + b_ref[...]


def _mod_call(cc, w_ada, b_ada):
    depth, d, n = w_ada.shape
    rows = cc.shape[0]
    tn = 1536
    return pl.pallas_call(
        _mod_kernel,
        grid=(depth, n // tn),
        in_specs=[
            pl.BlockSpec((rows, d), lambda l, j: (0, 0)),
            pl.BlockSpec((None, d, tn), lambda l, j: (l, 0, j)),
            pl.BlockSpec((None, 1, tn), lambda l, j: (l, 0, j)),
        ],
        out_specs=pl.BlockSpec((None, rows, tn), lambda l, j: (l, 0, j)),
        out_shape=jax.ShapeDtypeStruct((depth, rows, n), F32),
        compiler_params=_cparams("arbitrary", "arbitrary"),
        name="mod",
    )(cc, w_ada, b_ada.reshape(depth, 1, n))


def _log_forget(z, lb):
    t = jnp.exp(-jnp.abs(z))
    num = jnp.where(z >= 0.0, 1.0 + lb * t, lb + t)
    return jnp.where(num > 0.0, jnp.log(num / (1.0 + t)), z)


def _inproj_kernel(x_ref, mod_ref, nw_ref, w_ref, cos_ref, sin_ref, lb_ref,
                   qk_ref, vg_ref, qi_ref, lf_ref, g_ref, *, rope):
    h = _rms(x_ref[...]) * nw_ref[...]
    h = (h * (1.0 + mod_ref[1:2, :]) + mod_ref[0:1, :]).astype(BF16)
    ts = qk_ref.shape[1]
    half = ts // 2

    def proj(s):
        return jnp.dot(h, w_ref[:, s * ts:(s + 1) * ts], preferred_element_type=F32)

    p = proj(0)
    if rope:
        cos = cos_ref[...]
        sin = sin_ref[...]
    for j in range(ts // LANES):
        blk = p[:, j * LANES:(j + 1) * LANES]
        if rope:
            blk = blk * cos + pltpu.roll(blk, LANES // 2, 1) * sin
        if j * LANES < half:
            blk = blk * Q_SCALE
        qk_ref[:, j * LANES:(j + 1) * LANES] = blk.astype(BF16)

    p = proj(1)
    vg_ref[:, :half] = p[:, :half].astype(BF16)
    vg_ref[:, half:] = _silu(p[:, half:]).astype(BF16)

    p = proj(2)
    qi_ref[:, :half] = _silu(p[:, :half]).astype(BF16)
    qi_ref[:, half:] = p[:, half:].astype(BF16)

    lf_ref[...] = _log_forget(proj(3), lb_ref[...])

    g_ref[:, :ts] = jax.nn.sigmoid(proj(4)).astype(BF16)
    g_ref[:, ts:] = jax.nn.sigmoid(proj(5)).astype(BF16)


def _inproj_call(x, mod, nw, w, cos, sin, lb, *, tm, rows_per_mod, rope):
    n, d = x.shape
    assert n % tm == 0, (n, tm)
    ts = 1024
    assert w.shape[1] == 6 * ts
    pos_tiles = cos.shape[0] // tm
    row = lambda i: (i, 0)
    const = lambda i: (0, 0)
    if rows_per_mod is None:
        mod_map = lambda i: (mod.shape[0] - 1, 0, 0)
    else:
        mod_map = lambda i: (i // (rows_per_mod // tm), 0, 0)
    outs = pl.pallas_call(
        functools.partial(_inproj_kernel, rope=rope),
        grid=(n // tm,),
        in_specs=[
            pl.BlockSpec((tm, d), row),
            pl.BlockSpec((None, 6, d), mod_map),
            pl.BlockSpec((1, d), const),
            pl.BlockSpec(w.shape, const),
            pl.BlockSpec((tm, LANES), lambda i: (i % pos_tiles, 0)),
            pl.BlockSpec((tm, LANES), lambda i: (i % pos_tiles, 0)),
            pl.BlockSpec((1, ts), const),
        ],
        out_specs=[
            pl.BlockSpec((tm, ts), row),
            pl.BlockSpec((tm, ts), row),
            pl.BlockSpec((tm, ts), row),
            pl.BlockSpec((tm, ts), row),
            pl.BlockSpec((tm, 2 * ts), row),
        ],
        out_shape=[
            jax.ShapeDtypeStruct((n, ts), BF16),
            jax.ShapeDtypeStruct((n, ts), BF16),
            jax.ShapeDtypeStruct((n, ts), BF16),
            jax.ShapeDtypeStruct((n, ts), F32),
            jax.ShapeDtypeStruct((n, 2 * ts), BF16),
        ],
        compiler_params=_cparams("arbitrary"),
        name="inproj_rope" if rope else "inproj",
    )(x, mod, nw, w, cos, sin, lb)
    return outs


def _attn_kernel(lam_ref, q_ref, *refs, post_scale, with_lat):
    if with_lat:
        kl_ref, vl_ref, kc_ref, vc_ref, nw_ref, o_ref, vce_scr, vle_scr = refs
    else:
        kc_ref, vc_ref, nw_ref, o_ref, vce_scr = refs

    @pl.when(pl.program_id(2) == 0)
    def _():
        vce_scr[:, :LANES] = vc_ref[...]
        vce_scr[:, LANES:] = jnp.ones(vc_ref.shape, BF16)
        if with_lat:
            vle_scr[:, :LANES] = vl_ref[...]
            vle_scr[:, LANES:] = jnp.ones(vl_ref.shape, BF16)

    n_parts = max(1, q_ref.shape[0] // ATT_ROWS)
    rows = q_ref.shape[0] // n_parts
    qq = []
    for part in range(n_parts):
        q = q_ref[part * rows:(part + 1) * rows, :]
        lane = lax.broadcasted_iota(jnp.int32, q.shape, 1)
        zero = jnp.zeros_like(q)
        sub1 = (lane % ATT_QK_DIM) < (ATT_QK_DIM // 2)
        qq.append(jnp.concatenate([jnp.where(sub1, q, zero), jnp.where(sub1, zero, q)], axis=0))

    blocks = [(kc_ref, vce_scr, 0, kc_ref.shape[0])]
    if with_lat:
        blocks += [(kl_ref, vle_scr, j, ATT_KEYS) for j in range(0, kl_ref.shape[0], ATT_KEYS)]
    m = [jnp.full((2 * rows, 1), -jnp.inf, F32)] * n_parts
    acc = [jnp.zeros((2 * rows, 2 * LANES), F32)] * n_parts
    for k_ref, v_scr, start, size in blocks:
        for part in range(n_parts):
            s = lax.dot_general(qq[part], k_ref[start:start + size, :], NT_DIMS, preferred_element_type=F32)
            m_new = jnp.maximum(m[part], jnp.max(s, axis=-1, keepdims=True))
            p = jnp.exp2(s - m_new).astype(BF16)
            acc[part] = acc[part] * jnp.exp2(m[part] - m_new) + jnp.dot(
                p, v_scr[start:start + size, :], preferred_element_type=F32)
            m[part] = m_new
    for part in range(n_parts):
        on = acc[part][:, :LANES] * (1.0 / acc[part][:, LANES:])
        o = on[:rows] - lam_ref[0] * on[rows:]
        o_ref[part * rows:(part + 1) * rows, :] = (_rms(o) * nw_ref[...] * post_scale).astype(BF16)


def _attn_call(lam, q_src, qk_lat, vg_lat, qk_ctx, vg_ctx, nw, *, batch, tq, lam_init, with_lat):
    n_q = q_src.shape[0]
    t_q = n_q // batch
    t_lat = qk_lat.shape[0] // batch if with_lat else 0
    t_ctx = qk_ctx.shape[0] // batch
    nq_tiles = t_q // tq
    h = ATT_HEADS
    in_specs = [
        pl.BlockSpec(memory_space=pltpu.SMEM),
        pl.BlockSpec((tq, LANES), lambda b, hh, i: (b * nq_tiles + i, hh)),
    ]
    args = [lam, q_src]
    if with_lat:
        in_specs += [
            pl.BlockSpec((t_lat, LANES), lambda b, hh, i: (b, h + hh)),
            pl.BlockSpec((t_lat, LANES), lambda b, hh, i: (b, hh)),
        ]
        args += [qk_lat, vg_lat]
    in_specs += [
        pl.BlockSpec((t_ctx, LANES), lambda b, hh, i: (b, h + hh)),
        pl.BlockSpec((t_ctx, LANES), lambda b, hh, i: (b, hh)),
        pl.BlockSpec((1, LANES), lambda b, hh, i: (0, 0)),
    ]
    args += [qk_ctx, vg_ctx, nw]
    return pl.pallas_call(
        functools.partial(_attn_kernel, post_scale=1.0 - lam_init, with_lat=with_lat),
        grid=(batch, h, nq_tiles),
        in_specs=in_specs,
        out_specs=pl.BlockSpec((tq, LANES), lambda b, hh, i: (b * nq_tiles + i, hh)),
        out_shape=jax.ShapeDtypeStruct((n_q, h * LANES), BF16),
        scratch_shapes=[pltpu.VMEM((t_ctx, 2 * LANES), BF16)]
        + ([pltpu.VMEM((t_lat, 2 * LANES), BF16)] if with_lat else []),
        compiler_params=_cparams("arbitrary", "arbitrary", "arbitrary"),
        name="attn_lat" if with_lat else "attn_ctx",
    )(*args)


def _gla_constants():
    n = GLA_BLOCK
    idx = np.arange(n)
    same_chunk = (idx[:, None] // GLA_CHUNK) == (idx[None, :] // GLA_CHUNK)
    stacks, masks = [], []
    for reverse in (False, True):
        order = (idx[None, :] >= idx[:, None]) if reverse else (idx[None, :] <= idx[:, None])
        cum = (same_chunk & order).astype(np.int32)
        groups, lvl_masks = [cum], []
        for m in GLA_LEVELS:
            ref = (idx // (2 * m)) * (2 * m) + (m if reverse else m - 1)
            groups.append(np.abs(cum - cum[ref]))
            upper = (idx % (2 * m)) >= m
            same = (idx[:, None] // (2 * m)) == (idx[None, :] // (2 * m))
            q_side, k_side = (~upper, upper) if reverse else (upper, ~upper)
            lvl_masks.append(same & q_side[:, None] & k_side[None, :])
        end = (idx // GLA_CHUNK) * GLA_CHUNK + (0 if reverse else GLA_CHUNK - 1)
        groups.append(np.abs(cum[end] - cum))
        stacks.append(np.concatenate(groups, axis=0))
        masks.append(np.stack(lvl_masks))
    return jnp.asarray(np.stack(stacks), BF16), jnp.asarray(np.stack(masks), F32)


def _gla_prepare(q, v, lf, stack, *, reverse):
    n = q.shape[0]
    qf = q.astype(F32)
    k = 1.0 - jnp.exp(lf)

    hi = lf.astype(BF16)
    mid = (lf - hi.astype(F32)).astype(BF16)
    expo = jnp.dot(stack, jnp.concatenate([hi, mid], axis=1), preferred_element_type=F32)

    def exponent(group):
        blk = expo[group * n:(group + 1) * n]
        return blk[:, :LANES] + blk[:, LANES:]

    scale = [jnp.exp(exponent(1 + li)) for li in range(len(GLA_LEVELS))]
    qs = jnp.concatenate([(qf * e).astype(BF16) for e in scale], axis=1)
    ks = jnp.concatenate([(k * e).astype(BF16) for e in scale], axis=1)
    c = exponent(0)
    io = jnp.concatenate([(qf * jnp.exp(c)).astype(BF16),
                          (k * jnp.exp(exponent(1 + len(GLA_LEVELS)))).astype(BF16)], axis=1)
    diag = jnp.sum(qf * k, axis=-1, keepdims=True) * v.astype(F32)
    last = 0 if reverse else GLA_CHUNK - 1
    n_chunks = n // GLA_CHUNK
    decay = jnp.concatenate([jnp.exp(c[ci * GLA_CHUNK + last:ci * GLA_CHUNK + last + 1, :])
                             for ci in range(n_chunks)] + [jnp.ones((8 - n_chunks, LANES), F32)], axis=0)
    return qs, ks, io, diag, decay


def _gla_scan(qs, ks, io, diag, decay, v, st, masks, *, reverse):
    n = v.shape[0]
    scores = jnp.zeros((n, n), F32)
    for li in range(len(GLA_LEVELS)):
        cols = slice(li * LANES, (li + 1) * LANES)
        s_l = lax.dot_general(qs[:, cols], ks[:, cols], NT_DIMS, preferred_element_type=F32)
        scores = scores + s_l * masks[li]
    o = jnp.dot(scores.astype(BF16), v, preferred_element_type=F32) + diag

    n_chunks = n // GLA_CHUNK
    outs = [None] * n_chunks
    order = range(n_chunks - 1, -1, -1) if reverse else range(n_chunks)
    for ci in order:
        r = slice(ci * GLA_CHUNK, (ci + 1) * GLA_CHUNK)
        outs[ci] = lax.dot_general(io[r, :LANES], st.astype(BF16), NT_DIMS, preferred_element_type=F32)
        upd = lax.dot_general(v[r], io[r, LANES:], TN_DIMS, preferred_element_type=F32)
        st = st * decay[ci:ci + 1, :] + upd
    return o + jnp.concatenate(outs, axis=0), st


def _gla_kernel(ql_ref, il_ref, ffl_ref, fbl_ref, gl_ref, qc_ref, ic_ref, ffc_ref, fbc_ref, gc_ref,
                nw_ref, stack_ref, mask_ref, ol_ref, oc_ref,
                qs_scr, ks_scr, io_scr, dg_scr, dc_scr, of_scr, ob_scr):
    n_blocks = ql_ref.shape[0] // GLA_BLOCK
    nw = nw_ref[...]

    def finish(o, g):
        return (_rms(o) * nw * g.astype(F32)).astype(BF16)

    def rows(i):
        return pl.ds(pl.multiple_of(i * GLA_BLOCK, GLA_BLOCK), GLA_BLOCK)

    def prepare(slot, q, v, lf_f, lf_b):
        for d, lf in enumerate((lf_f, lf_b)):
            qs, ks, io, diag, decay = _gla_prepare(q, v, lf, stack_ref[d], reverse=bool(d))
            qs_scr[d, slot] = qs
            ks_scr[d, slot] = ks
            io_scr[d, slot] = io
            dg_scr[d, slot] = diag
            dc_scr[d, slot] = decay

    def scan(d, slot, v, st):
        return _gla_scan(qs_scr[d, slot], ks_scr[d, slot], io_scr[d, slot], dg_scr[d, slot], dc_scr[d, slot],
                         v, st, mask_ref[d], reverse=bool(d))

    prepare(0, qc_ref[...], ic_ref[...], ffc_ref[...], fbc_ref[...])

    def prep(i, carry):
        r = rows(i)
        prepare(i + 1, ql_ref[r, :], il_ref[r, :], ffl_ref[r, :], fbl_ref[r, :])
        return carry

    unroll = min(GLA_UNROLL, n_blocks)
    lax.fori_loop(0, n_blocks, prep, 0, unroll=unroll)

    zero_state = jnp.zeros((LANES, LANES), F32)
    o_cf, st_f = scan(0, 0, ic_ref[...], zero_state)
    o_cb, st_b = scan(1, 0, ic_ref[...], zero_state)
    oc_ref[...] = finish(o_cf + o_cb, gc_ref[...])

    def step(i, carry):
        st_f, st_b = carry
        j = n_blocks - 1 - i
        o_f, st_f = scan(0, i + 1, il_ref[rows(i), :], st_f)
        o_b, st_b = scan(1, j + 1, il_ref[rows(j), :], st_b)
        of_scr[rows(i), :] = o_f
        ob_scr[rows(j), :] = o_b
        return st_f, st_b

    lax.fori_loop(0, n_blocks, step, (st_f, st_b), unroll=min(2 * GLA_UNROLL, n_blocks))

    def emit(i, carry):
        r = rows(i)
        ol_ref[r, :] = finish(of_scr[r, :] + ob_scr[r, :], gl_ref[r, :])
        return carry

    lax.fori_loop(0, n_blocks, emit, 0, unroll=unroll)


def _gla_call(qi_lat, lf_lat, vg_lat, qi_ctx, lf_ctx, vg_ctx, nw, *, batch):
    t_lat = qi_lat.shape[0] // batch
    t_ctx = qi_ctx.shape[0] // batch
    assert t_ctx == GLA_BLOCK and t_lat % GLA_BLOCK == 0
    h = REC_HEADS
    stack, masks = _gla_constants()
    n_slots = 1 + t_lat // GLA_BLOCK
    n_lvl = len(GLA_LEVELS)

    def col(off):
        return lambda b, hh: (b, off + hh)

    def seq(t):
        return [pl.BlockSpec((t, LANES), col(0)), pl.BlockSpec((t, LANES), col(h)),
                pl.BlockSpec((t, LANES), col(0)), pl.BlockSpec((t, LANES), col(h)),
                pl.BlockSpec((t, LANES), col(h))]

    return pl.pallas_call(
        _gla_kernel,
        grid=(batch, h),
        in_specs=seq(t_lat) + seq(t_ctx) + [
            pl.BlockSpec((1, LANES), lambda b, hh: (0, 0)),
            pl.BlockSpec(stack.shape, lambda b, hh: (0, 0, 0)),
            pl.BlockSpec(masks.shape, lambda b, hh: (0, 0, 0, 0)),
        ],
        out_specs=[pl.BlockSpec((t_lat, LANES), col(0)), pl.BlockSpec((t_ctx, LANES), col(0))],
        out_shape=[jax.ShapeDtypeStruct((batch * t_lat, h * LANES), BF16),
                   jax.ShapeDtypeStruct((batch * t_ctx, h * LANES), BF16)],
        scratch_shapes=[
            pltpu.VMEM((2, n_slots, GLA_BLOCK, n_lvl * LANES), BF16),
            pltpu.VMEM((2, n_slots, GLA_BLOCK, n_lvl * LANES), BF16),
            pltpu.VMEM((2, n_slots, GLA_BLOCK, 2 * LANES), BF16),
            pltpu.VMEM((2, n_slots, GLA_BLOCK, LANES), F32),
            pltpu.VMEM((2, n_slots, 8, LANES), F32),
            pltpu.VMEM((t_lat, LANES), F32), pltpu.VMEM((t_lat, LANES), F32),
        ],
        compiler_params=_cparams("arbitrary", "arbitrary"),
        name="gla",
    )(qi_lat, qi_lat, lf_lat, lf_lat, vg_lat, qi_ctx, qi_ctx, lf_ctx, lf_ctx, vg_ctx, nw, stack, masks)


def _top2_gates(logits):
    lane = lax.broadcasted_iota(jnp.int32, logits.shape, 1).astype(F32)
    big = float(LANES)
    m1 = jnp.max(logits, axis=-1, keepdims=True)
    i1 = jnp.min(jnp.where(logits == m1, lane, big), axis=-1, keepdims=True)
    rest = jnp.where(lane == i1, -jnp.inf, logits)
    m2 = jnp.max(rest, axis=-1, keepdims=True)
    i2 = jnp.min(jnp.where(rest == m2, lane, big), axis=-1, keepdims=True)
    e = jnp.exp(m2 - m1)
    w1 = 1.0 / (1.0 + e)
    return jnp.where(lane == i1, w1, 0.0) + jnp.where(lane == i2, e * w1, 0.0)


def _outproj_kernel(ya_ref, yr_ref, g_ref, x_ref, mod_ref, wua_ref, wur_ref, wo_ref, nw2_ref, *refs,
                    router, tiles_per_block):
    if router:
        rw_ref, tri_ref, xo_ref, h2a_ref, h2b_ref, route_ref, cnt_ref, carry_scr = refs
    else:
        xo_ref, h2_ref = refs
    d = x_ref.shape[1]
    ua = jnp.dot(ya_ref[...], wua_ref[...], preferred_element_type=F32)
    ur = jnp.dot(yr_ref[...], wur_ref[...], preferred_element_type=F32)
    u = g_ref[:, :d].astype(F32) * ua + g_ref[:, d:].astype(F32) * ur
    y = jnp.dot(u.astype(BF16), wo_ref[...], preferred_element_type=F32)
    xn = x_ref[...] + mod_ref[2:3, :] * y
    xo_ref[...] = xn
    h2 = (_rms(xn) * nw2_ref[...]) * (1.0 + mod_ref[4:5, :]) + mod_ref[3:4, :]
    h2_hi = h2.astype(BF16)
    if not router:
        h2_ref[...] = h2_hi
    if router:
        h2a_ref[...], h2b_ref[...] = _pack_row(h2)
        h2_lo = (h2 - h2_hi.astype(F32)).astype(BF16)
        logits = (jnp.dot(h2_hi, rw_ref[0], preferred_element_type=F32)
                  + jnp.dot(h2_lo, rw_ref[0], preferred_element_type=F32)
                  + jnp.dot(h2_hi, rw_ref[1], preferred_element_type=F32))
        lane = lax.broadcasted_iota(jnp.int32, logits.shape, 1)
        gates = _top2_gates(jnp.where(lane < N_EXPERTS, logits, -jnp.inf))

        @pl.when(pl.program_id(0) % tiles_per_block == 0)
        def _():
            carry_scr[...] = jnp.zeros_like(carry_scr)

        sel = gates > 0.0
        sel_f = jnp.where(sel, 1.0, 0.0)
        rank = jnp.dot(tri_ref[...], sel_f.astype(BF16), preferred_element_type=F32) + carry_scr[...]
        total = carry_scr[...] + jnp.sum(sel_f, axis=0, keepdims=True)
        carry_scr[...] = total
        cnt_ref[...] = total

        lanef = lane.astype(F32)
        e_a = jnp.min(jnp.where(sel, lanef, float(LANES)), axis=-1, keepdims=True)
        e_b = jnp.max(jnp.where(sel, lanef, -1.0), axis=-1, keepdims=True)
        is_a = lanef == e_a
        is_b = lanef == e_b

        def pick(mask, v):
            return jnp.sum(jnp.where(mask, v, 0.0), axis=-1, keepdims=True)

        fields = (pick(is_a, rank), pick(is_b, rank), e_a, e_b, pick(is_a, gates),
                  jnp.where(e_b == e_a, 0.0, pick(is_b, gates)))
        route = jnp.zeros_like(gates)
        for k, v in enumerate(fields):
            route = jnp.where(lane == k, v, route)
        route_ref[...] = route


def _outproj_call(ya, yr, g, x, mod, wua, wur, wo, nw2, rw, *, tm, rows_per_mod, route_block=None):
    n, d = x.shape
    assert n % tm == 0, (n, tm)
    router = rw is not None
    row = lambda i: (i, 0)
    const = lambda i: (0, 0)
    tiles_per_block = route_block // tm if router else 1
    if rows_per_mod is None:
        mod_map = lambda i: (mod.shape[0] - 1, 0, 0)
    else:
        mod_map = lambda i: (i // (rows_per_mod // tm), 0, 0)
    in_specs = [
        pl.BlockSpec((tm, ya.shape[1]), row),
        pl.BlockSpec((tm, yr.shape[1]), row),
        pl.BlockSpec((tm, 2 * d), row),
        pl.BlockSpec((tm, d), row),
        pl.BlockSpec((None, 6, d), mod_map),
        pl.BlockSpec(wua.shape, const),
        pl.BlockSpec(wur.shape, const),
        pl.BlockSpec(wo.shape, const),
        pl.BlockSpec((1, d), const),
    ]
    args = [ya, yr, g, x, mod, wua, wur, wo, nw2]
    out_specs = [pl.BlockSpec((tm, d), row)]
    out_shape = [jax.ShapeDtypeStruct((n, d), F32)]
    scratch = []
    if router:
        tri = jnp.asarray(np.tril(np.ones((tm, tm), np.float32), -1), BF16)
        in_specs += [pl.BlockSpec(rw.shape, lambda i: (0, 0, 0)), pl.BlockSpec(tri.shape, const)]
        args += [rw, tri]
        out_specs += [pl.BlockSpec((tm, d // 4), row), pl.BlockSpec((tm, d // 4), row),
                      pl.BlockSpec((tm, LANES), row),
                      pl.BlockSpec((None, 1, LANES), lambda i: (i // tiles_per_block, 0, 0))]
        out_shape += [jax.ShapeDtypeStruct((n, d // 4), jnp.uint32), jax.ShapeDtypeStruct((n, d // 4), jnp.uint32),
                      jax.ShapeDtypeStruct((n, LANES), F32),
                      jax.ShapeDtypeStruct((n // route_block, 1, LANES), F32)]
        scratch = [pltpu.VMEM((1, LANES), F32)]
    else:
        out_specs.append(pl.BlockSpec((tm, d), row))
        out_shape.append(jax.ShapeDtypeStruct((n, d), BF16))
    return pl.pallas_call(
        functools.partial(_outproj_kernel, router=router, tiles_per_block=tiles_per_block),
        grid=(n // tm,),
        in_specs=in_specs,
        out_specs=out_specs,
        out_shape=out_shape,
        scratch_shapes=scratch,
        compiler_params=_cparams("arbitrary"),
        name="outproj_router" if router else "outproj",
    )(*args)


def _swiglu_chunk(xs, w1_ref, w3_ref, w2_ref):
    a = jnp.dot(xs, w1_ref[...].astype(BF16), preferred_element_type=F32)
    b = jnp.dot(xs, w3_ref[...].astype(BF16), preferred_element_type=F32)
    return jnp.dot((_silu(a) * b).astype(BF16), w2_ref[...].astype(BF16), preferred_element_type=F32)


def _ffn_kernel(h_ref, x_ref, mod_ref, w1_ref, w3_ref, w2_ref, o_ref, *, tf):
    h = h_ref[...]
    ff = w1_ref.shape[1]
    acc = jnp.zeros(o_ref.shape, F32)
    for lo in range(0, ff, tf):
        hi = min(lo + tf, ff)
        a = jnp.dot(h, w1_ref[:, lo:hi], preferred_element_type=F32)
        b = jnp.dot(h, w3_ref[:, lo:hi], preferred_element_type=F32)
        acc = acc + jnp.dot((_silu(a) * b).astype(BF16), w2_ref[lo:hi, :], preferred_element_type=F32)
    o_ref[...] = x_ref[...] + mod_ref[5:6, :] * acc


def _ffn_call(h, x, mod, w1, w3, w2, *, tm, tf, rows_per_mod):
    n, d = x.shape
    assert n % tm == 0, (n, tm)
    row = lambda i: (i, 0)
    const = lambda i: (0, 0)
    if rows_per_mod is None:
        mod_map = lambda i: (mod.shape[0] - 1, 0, 0)
    else:
        mod_map = lambda i: (i // (rows_per_mod // tm), 0, 0)
    return pl.pallas_call(
        functools.partial(_ffn_kernel, tf=tf),
        grid=(n // tm,),
        in_specs=[
            pl.BlockSpec((tm, d), row), pl.BlockSpec((tm, d), row), pl.BlockSpec((None, 6, d), mod_map),
            pl.BlockSpec(w1.shape, const), pl.BlockSpec(w3.shape, const), pl.BlockSpec(w2.shape, const),
        ],
        out_specs=pl.BlockSpec((tm, d), row),
        out_shape=jax.ShapeDtypeStruct((n, d), F32),
        compiler_params=_cparams("arbitrary"),
        name="ffn",
    )(h, x, mod, w1, w3, w2)


def _pack_halves(x):
    n = x.shape[1] // 2
    lo = lax.bitcast_convert_type(x[:, :n].astype(BF16).astype(F32), jnp.uint32)
    hi = lax.bitcast_convert_type(x[:, n:].astype(BF16).astype(F32), jnp.uint32)
    return (hi & jnp.uint32(0xFFFF0000)) | (lo >> 16)


def _unpack_halves(w):
    lo = lax.bitcast_convert_type(w << 16, F32)
    hi = lax.bitcast_convert_type(w & jnp.uint32(0xFFFF0000), F32)
    return jnp.concatenate([lo, hi], axis=1)


def _sc_gather_rows(tables, idxs, *, window):
    n_jobs = len(tables)
    mesh = plsc.VectorSubcoreMesh(core_axis_name="c", subcore_axis_name="s")
    out_type = [jax.ShapeDtypeStruct((idx.shape[0], t.shape[1]), t.dtype) for t, idx in zip(tables, idxs)]
    for idx in idxs:
        assert idx.shape[0] % window == 0, (idx.shape, window)

    @pl.kernel(out_type=out_type, mesh=mesh)
    def gather(*refs):
        for j in range(n_jobs):
            table_hbm, idx_hbm, out_hbm = refs[j], refs[n_jobs + j], refs[2 * n_jobs + j]

            def body(idx_vmem, out_vmem, table_hbm=table_hbm):
                pltpu.sync_copy(table_hbm.at[idx_vmem.at[0]], out_vmem)

            pltpu.emit_pipeline(
                body,
                grid=(idx_hbm.shape[1] // window,),
                in_specs=[pl.BlockSpec((1, window), lambda i: (0, i))],
                out_specs=[pl.BlockSpec((window, table_hbm.shape[1]), lambda i: (i, 0))],
                core_axis_name=("c", "s"),
                dimension_semantics=(pltpu.PARALLEL,),
            )(idx_hbm, out_hbm)

    return gather(*tables, *[idx.reshape(1, -1) for idx in idxs])


def _packed_row(pair):
    return jnp.concatenate([_unpack_halves(pair[0][...]), _unpack_halves(pair[1][...])], axis=1)


def _pack_row(x):
    half = x.shape[1] // 2
    return _pack_halves(x[:, :half]), _pack_halves(x[:, half:])


def _experts_kernel(tile_expert_ref, n_active_ref, xa_ref, xb_ref, w1_ref, w3_ref, w2_ref, oa_ref, ob_ref,
                    x_scr, acc_ref):
    i, f = pl.program_id(0), pl.program_id(1)

    @pl.when(i < n_active_ref[0])
    def _():
        @pl.when(f == 0)
        def _():
            x_scr[...] = _packed_row((xa_ref, xb_ref)).astype(BF16)
            acc_ref[...] = jnp.zeros_like(acc_ref)

        acc_ref[...] += _swiglu_chunk(x_scr[...], w1_ref, w3_ref, w2_ref)

        @pl.when(f == pl.num_programs(1) - 1)
        def _():
            oa_ref[...], ob_ref[...] = _pack_row(acc_ref[...])


def _experts_call(tile_expert, n_active, xa, xb, w1, w3, w2, *, tm, tf):
    n, dp = xa.shape
    assert n % tm == 0, (n, tm)
    _, d, ff = w1.shape
    rows = pl.BlockSpec((tm, dp), lambda i, f, te, na: (i, 0))
    grid_spec = pltpu.PrefetchScalarGridSpec(
        num_scalar_prefetch=2,
        grid=(n // tm, ff // tf),
        in_specs=[
            rows, rows,
            pl.BlockSpec((None, d, tf), lambda i, f, te, na: (te[i], 0, f)),
            pl.BlockSpec((None, d, tf), lambda i, f, te, na: (te[i], 0, f)),
            pl.BlockSpec((None, tf, d), lambda i, f, te, na: (te[i], f, 0)),
        ],
        out_specs=[rows, rows],
        scratch_shapes=[pltpu.VMEM((tm, d), BF16), pltpu.VMEM((tm, d), F32)],
    )
    return pl.pallas_call(
        _experts_kernel,
        grid_spec=grid_spec,
        out_shape=[jax.ShapeDtypeStruct((n, dp), jnp.uint32)] * 2,
        compiler_params=_cparams("arbitrary", "arbitrary"),
        name="experts",
    )(tile_expert, n_active, xa, xb, w1, w3, w2)


def _combine_kernel(x_ref, yaa_ref, yab_ref, yba_ref, ybb_ref, route_ref, mod_ref, fnw_ref, o_ref):
    y = (route_ref[:, 4:5] * _packed_row((yaa_ref, yab_ref))
         + route_ref[:, 5:6] * _packed_row((yba_ref, ybb_ref)))
    o_ref[...] = _rms(x_ref[...] + mod_ref[5:6, :] * y) * fnw_ref[...]


def _combine_call(x, ys, route, mod, fnw, *, tm, rows_per_mod):
    n, d = x.shape
    assert n % tm == 0, (n, tm)
    row = lambda i: (i, 0)
    return pl.pallas_call(
        _combine_kernel,
        grid=(n // tm,),
        in_specs=[pl.BlockSpec((tm, d), row)] + [pl.BlockSpec((tm, d // 4), row)] * 4 + [
            pl.BlockSpec((tm, LANES), row),
            pl.BlockSpec((None, 6, d), lambda i: (i // (rows_per_mod // tm), 0, 0)),
            pl.BlockSpec((1, d), lambda i: (0, 0))],
        out_specs=pl.BlockSpec((tm, d), row),
        out_shape=jax.ShapeDtypeStruct((n, d), F32),
        compiler_params=_cparams("arbitrary"),
        name="combine",
    )(x, *ys, route, mod, fnw)


def _dispatch_plan(route, counts, *, tm):
    n = route.shape[0]
    n_e = counts.shape[0]
    n_slots = TOP_K * n + n_e * tm
    seg = (counts + tm - 1) // tm * tm
    ends = jnp.cumsum(seg)
    offs = ends - seg
    rec = route[:, :4].astype(jnp.int32)
    slot_a = jnp.take(offs, rec[:, 2]) + rec[:, 0]
    slot_b = jnp.take(offs, rec[:, 3]) + rec[:, 1]
    n_tiles = n_slots // tm
    n_active = (ends[-1] // tm).astype(jnp.int32)
    tile_start = jnp.arange(n_tiles, dtype=jnp.int32) * tm
    tile_start = jnp.minimum(tile_start, jnp.maximum(ends[-1] - tm, 0))
    tile_expert = jnp.minimum(jnp.sum(tile_start[:, None] >= ends[None, :], axis=1).astype(jnp.int32), n_e - 1)

    tok = jnp.arange(n, dtype=jnp.int32)
    key_b = jnp.where(rec[:, 3] == rec[:, 2], n_slots + tok, slot_b)
    _, by_slot = lax.sort_key_val(jnp.concatenate([slot_a, key_b]), jnp.concatenate([tok, tok]))
    e_of_slot = jnp.repeat(tile_expert, tm)
    packed_before = jnp.cumsum(counts) - counts
    src = jnp.arange(n_slots, dtype=jnp.int32) - jnp.take(offs, e_of_slot) + jnp.take(packed_before, e_of_slot)
    token_of_slot = jnp.take(by_slot, jnp.clip(src, 0, TOP_K * n - 1))
    return token_of_slot, slot_a, slot_b, tile_expert, n_active.reshape(1)


def _rope_tables(t_lat):
    rows = t_lat // GRID_W
    row = jnp.repeat(jnp.arange(rows, dtype=F32), GRID_W)
    col = jnp.tile(jnp.arange(GRID_W, dtype=F32), rows)
    n_freq = ATT_QK_DIM // 4
    inv_freq = ROPE_THETA ** (-jnp.arange(n_freq, dtype=F32) / n_freq)
    ang = jnp.concatenate([row[:, None] * inv_freq, col[:, None] * inv_freq], axis=-1)
    cos, sin = jnp.cos(ang), jnp.sin(ang)
    return (jnp.concatenate([cos, cos, cos, cos], axis=-1), jnp.concatenate([-sin, -sin, sin, sin], axis=-1))


def _layer_lower_bounds(lb_param):
    cs = jnp.cumsum(jax.nn.softmax(lb_param.astype(F32), axis=0), axis=0)
    return cs - cs[0:1]


def _win_columns(w_in_l):
    c = [w_in_l[:, i * 512:(i + 1) * 512] for i in range(8)]
    half = ATT_QK_DIM // 2

    def pair_halves(w):
        w = w.reshape(w.shape[0], ATT_HEADS, 2, 2, half)
        return w.transpose(0, 1, 3, 2, 4).reshape(w.shape[0], -1)

    return jnp.concatenate([pair_halves(c[0]), pair_halves(c[1]), c[2], c[7], c[3], c[4], c[5], c[6],
                            w_in_l[:, 4096:]], axis=1).astype(BF16)


def _pad_ff(w, axis, mult):
    ff = w.shape[axis]
    pad = (-ff) % mult
    if pad == 0:
        return w
    widths = [(0, 0)] * w.ndim
    widths[axis] = (0, pad)
    return jnp.pad(w, widths)


def kernel(x, c, ctx, c_ctx, w_ada, b_ada, norm_mix_w, norm_ffn_w, w_in, lambda_q1, lambda_k1, lambda_q2,
           lambda_k2, att_norm_w, rec_norm_w, lb_fwd, lb_bwd, w_up_att, w_up_rec, w_out, ffn_w1, ffn_w3,
           ffn_w2, router_w, moe_w1, moe_w3, moe_w2, final_norm_w):
    batch, t_lat, d = x.shape
    t_ctx = ctx.shape[1]
    depth = w_ada.shape[0]
    n_lat, n_ctx = batch * t_lat, batch * t_ctx
    tm = 512
    tm_in = 512
    tq = min(1024, t_lat)

    xl = x.reshape(n_lat, d)
    xc = ctx.reshape(n_ctx, d)

    pad_rows = (-(batch + 1)) % 8
    cc = jnp.concatenate([c, jnp.zeros((pad_rows, d), F32), c_ctx[None, :]], axis=0)
    mod_all = _mod_call(cc, w_ada, b_ada).reshape(depth, cc.shape[0], 6, d)

    cos, sin = _rope_tables(t_lat)
    lbs_f = _layer_lower_bounds(lb_fwd)
    lbs_b = _layer_lower_bounds(lb_bwd)

    for l in range(depth):
        last = l == depth - 1
        mod = mod_all[l]
        lam_init = 0.8 - 0.6 * math.exp(-0.3 * l)
        lam = (jnp.exp(jnp.sum(lambda_q1[l] * lambda_k1[l])) - jnp.exp(jnp.sum(lambda_q2[l] * lambda_k2[l]))
               + lam_init).reshape(1).astype(F32)
        lb = jnp.concatenate([lbs_f[l], lbs_b[l]])[None, :]
        w_l = _win_columns(w_in[l])
        nw = norm_mix_w[l][None, :]

        pl_lat = _inproj_call(xl, mod, nw, w_l, cos, sin, lb, tm=tm_in, rows_per_mod=t_lat, rope=True)
        pl_ctx = _inproj_call(xc, mod, nw, w_l, cos, sin, lb, tm=min(tm_in, n_ctx), rows_per_mod=None,
                              rope=False)
        qk_l, vg_l, qi_l, lf_l, g_l = pl_lat
        qk_c, vg_c, qi_c, lf_c, g_c = pl_ctx

        anw = att_norm_w[l][None, :]
        ya_l = _attn_call(lam, qk_l, qk_l, vg_l, qk_c, vg_c, anw, batch=batch, tq=tq,
                          lam_init=lam_init, with_lat=True)
        yr_l, yr_c = _gla_call(qi_l, lf_l, vg_l, qi_c, lf_c, vg_c, rec_norm_w[l][None, :], batch=batch)

        wua = w_up_att[l].astype(BF16)
        wur = w_up_rec[l].astype(BF16)
        wo = w_out[l].astype(BF16)
        nw2 = norm_ffn_w[l][None, :]
        moe_layer = l % 2 == 1
        j = l // 2
        assert moe_layer == last
        rw = None
        if moe_layer:
            rw32 = jnp.pad(router_w[j], ((0, 0), (0, LANES - N_EXPERTS)))
            rw_hi = rw32.astype(BF16)
            rw = jnp.stack([rw_hi, (rw32 - rw_hi.astype(F32)).astype(BF16)])
        res = _outproj_call(ya_l, yr_l, g_l, xl, mod, wua, wur, wo, nw2, rw, tm=tm, rows_per_mod=t_lat,
                            route_block=n_lat)
        xl = res[0]
        if not last:
            ya_c = _attn_call(lam, qk_c, None, None, qk_c, vg_c, anw, batch=batch, tq=t_ctx,
                              lam_init=lam_init, with_lat=False)
            xc, h2_c = _outproj_call(ya_c, yr_c, g_c, xc, mod, wua, wur, wo, nw2, None, tm=tm, rows_per_mod=None)

        if moe_layer:
            h2a, h2b, route, cnt = res[1:]
            counts = cnt[0, 0, :N_EXPERTS].astype(jnp.int32)
            mt = min(MOE_TILE, n_lat)
            token_of_slot, slot_a, slot_b, tile_expert, n_active = _dispatch_plan(route, counts, tm=mt)
            xa, xb = _sc_gather_rows([h2a, h2b], [token_of_slot] * 2, window=SC_WINDOW)
            oa, ob = _experts_call(tile_expert, n_active, xa, xb, moe_w1[j], moe_w3[j], moe_w2[j], tm=mt, tf=512)
            ys = _sc_gather_rows([oa, ob, oa, ob], [slot_a, slot_a, slot_b, slot_b], window=SC_WINDOW)
            xl = _combine_call(xl, ys, route, mod, final_norm_w[None, :], tm=tm, rows_per_mod=t_lat)
        else:
            h2_l = res[1]
            w1 = _pad_ff(ffn_w1[j], 1, 2 * LANES).astype(BF16)
            w3 = _pad_ff(ffn_w3[j], 1, 2 * LANES).astype(BF16)
            w2 = _pad_ff(ffn_w2[j], 0, 2 * LANES).astype(BF16)
            tf = 512
            xl = _ffn_call(h2_l, xl, mod, w1, w3, w2, tm=tm, tf=tf, rows_per_mod=t_lat)
            xc = _ffn_call(h2_c, xc, mod, w1, w3, w2, tm=tm, tf=tf, rows_per_mod=None)

    return xl.reshape(batch, t_lat, d)
```

```python
import functools
import math

import numpy as np
import jax
import jax.numpy as jnp
from jax import lax
from jax.experimental import pallas as pl
from jax.experimental.pallas import tpu as pltpu
from jax.experimental.pallas import tpu_sc as plsc

F32 = jnp.float32
BF16 = jnp.bfloat16
HIGHEST = lax.Precision.HIGHEST

EPS = 1e-6
GRID_W = 64
ROPE_THETA = 10000.0
ATT_HEADS = 4
ATT_QK_DIM = 64
REC_HEADS = 4
N_EXPERTS = 8
TOP_K = 2
Q_SCALE = ATT_QK_DIM ** -0.5 * math.log2(math.e)

LANES = 128
GLA_BLOCK = 256
GLA_CHUNK = 256
GLA_UNROLL = 4
GLA_LEVELS = tuple(2 ** i for i in range(GLA_CHUNK.bit_length() - 1))
VMEM_LIMIT = 56 * 1024 * 1024
MOE_TILE = 1024
EXPERT_ROWS = 512
SC_WINDOW = 128
ATT_KEYS = 256
ATT_ROWS = 512

NT_DIMS = (((1,), (1,)), ((), ()))
TN_DIMS = (((0,), (0,)), ((), ()))


def _cparams(*sem):
    return pltpu.CompilerParams(dimension_semantics=sem, vmem_limit_bytes=VMEM_LIMIT)


def _silu(a):
    return a * jax.nn.sigmoid(a)


def _rms(x):
    return x * lax.rsqrt(jnp.mean(x * x, axis=-1, keepdims=True) + EPS)


def _mod_kernel(c_ref, w_ref, b_ref, o_ref):
    s = _silu(c_ref[...])
    o_ref[...] = jnp.dot(s, w_ref[...], precision=HIGHEST, preferred_element_type=F32) + b_ref[...]


def _mod_call(cc, w_ada, b_ada):
    depth, d, n = w_ada.shape
    rows = cc.shape[0]
    tn = 1536
    return pl.pallas_call(
        _mod_kernel,
        grid=(depth, n // tn),
        in_specs=[
            pl.BlockSpec((rows, d), lambda l, j: (0, 0)),
            pl.BlockSpec((None, d, tn), lambda l, j: (l, 0, j)),
            pl.BlockSpec((None, 1, tn), lambda l, j: (l, 0, j)),
        ],
        out_specs=pl.BlockSpec((None, rows, tn), lambda l, j: (l, 0, j)),
        out_shape=jax.ShapeDtypeStruct((depth, rows, n), F32),
        compiler_params=_cparams("arbitrary", "arbitrary"),
        name="mod",
    )(cc, w_ada, b_ada.reshape(depth, 1, n))


def _log_forget(z, lb):
    t = jnp.exp(-jnp.abs(z))
    num = jnp.where(z >= 0.0, 1.0 + lb * t, lb + t)
    return jnp.where(num > 0.0, jnp.log(num / (1.0 + t)), z)


def _inproj_kernel(x_ref, mod_ref, nw_ref, w_ref, cos_ref, sin_ref, lb_ref,
                   qk_ref, vg_ref, qi_ref, lf_ref, g_ref, *, rope):
    h = _rms(x_ref[...]) * nw_ref[...]
    h = (h * (1.0 + mod_ref[1:2, :]) + mod_ref[0:1, :]).astype(BF16)
    ts = qk_ref.shape[1]
    half = ts // 2

    def proj(s):
        return jnp.dot(h, w_ref[:, s * ts:(s + 1) * ts], preferred_element_type=F32)

    p = proj(0)
    if rope:
        cos = cos_ref[...]
        sin = sin_ref[...]
    for j in range(ts // LANES):
        blk = p[:, j * LANES:(j + 1) * LANES]
        if rope:
            blk = blk * cos + pltpu.roll(blk, LANES // 2, 1) * sin
        if j * LANES < half:
            blk = blk * Q_SCALE
        qk_ref[:, j * LANES:(j + 1) * LANES] = blk.astype(BF16)

    p = proj(1)
    vg_ref[:, :half] = p[:, :half].astype(BF16)
    vg_ref[:, half:] = _silu(p[:, half:]).astype(BF16)

    p = proj(2)
    qi_ref[:, :half] = _silu(p[:, :half]).astype(BF16)
    qi_ref[:, half:] = p[:, half:].astype(BF16)

    lf_ref[...] = _log_forget(proj(3), lb_ref[...])

    g_ref[:, :ts] = jax.nn.sigmoid(proj(4)).astype(BF16)
    g_ref[:, ts:] = jax.nn.sigmoid(proj(5)).astype(BF16)


def _inproj_call(x, mod, nw, w, cos, sin, lb, *, tm, rows_per_mod, rope):
    n, d = x.shape
    assert n % tm == 0, (n, tm)
    ts = 1024
    assert w.shape[1] == 6 * ts
    pos_tiles = cos.shape[0] // tm
    row = lambda i: (i, 0)
    const = lambda i: (0, 0)
    if rows_per_mod is None:
        mod_map = lambda i: (mod.shape[0] - 1, 0, 0)
    else:
        mod_map = lambda i: (i // (rows_per_mod // tm), 0, 0)
    outs = pl.pallas_call(
        functools.partial(_inproj_kernel, rope=rope),
        grid=(n // tm,),
        in_specs=[
            pl.BlockSpec((tm, d), row),
            pl.BlockSpec((None, 6, d), mod_map),
            pl.BlockSpec((1, d), const),
            pl.BlockSpec(w.shape, const),
            pl.BlockSpec((tm, LANES), lambda i: (i % pos_tiles, 0)),
            pl.BlockSpec((tm, LANES), lambda i: (i % pos_tiles, 0)),
            pl.BlockSpec((1, ts), const),
        ],
        out_specs=[
            pl.BlockSpec((tm, ts), row),
            pl.BlockSpec((tm, ts), row),
            pl.BlockSpec((tm, ts), row),
            pl.BlockSpec((tm, ts), row),
            pl.BlockSpec((tm, 2 * ts), row),
        ],
        out_shape=[
            jax.ShapeDtypeStruct((n, ts), BF16),
            jax.ShapeDtypeStruct((n, ts), BF16),
            jax.ShapeDtypeStruct((n, ts), BF16),
            jax.ShapeDtypeStruct((n, ts), F32),
            jax.ShapeDtypeStruct((n, 2 * ts), BF16),
        ],
        compiler_params=_cparams("arbitrary"),
        name="inproj_rope" if rope else "inproj",
    )(x, mod, nw, w, cos, sin, lb)
    return outs


def _attn_kernel(lam_ref, q_ref, *refs, post_scale, with_lat):
    if with_lat:
        kl_ref, vl_ref, kc_ref, vc_ref, nw_ref, o_ref, vce_scr, vle_scr = refs
    else:
        kc_ref, vc_ref, nw_ref, o_ref, vce_scr = refs

    @pl.when(pl.program_id(2) == 0)
    def _():
        vce_scr[:, :LANES] = vc_ref[...]
        vce_scr[:, LANES:] = jnp.ones(vc_ref.shape, BF16)
        if with_lat:
            vle_scr[:, :LANES] = vl_ref[...]
            vle_scr[:, LANES:] = jnp.ones(vl_ref.shape, BF16)

    n_parts = max(1, q_ref.shape[0] // ATT_ROWS)
    rows = q_ref.shape[0] // n_parts
    qq = []
    for part in range(n_parts):
        q = q_ref[part * rows:(part + 1) * rows, :]
        lane = lax.broadcasted_iota(jnp.int32, q.shape, 1)
        zero = jnp.zeros_like(q)
        sub1 = (lane % ATT_QK_DIM) < (ATT_QK_DIM // 2)
        qq.append(jnp.concatenate([jnp.where(sub1, q, zero), jnp.where(sub1, zero, q)], axis=0))

    blocks = [(kc_ref, vce_scr, 0, kc_ref.shape[0])]
    if with_lat:
        blocks += [(kl_ref, vle_scr, j, ATT_KEYS) for j in range(0, kl_ref.shape[0], ATT_KEYS)]
    m = [jnp.full((2 * rows, 1), -jnp.inf, F32)] * n_parts
    acc = [jnp.zeros((2 * rows, 2 * LANES), F32)] * n_parts
    for k_ref, v_scr, start, size in blocks:
        for part in range(n_parts):
            s = lax.dot_general(qq[part], k_ref[start:start + size, :], NT_DIMS, preferred_element_type=F32)
            m_new = jnp.maximum(m[part], jnp.max(s, axis=-1, keepdims=True))
            p = jnp.exp2(s - m_new).astype(BF16)
            acc[part] = acc[part] * jnp.exp2(m[part] - m_new) + jnp.dot(
                p, v_scr[start:start + size, :], preferred_element_type=F32)
            m[part] = m_new
    for part in range(n_parts):
        on = acc[part][:, :LANES] * (1.0 / acc[part][:, LANES:])
        o = on[:rows] - lam_ref[0] * on[rows:]
        o_ref[part * rows:(part + 1) * rows, :] = (_rms(o) * nw_ref[...] * post_scale).astype(BF16)


def _attn_call(lam, q_src, qk_lat, vg_lat, qk_ctx, vg_ctx, nw, *, batch, tq, lam_init, with_lat):
    n_q = q_src.shape[0]
    t_q = n_q // batch
    t_lat = qk_lat.shape[0] // batch if with_lat else 0
    t_ctx = qk_ctx.shape[0] // batch
    nq_tiles = t_q // tq
    h = ATT_HEADS
    in_specs = [
        pl.BlockSpec(memory_space=pltpu.SMEM),
        pl.BlockSpec((tq, LANES), lambda b, hh, i: (b * nq_tiles + i, hh)),
    ]
    args = [lam, q_src]
    if with_lat:
        in_specs += [
            pl.BlockSpec((t_lat, LANES), lambda b, hh, i: (b, h + hh)),
            pl.BlockSpec((t_lat, LANES), lambda b, hh, i: (b, hh)),
        ]
        args += [qk_lat, vg_lat]
    in_specs += [
        pl.BlockSpec((t_ctx, LANES), lambda b, hh, i: (b, h + hh)),
        pl.BlockSpec((t_ctx, LANES), lambda b, hh, i: (b, hh)),
        pl.BlockSpec((1, LANES), lambda b, hh, i: (0, 0)),
    ]
    args += [qk_ctx, vg_ctx, nw]
    return pl.pallas_call(
        functools.partial(_attn_kernel, post_scale=1.0 - lam_init, with_lat=with_lat),
        grid=(batch, h, nq_tiles),
        in_specs=in_specs,
        out_specs=pl.BlockSpec((tq, LANES), lambda b, hh, i: (b * nq_tiles + i, hh)),
        out_shape=jax.ShapeDtypeStruct((n_q, h * LANES), BF16),
        scratch_shapes=[pltpu.VMEM((t_ctx, 2 * LANES), BF16)]
        + ([pltpu.VMEM((t_lat, 2 * LANES), BF16)] if with_lat else []),
        compiler_params=_cparams("arbitrary", "arbitrary", "arbitrary"),
        name="attn_lat" if with_lat else "attn_ctx",
    )(*args)


def _gla_constants():
    n = GLA_BLOCK
    idx = np.arange(n)
    same_chunk = (idx[:, None] // GLA_CHUNK) == (idx[None, :] // GLA_CHUNK)
    stacks, masks = [], []
    for reverse in (False, True):
        order = (idx[None, :] >= idx[:, None]) if reverse else (idx[None, :] <= idx[:, None])
        cum = (same_chunk & order).astype(np.int32)
        groups, lvl_masks = [cum], []
        for m in GLA_LEVELS:
            ref = (idx // (2 * m)) * (2 * m) + (m if reverse else m - 1)
            groups.append(np.abs(cum - cum[ref]))
            upper = (idx % (2 * m)) >= m
            same = (idx[:, None] // (2 * m)) == (idx[None, :] // (2 * m))
            q_side, k_side = (~upper, upper) if reverse else (upper, ~upper)
            lvl_masks.append(same & q_side[:, None] & k_side[None, :])
        end = (idx // GLA_CHUNK) * GLA_CHUNK + (0 if reverse else GLA_CHUNK - 1)
        groups.append(np.abs(cum[end] - cum))
        stacks.append(np.concatenate(groups, axis=0))
        masks.append(np.stack(lvl_masks))
    return jnp.asarray(np.stack(stacks), BF16), jnp.asarray(np.stack(masks), F32)


def _gla_prepare(q, v, lf, stack, *, reverse):
    n = q.shape[0]
    qf = q.astype(F32)
    k = 1.0 - jnp.exp(lf)

    hi = lf.astype(BF16)
    mid = (lf - hi.astype(F32)).astype(BF16)
    expo = jnp.dot(stack, jnp.concatenate([hi, mid], axis=1), preferred_element_type=F32)

    def exponent(group):
        blk = expo[group * n:(group + 1) * n]
        return blk[:, :LANES] + blk[:, LANES:]

    scale = [jnp.exp(exponent(1 + li)) for li in range(len(GLA_LEVELS))]
    qs = jnp.concatenate([(qf * e).astype(BF16) for e in scale], axis=1)
    ks = jnp.concatenate([(k * e).astype(BF16) for e in scale], axis=1)
    c = exponent(0)
    io = jnp.concatenate([(qf * jnp.exp(c)).astype(BF16),
                          (k * jnp.exp(exponent(1 + len(GLA_LEVELS)))).astype(BF16)], axis=1)
    diag = jnp.sum(qf * k, axis=-1, keepdims=True) * v.astype(F32)
    last = 0 if reverse else GLA_CHUNK - 1
    n_chunks = n // GLA_CHUNK
    decay = jnp.concatenate([jnp.exp(c[ci * GLA_CHUNK + last:ci * GLA_CHUNK + last + 1, :])
                             for ci in range(n_chunks)] + [jnp.ones((8 - n_chunks, LANES), F32)], axis=0)
    return qs, ks, io, diag, decay


def _gla_scan(qs, ks, io, diag, decay, v, st, masks, *, reverse):
    n = v.shape[0]
    scores = jnp.zeros((n, n), F32)
    for li in range(len(GLA_LEVELS)):
        cols = slice(li * LANES, (li + 1) * LANES)
        s_l = lax.dot_general(qs[:, cols], ks[:, cols], NT_DIMS, preferred_element_type=F32)
        scores = scores + s_l * masks[li]
    o = jnp.dot(scores.astype(BF16), v, preferred_element_type=F32) + diag

    n_chunks = n // GLA_CHUNK
    outs = [None] * n_chunks
    order = range(n_chunks - 1, -1, -1) if reverse else range(n_chunks)
    for ci in order:
        r = slice(ci * GLA_CHUNK, (ci + 1) * GLA_CHUNK)
        outs[ci] = lax.dot_general(io[r, :LANES], st.astype(BF16), NT_DIMS, preferred_element_type=F32)
        upd = lax.dot_general(v[r], io[r, LANES:], TN_DIMS, preferred_element_type=F32)
        st = st * decay[ci:ci + 1, :] + upd
    return o + jnp.concatenate(outs, axis=0), st


def _gla_kernel(ql_ref, il_ref, ffl_ref, fbl_ref, gl_ref, qc_ref, ic_ref, ffc_ref, fbc_ref, gc_ref,
                nw_ref, stack_ref, mask_ref, ol_ref, oc_ref,
                qs_scr, ks_scr, io_scr, dg_scr, dc_scr, of_scr, ob_scr):
    n_blocks = ql_ref.shape[0] // GLA_BLOCK
    nw = nw_ref[...]

    def finish(o, g):
        return (_rms(o) * nw * g.astype(F32)).astype(BF16)

    def rows(i):
        return pl.ds(pl.multiple_of(i * GLA_BLOCK, GLA_BLOCK), GLA_BLOCK)

    def prepare(slot, q, v, lf_f, lf_b):
        for d, lf in enumerate((lf_f, lf_b)):
            qs, ks, io, diag, decay = _gla_prepare(q, v, lf, stack_ref[d], reverse=bool(d))
            qs_scr[d, slot] = qs
            ks_scr[d, slot] = ks
            io_scr[d, slot] = io
            dg_scr[d, slot] = diag
            dc_scr[d, slot] = decay

    def scan(d, slot, v, st):
        return _gla_scan(qs_scr[d, slot], ks_scr[d, slot], io_scr[d, slot], dg_scr[d, slot], dc_scr[d, slot],
                         v, st, mask_ref[d], reverse=bool(d))

    prepare(0, qc_ref[...], ic_ref[...], ffc_ref[...], fbc_ref[...])

    def prep(i, carry):
        r = rows(i)
        prepare(i + 1, ql_ref[r, :], il_ref[r, :], ffl_ref[r, :], fbl_ref[r, :])
        return carry

    unroll = min(GLA_UNROLL, n_blocks)
    lax.fori_loop(0, n_blocks, prep, 0, unroll=unroll)

    zero_state = jnp.zeros((LANES, LANES), F32)
    o_cf, st_f = scan(0, 0, ic_ref[...], zero_state)
    o_cb, st_b = scan(1, 0, ic_ref[...], zero_state)
    oc_ref[...] = finish(o_cf + o_cb, gc_ref[...])

    def step(i, carry):
        st_f, st_b = carry
        j = n_blocks - 1 - i
        o_f, st_f = scan(0, i + 1, il_ref[rows(i), :], st_f)
        o_b, st_b = scan(1, j + 1, il_ref[rows(j), :], st_b)
        of_scr[rows(i), :] = o_f
        ob_scr[rows(j), :] = o_b
        return st_f, st_b

    lax.fori_loop(0, n_blocks, step, (st_f, st_b), unroll=min(2 * GLA_UNROLL, n_blocks))

    def emit(i, carry):
        r = rows(i)
        ol_ref[r, :] = finish(of_scr[r, :] + ob_scr[r, :], gl_ref[r, :])
        return carry

    lax.fori_loop(0, n_blocks, emit, 0, unroll=unroll)


def _gla_call(qi_lat, lf_lat, vg_lat, qi_ctx, lf_ctx, vg_ctx, nw, *, batch):
    t_lat = qi_lat.shape[0] // batch
    t_ctx = qi_ctx.shape[0] // batch
    assert t_ctx == GLA_BLOCK and t_lat % GLA_BLOCK == 0
    h = REC_HEADS
    stack, masks = _gla_constants()
    n_slots = 1 + t_lat // GLA_BLOCK
    n_lvl = len(GLA_LEVELS)

    def col(off):
        return lambda b, hh: (b, off + hh)

    def seq(t):
        return [pl.BlockSpec((t, LANES), col(0)), pl.BlockSpec((t, LANES), col(h)),
                pl.BlockSpec((t, LANES), col(0)), pl.BlockSpec((t, LANES), col(h)),
                pl.BlockSpec((t, LANES), col(h))]

    return pl.pallas_call(
        _gla_kernel,
        grid=(batch, h),
        in_specs=seq(t_lat) + seq(t_ctx) + [
            pl.BlockSpec((1, LANES), lambda b, hh: (0, 0)),
            pl.BlockSpec(stack.shape, lambda b, hh: (0, 0, 0)),
            pl.BlockSpec(masks.shape, lambda b, hh: (0, 0, 0, 0)),
        ],
        out_specs=[pl.BlockSpec((t_lat, LANES), col(0)), pl.BlockSpec((t_ctx, LANES), col(0))],
        out_shape=[jax.ShapeDtypeStruct((batch * t_lat, h * LANES), BF16),
                   jax.ShapeDtypeStruct((batch * t_ctx, h * LANES), BF16)],
        scratch_shapes=[
            pltpu.VMEM((2, n_slots, GLA_BLOCK, n_lvl * LANES), BF16),
            pltpu.VMEM((2, n_slots, GLA_BLOCK, n_lvl * LANES), BF16),
            pltpu.VMEM((2, n_slots, GLA_BLOCK, 2 * LANES), BF16),
            pltpu.VMEM((2, n_slots, GLA_BLOCK, LANES), F32),
            pltpu.VMEM((2, n_slots, 8, LANES), F32),
            pltpu.VMEM((t_lat, LANES), F32), pltpu.VMEM((t_lat, LANES), F32),
        ],
        compiler_params=_cparams("arbitrary", "arbitrary"),
        name="gla",
    )(qi_lat, qi_lat, lf_lat, lf_lat, vg_lat, qi_ctx, qi_ctx, lf_ctx, lf_ctx, vg_ctx, nw, stack, masks)


def _top2_gates(logits):
    lane = lax.broadcasted_iota(jnp.int32, logits.shape, 1).astype(F32)
    big = float(LANES)
    m1 = jnp.max(logits, axis=-1, keepdims=True)
    i1 = jnp.min(jnp.where(logits == m1, lane, big), axis=-1, keepdims=True)
    rest = jnp.where(lane == i1, -jnp.inf, logits)
    m2 = jnp.max(rest, axis=-1, keepdims=True)
    i2 = jnp.min(jnp.where(rest == m2, lane, big), axis=-1, keepdims=True)
    e = jnp.exp(m2 - m1)
    w1 = 1.0 / (1.0 + e)
    return jnp.where(lane == i1, w1, 0.0) + jnp.where(lane == i2, e * w1, 0.0)


def _outproj_kernel(ya_ref, yr_ref, g_ref, x_ref, mod_ref, wua_ref, wur_ref, wo_ref, nw2_ref, *refs,
                    router, tiles_per_block):
    if router:
        rw_ref, tri_ref, xo_ref, h2a_ref, h2b_ref, route_ref, cnt_ref, carry_scr = refs
    else:
        xo_ref, h2_ref = refs
    d = x_ref.shape[1]
    ua = jnp.dot(ya_ref[...], wua_ref[...], preferred_element_type=F32)
    ur = jnp.dot(yr_ref[...], wur_ref[...], preferred_element_type=F32)
    u = g_ref[:, :d].astype(F32) * ua + g_ref[:, d:].astype(F32) * ur
    y = jnp.dot(u.astype(BF16), wo_ref[...], preferred_element_type=F32)
    xn = x_ref[...] + mod_ref[2:3, :] * y
    xo_ref[...] = xn
    h2 = (_rms(xn) * nw2_ref[...]) * (1.0 + mod_ref[4:5, :]) + mod_ref[3:4, :]
    h2_hi = h2.astype(BF16)
    if not router:
        h2_ref[...] = h2_hi
    if router:
        h2a_ref[...], h2b_ref[...] = _pack_row(h2)
        h2_lo = (h2 - h2_hi.astype(F32)).astype(BF16)
        logits = (jnp.dot(h2_hi, rw_ref[0], preferred_element_type=F32)
                  + jnp.dot(h2_lo, rw_ref[0], preferred_element_type=F32)
                  + jnp.dot(h2_hi, rw_ref[1], preferred_element_type=F32))
        lane = lax.broadcasted_iota(jnp.int32, logits.shape, 1)
        gates = _top2_gates(jnp.where(lane < N_EXPERTS, logits, -jnp.inf))

        @pl.when(pl.program_id(0) % tiles_per_block == 0)
        def _():
            carry_scr[...] = jnp.zeros_like(carry_scr)

        sel = gates > 0.0
        sel_f = jnp.where(sel, 1.0, 0.0)
        rank = jnp.dot(tri_ref[...], sel_f.astype(BF16), preferred_element_type=F32) + carry_scr[...]
        total = carry_scr[...] + jnp.sum(sel_f, axis=0, keepdims=True)
        carry_scr[...] = total
        cnt_ref[...] = total

        lanef = lane.astype(F32)
        e_a = jnp.min(jnp.where(sel, lanef, float(LANES)), axis=-1, keepdims=True)
        e_b = jnp.max(jnp.where(sel, lanef, -1.0), axis=-1, keepdims=True)
        is_a = lanef == e_a
        is_b = lanef == e_b

        def pick(mask, v):
            return jnp.sum(jnp.where(mask, v, 0.0), axis=-1, keepdims=True)

        fields = (pick(is_a, rank), pick(is_b, rank), e_a, e_b, pick(is_a, gates),
                  jnp.where(e_b == e_a, 0.0, pick(is_b, gates)))
        route = jnp.zeros_like(gates)
        for k, v in enumerate(fields):
            route = jnp.where(lane == k, v, route)
        route_ref[...] = route


def _outproj_call(ya, yr, g, x, mod, wua, wur, wo, nw2, rw, *, tm, rows_per_mod, route_block=None):
    n, d = x.shape
    assert n % tm == 0, (n, tm)
    router = rw is not None
    row = lambda i: (i, 0)
    const = lambda i: (0, 0)
    tiles_per_block = route_block // tm if router else 1
    if rows_per_mod is None:
        mod_map = lambda i: (mod.shape[0] - 1, 0, 0)
    else:
        mod_map = lambda i: (i // (rows_per_mod // tm), 0, 0)
    in_specs = [
        pl.BlockSpec((tm, ya.shape[1]), row),
        pl.BlockSpec((tm, yr.shape[1]), row),
        pl.BlockSpec((tm, 2 * d), row),
        pl.BlockSpec((tm, d), row),
        pl.BlockSpec((None, 6, d), mod_map),
        pl.BlockSpec(wua.shape, const),
        pl.BlockSpec(wur.shape, const),
        pl.BlockSpec(wo.shape, const),
        pl.BlockSpec((1, d), const),
    ]
    args = [ya, yr, g, x, mod, wua, wur, wo, nw2]
    out_specs = [pl.BlockSpec((tm, d), row)]
    out_shape = [jax.ShapeDtypeStruct((n, d), F32)]
    scratch = []
    if router:
        tri = jnp.asarray(np.tril(np.ones((tm, tm), np.float32), -1), BF16)
        in_specs += [pl.BlockSpec(rw.shape, lambda i: (0, 0, 0)), pl.BlockSpec(tri.shape, const)]
        args += [rw, tri]
        out_specs += [pl.BlockSpec((tm, d // 4), row), pl.BlockSpec((tm, d // 4), row),
                      pl.BlockSpec((tm, LANES), row),
                      pl.BlockSpec((None, 1, LANES), lambda i: (i // tiles_per_block, 0, 0))]
        out_shape += [jax.ShapeDtypeStruct((n, d // 4), jnp.uint32), jax.ShapeDtypeStruct((n, d // 4), jnp.uint32),
                      jax.ShapeDtypeStruct((n, LANES), F32),
                      jax.ShapeDtypeStruct((n // route_block, 1, LANES), F32)]
        scratch = [pltpu.VMEM((1, LANES), F32)]
    else:
        out_specs.append(pl.BlockSpec((tm, d), row))
        out_shape.append(jax.ShapeDtypeStruct((n, d), BF16))
    return pl.pallas_call(
        functools.partial(_outproj_kernel, router=router, tiles_per_block=tiles_per_block),
        grid=(n // tm,),
        in_specs=in_specs,
        out_specs=out_specs,
        out_shape=out_shape,
        scratch_shapes=scratch,
        compiler_params=_cparams("arbitrary"),
        name="outproj_router" if router else "outproj",
    )(*args)


def _swiglu_chunk(xs, w1, w3, w2):
    a = jnp.dot(xs, w1, preferred_element_type=F32)
    b = jnp.dot(xs, w3, preferred_element_type=F32)
    return jnp.dot((_silu(a) * b).astype(BF16), w2, preferred_element_type=F32)


def _ffn_kernel(h_ref, x_ref, mod_ref, w1_ref, w3_ref, w2_ref, o_ref, *, tf):
    h = h_ref[...]
    ff = w1_ref.shape[1]
    acc = jnp.zeros(o_ref.shape, F32)
    for lo in range(0, ff, tf):
        hi = min(lo + tf, ff)
        a = jnp.dot(h, w1_ref[:, lo:hi], preferred_element_type=F32)
        b = jnp.dot(h, w3_ref[:, lo:hi], preferred_element_type=F32)
        acc = acc + jnp.dot((_silu(a) * b).astype(BF16), w2_ref[lo:hi, :], preferred_element_type=F32)
    o_ref[...] = x_ref[...] + mod_ref[5:6, :] * acc


def _ffn_call(h, x, mod, w1, w3, w2, *, tm, tf, rows_per_mod):
    n, d = x.shape
    assert n % tm == 0, (n, tm)
    row = lambda i: (i, 0)
    const = lambda i: (0, 0)
    if rows_per_mod is None:
        mod_map = lambda i: (mod.shape[0] - 1, 0, 0)
    else:
        mod_map = lambda i: (i // (rows_per_mod // tm), 0, 0)
    return pl.pallas_call(
        functools.partial(_ffn_kernel, tf=tf),
        grid=(n // tm,),
        in_specs=[
            pl.BlockSpec((tm, d), row), pl.BlockSpec((tm, d), row), pl.BlockSpec((None, 6, d), mod_map),
            pl.BlockSpec(w1.shape, const), pl.BlockSpec(w3.shape, const), pl.BlockSpec(w2.shape, const),
        ],
        out_specs=pl.BlockSpec((tm, d), row),
        out_shape=jax.ShapeDtypeStruct((n, d), F32),
        compiler_params=_cparams("arbitrary"),
        name="ffn",
    )(h, x, mod, w1, w3, w2)


def _pack_halves(x):
    n = x.shape[1] // 2
    lo = lax.bitcast_convert_type(x[:, :n].astype(BF16).astype(F32), jnp.uint32)
    hi = lax.bitcast_convert_type(x[:, n:].astype(BF16).astype(F32), jnp.uint32)
    return (hi & jnp.uint32(0xFFFF0000)) | (lo >> 16)


def _unpack_halves(w):
    lo = lax.bitcast_convert_type(w << 16, F32)
    hi = lax.bitcast_convert_type(w & jnp.uint32(0xFFFF0000), F32)
    return jnp.concatenate([lo, hi], axis=1)


def _sc_gather_rows(tables, idxs, *, window):
    n_jobs = len(tables)
    mesh = plsc.VectorSubcoreMesh(core_axis_name="c", subcore_axis_name="s")
    out_type = [jax.ShapeDtypeStruct((idx.shape[0], t.shape[1]), t.dtype) for t, idx in zip(tables, idxs)]
    for idx in idxs:
        assert idx.shape[0] % window == 0, (idx.shape, window)

    @pl.kernel(out_type=out_type, mesh=mesh)
    def gather(*refs):
        for j in range(n_jobs):
            table_hbm, idx_hbm, out_hbm = refs[j], refs[n_jobs + j], refs[2 * n_jobs + j]

            def body(idx_vmem, out_vmem, table_hbm=table_hbm):
                pltpu.sync_copy(table_hbm.at[idx_vmem.at[0]], out_vmem)

            pltpu.emit_pipeline(
                body,
                grid=(idx_hbm.shape[1] // window,),
                in_specs=[pl.BlockSpec((1, window), lambda i: (0, i))],
                out_specs=[pl.BlockSpec((window, table_hbm.shape[1]), lambda i: (i, 0))],
                core_axis_name=("c", "s"),
                dimension_semantics=(pltpu.PARALLEL,),
            )(idx_hbm, out_hbm)

    return gather(*tables, *[idx.reshape(1, -1) for idx in idxs])


def _packed_row(pair):
    return jnp.concatenate([_unpack_halves(pair[0][...]), _unpack_halves(pair[1][...])], axis=1)


def _pack_row(x):
    half = x.shape[1] // 2
    return _pack_halves(x[:, :half]), _pack_halves(x[:, half:])


def _experts_kernel(tile_expert_ref, n_active_ref, xa_ref, xb_ref, w1_ref, w3_ref, w2_ref, oa_ref, ob_ref,
                    x_scr, acc_ref):
    i, f = pl.program_id(0), pl.program_id(1)

    @pl.when(i < n_active_ref[0])
    def _():
        @pl.when(f == 0)
        def _():
            x_scr[...] = _packed_row((xa_ref, xb_ref)).astype(BF16)
            acc_ref[...] = jnp.zeros_like(acc_ref)

        w1, w3, w2 = (w_ref[...].astype(BF16) for w_ref in (w1_ref, w3_ref, w2_ref))
        n_groups = max(1, x_scr.shape[0] // EXPERT_ROWS)
        group = x_scr.shape[0] // n_groups
        for g in range(n_groups):
            rows = slice(g * group, (g + 1) * group)
            acc_ref[rows, :] += _swiglu_chunk(x_scr[rows, :], w1, w3, w2)

        @pl.when(f == pl.num_programs(1) - 1)
        def _():
            oa_ref[...], ob_ref[...] = _pack_row(acc_ref[...])


def _experts_call(tile_expert, n_active, xa, xb, w1, w3, w2, *, tm, tf):
    n, dp = xa.shape
    assert n % tm == 0, (n, tm)
    _, d, ff = w1.shape
    rows = pl.BlockSpec((tm, dp), lambda i, f, te, na: (i, 0))
    grid_spec = pltpu.PrefetchScalarGridSpec(
        num_scalar_prefetch=2,
        grid=(n // tm, ff // tf),
        in_specs=[
            rows, rows,
            pl.BlockSpec((None, d, tf), lambda i, f, te, na: (te[i], 0, f)),
            pl.BlockSpec((None, d, tf), lambda i, f, te, na: (te[i], 0, f)),
            pl.BlockSpec((None, tf, d), lambda i, f, te, na: (te[i], f, 0)),
        ],
        out_specs=[rows, rows],
        scratch_shapes=[pltpu.VMEM((tm, d), BF16), pltpu.VMEM((tm, d), F32)],
    )
    return pl.pallas_call(
        _experts_kernel,
        grid_spec=grid_spec,
        out_shape=[jax.ShapeDtypeStruct((n, dp), jnp.uint32)] * 2,
        compiler_params=_cparams("arbitrary", "arbitrary"),
        name="experts",
    )(tile_expert, n_active, xa, xb, w1, w3, w2)


def _combine_kernel(x_ref, yaa_ref, yab_ref, yba_ref, ybb_ref, route_ref, mod_ref, fnw_ref, o_ref):
    y = (route_ref[:, 4:5] * _packed_row((yaa_ref, yab_ref))
         + route_ref[:, 5:6] * _packed_row((yba_ref, ybb_ref)))
    o_ref[...] = _rms(x_ref[...] + mod_ref[5:6, :] * y) * fnw_ref[...]


def _combine_call(x, ys, route, mod, fnw, *, tm, rows_per_mod):
    n, d = x.shape
    assert n % tm == 0, (n, tm)
    row = lambda i: (i, 0)
    return pl.pallas_call(
        _combine_kernel,
        grid=(n // tm,),
        in_specs=[pl.BlockSpec((tm, d), row)] + [pl.BlockSpec((tm, d // 4), row)] * 4 + [
            pl.BlockSpec((tm, LANES), row),
            pl.BlockSpec((None, 6, d), lambda i: (i // (rows_per_mod // tm), 0, 0)),
            pl.BlockSpec((1, d), lambda i: (0, 0))],
        out_specs=pl.BlockSpec((tm, d), row),
        out_shape=jax.ShapeDtypeStruct((n, d), F32),
        compiler_params=_cparams("arbitrary"),
        name="combine",
    )(x, *ys, route, mod, fnw)


def _dispatch_plan(route, counts, *, tm):
    n = route.shape[0]
    n_e = counts.shape[0]
    n_slots = TOP_K * n + n_e * tm
    seg = (counts + tm - 1) // tm * tm
    ends = jnp.cumsum(seg)
    offs = ends - seg
    rec = route[:, :4].astype(jnp.int32)
    slot_a = jnp.take(offs, rec[:, 2]) + rec[:, 0]
    slot_b = jnp.take(offs, rec[:, 3]) + rec[:, 1]
    n_tiles = n_slots // tm
    n_active = (ends[-1] // tm).astype(jnp.int32)
    tile_start = jnp.arange(n_tiles, dtype=jnp.int32) * tm
    tile_start = jnp.minimum(tile_start, jnp.maximum(ends[-1] - tm, 0))
    tile_expert = jnp.minimum(jnp.sum(tile_start[:, None] >= ends[None, :], axis=1).astype(jnp.int32), n_e - 1)

    tok = jnp.arange(n, dtype=jnp.int32)
    key_b = jnp.where(rec[:, 3] == rec[:, 2], n_slots + tok, slot_b)
    _, by_slot = lax.sort_key_val(jnp.concatenate([slot_a, key_b]), jnp.concatenate([tok, tok]))
    e_of_slot = jnp.repeat(tile_expert, tm)
    packed_before = jnp.cumsum(counts) - counts
    src = jnp.arange(n_slots, dtype=jnp.int32) - jnp.take(offs, e_of_slot) + jnp.take(packed_before, e_of_slot)
    token_of_slot = jnp.take(by_slot, jnp.clip(src, 0, TOP_K * n - 1))
    return token_of_slot, slot_a, slot_b, tile_expert, n_active.reshape(1)


def _rope_tables(t_lat):
    rows = t_lat // GRID_W
    row = jnp.repeat(jnp.arange(rows, dtype=F32), GRID_W)
    col = jnp.tile(jnp.arange(GRID_W, dtype=F32), rows)
    n_freq = ATT_QK_DIM // 4
    inv_freq = ROPE_THETA ** (-jnp.arange(n_freq, dtype=F32) / n_freq)
    ang = jnp.concatenate([row[:, None] * inv_freq, col[:, None] * inv_freq], axis=-1)
    cos, sin = jnp.cos(ang), jnp.sin(ang)
    return (jnp.concatenate([cos, cos, cos, cos], axis=-1), jnp.concatenate([-sin, -sin, sin, sin], axis=-1))


def _layer_lower_bounds(lb_param):
    cs = jnp.cumsum(jax.nn.softmax(lb_param.astype(F32), axis=0), axis=0)
    return cs - cs[0:1]


def _win_columns(w_in_l):
    c = [w_in_l[:, i * 512:(i + 1) * 512] for i in range(8)]
    half = ATT_QK_DIM // 2

    def pair_halves(w):
        w = w.reshape(w.shape[0], ATT_HEADS, 2, 2, half)
        return w.transpose(0, 1, 3, 2, 4).reshape(w.shape[0], -1)

    return jnp.concatenate([pair_halves(c[0]), pair_halves(c[1]), c[2], c[7], c[3], c[4], c[5], c[6],
                            w_in_l[:, 4096:]], axis=1).astype(BF16)


def _pad_ff(w, axis, mult):
    ff = w.shape[axis]
    pad = (-ff) % mult
    if pad == 0:
        return w
    widths = [(0, 0)] * w.ndim
    widths[axis] = (0, pad)
    return jnp.pad(w, widths)


def kernel(x, c, ctx, c_ctx, w_ada, b_ada, norm_mix_w, norm_ffn_w, w_in, lambda_q1, lambda_k1, lambda_q2,
           lambda_k2, att_norm_w, rec_norm_w, lb_fwd, lb_bwd, w_up_att, w_up_rec, w_out, ffn_w1, ffn_w3,
           ffn_w2, router_w, moe_w1, moe_w3, moe_w2, final_norm_w):
    batch, t_lat, d = x.shape
    t_ctx = ctx.shape[1]
    depth = w_ada.shape[0]
    n_lat, n_ctx = batch * t_lat, batch * t_ctx
    tm = 512
    tm_in = 512
    tq = min(1024, t_lat)

    xl = x.reshape(n_lat, d)
    xc = ctx.reshape(n_ctx, d)

    pad_rows = (-(batch + 1)) % 8
    cc = jnp.concatenate([c, jnp.zeros((pad_rows, d), F32), c_ctx[None, :]], axis=0)
    mod_all = _mod_call(cc, w_ada, b_ada).reshape(depth, cc.shape[0], 6, d)

    cos, sin = _rope_tables(t_lat)
    lbs_f = _layer_lower_bounds(lb_fwd)
    lbs_b = _layer_lower_bounds(lb_bwd)

    for l in range(depth):
        last = l == depth - 1
        mod = mod_all[l]
        lam_init = 0.8 - 0.6 * math.exp(-0.3 * l)
        lam = (jnp.exp(jnp.sum(lambda_q1[l] * lambda_k1[l])) - jnp.exp(jnp.sum(lambda_q2[l] * lambda_k2[l]))
               + lam_init).reshape(1).astype(F32)
        lb = jnp.concatenate([lbs_f[l], lbs_b[l]])[None, :]
        w_l = _win_columns(w_in[l])
        nw = norm_mix_w[l][None, :]

        pl_lat = _inproj_call(xl, mod, nw, w_l, cos, sin, lb, tm=tm_in, rows_per_mod=t_lat, rope=True)
        pl_ctx = _inproj_call(xc, mod, nw, w_l, cos, sin, lb, tm=min(tm_in, n_ctx), rows_per_mod=None,
                              rope=False)
        qk_l, vg_l, qi_l, lf_l, g_l = pl_lat
        qk_c, vg_c, qi_c, lf_c, g_c = pl_ctx

        anw = att_norm_w[l][None, :]
        ya_l = _attn_call(lam, qk_l, qk_l, vg_l, qk_c, vg_c, anw, batch=batch, tq=tq,
                          lam_init=lam_init, with_lat=True)
        yr_l, yr_c = _gla_call(qi_l, lf_l, vg_l, qi_c, lf_c, vg_c, rec_norm_w[l][None, :], batch=batch)

        wua = w_up_att[l].astype(BF16)
        wur = w_up_rec[l].astype(BF16)
        wo = w_out[l].astype(BF16)
        nw2 = norm_ffn_w[l][None, :]
        moe_layer = l % 2 == 1
        j = l // 2
        assert moe_layer == last
        rw = None
        if moe_layer:
            rw32 = jnp.pad(router_w[j], ((0, 0), (0, LANES - N_EXPERTS)))
            rw_hi = rw32.astype(BF16)
            rw = jnp.stack([rw_hi, (rw32 - rw_hi.astype(F32)).astype(BF16)])
        res = _outproj_call(ya_l, yr_l, g_l, xl, mod, wua, wur, wo, nw2, rw, tm=tm, rows_per_mod=t_lat,
                            route_block=n_lat)
        xl = res[0]
        if not last:
            ya_c = _attn_call(lam, qk_c, None, None, qk_c, vg_c, anw, batch=batch, tq=t_ctx,
                              lam_init=lam_init, with_lat=False)
            xc, h2_c = _outproj_call(ya_c, yr_c, g_c, xc, mod, wua, wur, wo, nw2, None, tm=tm, rows_per_mod=None)

        if moe_layer:
            h2a, h2b, route, cnt = res[1:]
            counts = cnt[0, 0, :N_EXPERTS].astype(jnp.int32)
            mt = min(MOE_TILE, n_lat)
            token_of_slot, slot_a, slot_b, tile_expert, n_active = _dispatch_plan(route, counts, tm=mt)
            xa, xb = _sc_gather_rows([h2a, h2b], [token_of_slot] * 2, window=SC_WINDOW)
            oa, ob = _experts_call(tile_expert, n_active, xa, xb, moe_w1[j], moe_w3[j], moe_w2[j], tm=mt, tf=512)
            ys = _sc_gather_rows([oa, ob, oa, ob], [slot_a, slot_a, slot_b, slot_b], window=SC_WINDOW)
            xl = _combine_call(xl, ys, route, mod, final_norm_w[None, :], tm=tm, rows_per_mod=t_lat)
        else:
            h2_l = res[1]
            w1 = _pad_ff(ffn_w1[j], 1, 2 * LANES).astype(BF16)
            w3 = _pad_ff(ffn_w3[j], 1, 2 * LANES).astype(BF16)
            w2 = _pad_ff(ffn_w2[j], 0, 2 * LANES).astype(BF16)
            tf = 512
            xl = _ffn_call(h2_l, xl, mod, w1, w3, w2, tm=tm, tf=tf, rows_per_mod=t_lat)
            xc = _ffn_call(h2_c, xc, mod, w1, w3, w2, tm=tm, tf=tf, rows_per_mod=None)

    return xl.reshape(batch, t_lat, d)
```

```python
import functools
import math

import numpy as np
import jax
import jax.numpy as jnp
from jax import lax
from jax.experimental import pallas as pl
from jax.experimental.pallas import tpu as pltpu
from jax.experimental.pallas import tpu_sc as plsc

F32 = jnp.float32
BF16 = jnp.bfloat16
HIGHEST = lax.Precision.HIGHEST

EPS = 1e-6
GRID_W = 64
ROPE_THETA = 10000.0
ATT_HEADS = 4
ATT_QK_DIM = 64
REC_HEADS = 4
N_EXPERTS = 8
TOP_K = 2
Q_SCALE = ATT_QK_DIM ** -0.5 * math.log2(math.e)

LANES = 128
GLA_BLOCK = 256
GLA_CHUNK = 256
GLA_UNROLL = 4
GLA_LEVELS = tuple(2 ** i for i in range(GLA_CHUNK.bit_length() - 1))
VMEM_LIMIT = 56 * 1024 * 1024
MOE_TILE = 1024
SC_WINDOW = 128
ATT_KEYS = 256
ATT_ROWS = 512

NT_DIMS = (((1,), (1,)), ((), ()))
TN_DIMS = (((0,), (0,)), ((), ()))


def _cparams(*sem):
    return pltpu.CompilerParams(dimension_semantics=sem, vmem_limit_bytes=VMEM_LIMIT)


def _silu(a):
    return a * jax.nn.sigmoid(a)


def _rms(x):
    return x * lax.rsqrt(jnp.mean(x * x, axis=-1, keepdims=True) + EPS)


def _mod_kernel(c_ref, w_ref, b_ref, o_ref):
    s = _silu(c_ref[...])
    o_ref[...] = jnp.dot(s, w_ref[...], precision=HIGHEST, preferred_element_type=F32) + b_ref[...]


def _mod_call(cc, w_ada, b_ada):
    depth, d, n = w_ada.shape
    rows = cc.shape[0]
    tn = 1536
    return pl.pallas_call(
        _mod_kernel,
        grid=(depth, n // tn),
        in_specs=[
            pl.BlockSpec((rows, d), lambda l, j: (0, 0)),
            pl.BlockSpec((None, d, tn), lambda l, j: (l, 0, j)),
            pl.BlockSpec((None, 1, tn), lambda l, j: (l, 0, j)),
        ],
        out_specs=pl.BlockSpec((None, rows, tn), lambda l, j: (l, 0, j)),
        out_shape=jax.ShapeDtypeStruct((depth, rows, n), F32),
        compiler_params=_cparams("arbitrary", "arbitrary"),
        name="mod",
    )(cc, w_ada, b_ada.reshape(depth, 1, n))


def _log_forget(z, lb):
    t = jnp.exp(-jnp.abs(z))
    num = jnp.where(z >= 0.0, 1.0 + lb * t, lb + t)
    return jnp.where(num > 0.0, jnp.log(num / (1.0 + t)), z)


def _inproj_kernel(x_ref, mod_ref, nw_ref, w_ref, cos_ref, sin_ref, lb_ref,
                   qk_ref, vg_ref, qi_ref, lf_ref, g_ref, *, rope):
    h = _rms(x_ref[...]) * nw_ref[...]
    h = (h * (1.0 + mod_ref[1:2, :]) + mod_ref[0:1, :]).astype(BF16)
    ts = qk_ref.shape[1]
    half = ts // 2

    def proj(s):
        return jnp.dot(h, w_ref[:, s * ts:(s + 1) * ts], preferred_element_type=F32)

    p = proj(0)
    if rope:
        cos = cos_ref[...]
        sin = sin_ref[...]
    for j in range(ts // LANES):
        blk = p[:, j * LANES:(j + 1) * LANES]
        if rope:
            blk = blk * cos + pltpu.roll(blk, LANES // 2, 1) * sin
        if j * LANES < half:
            blk = blk * Q_SCALE
        qk_ref[:, j * LANES:(j + 1) * LANES] = blk.astype(BF16)

    p = proj(1)
    vg_ref[:, :half] = p[:, :half].astype(BF16)
    vg_ref[:, half:] = _silu(p[:, half:]).astype(BF16)

    p = proj(2)
    qi_ref[:, :half] = _silu(p[:, :half]).astype(BF16)
    qi_ref[:, half:] = p[:, half:].astype(BF16)

    lf_ref[...] = _log_forget(proj(3), lb_ref[...])

    g_ref[:, :ts] = jax.nn.sigmoid(proj(4)).astype(BF16)
    g_ref[:, ts:] = jax.nn.sigmoid(proj(5)).astype(BF16)


def _inproj_call(x, mod, nw, w, cos, sin, lb, *, tm, rows_per_mod, rope):
    n, d = x.shape
    assert n % tm == 0, (n, tm)
    ts = 1024
    assert w.shape[1] == 6 * ts
    pos_tiles = cos.shape[0] // tm
    row = lambda i: (i, 0)
    const = lambda i: (0, 0)
    if rows_per_mod is None:
        mod_map = lambda i: (mod.shape[0] - 1, 0, 0)
    else:
        mod_map = lambda i: (i // (rows_per_mod // tm), 0, 0)
    outs = pl.pallas_call(
        functools.partial(_inproj_kernel, rope=rope),
        grid=(n // tm,),
        in_specs=[
            pl.BlockSpec((tm, d), row),
            pl.BlockSpec((None, 6, d), mod_map),
            pl.BlockSpec((1, d), const),
            pl.BlockSpec(w.shape, const),
            pl.BlockSpec((tm, LANES), lambda i: (i % pos_tiles, 0)),
            pl.BlockSpec((tm, LANES), lambda i: (i % pos_tiles, 0)),
            pl.BlockSpec((1, ts), const),
        ],
        out_specs=[
            pl.BlockSpec((tm, ts), row),
            pl.BlockSpec((tm, ts), row),
            pl.BlockSpec((tm, ts), row),
            pl.BlockSpec((tm, ts), row),
            pl.BlockSpec((tm, 2 * ts), row),
        ],
        out_shape=[
            jax.ShapeDtypeStruct((n, ts), BF16),
            jax.ShapeDtypeStruct((n, ts), BF16),
            jax.ShapeDtypeStruct((n, ts), BF16),
            jax.ShapeDtypeStruct((n, ts), F32),
            jax.ShapeDtypeStruct((n, 2 * ts), BF16),
        ],
        compiler_params=_cparams("arbitrary"),
        name="inproj_rope" if rope else "inproj",
    )(x, mod, nw, w, cos, sin, lb)
    return outs


def _attn_kernel(lam_ref, q_ref, *refs, post_scale, with_lat):
    if with_lat:
        kl_ref, vl_ref, kc_ref, vc_ref, nw_ref, o_ref, vce_scr, vle_scr = refs
    else:
        kc_ref, vc_ref, nw_ref, o_ref, vce_scr = refs

    @pl.when(pl.program_id(2) == 0)
    def _():
        vce_scr[:, :LANES] = vc_ref[...]
        vce_scr[:, LANES:] = jnp.ones(vc_ref.shape, BF16)
        if with_lat:
            vle_scr[:, :LANES] = vl_ref[...]
            vle_scr[:, LANES:] = jnp.ones(vl_ref.shape, BF16)

    n_parts = max(1, q_ref.shape[0] // ATT_ROWS)
    rows = q_ref.shape[0] // n_parts
    qq = []
    for part in range(n_parts):
        q = q_ref[part * rows:(part + 1) * rows, :]
        lane = lax.broadcasted_iota(jnp.int32, q.shape, 1)
        zero = jnp.zeros_like(q)
        sub1 = (lane % ATT_QK_DIM) < (ATT_QK_DIM // 2)
        qq.append(jnp.concatenate([jnp.where(sub1, q, zero), jnp.where(sub1, zero, q)], axis=0))

    blocks = [(kc_ref, vce_scr, 0, kc_ref.shape[0])]
    if with_lat:
        blocks += [(kl_ref, vle_scr, j, ATT_KEYS) for j in range(0, kl_ref.shape[0], ATT_KEYS)]
    m = [jnp.full((2 * rows, 1), -jnp.inf, F32)] * n_parts
    acc = [jnp.zeros((2 * rows, 2 * LANES), F32)] * n_parts
    for k_ref, v_scr, start, size in blocks:
        for part in range(n_parts):
            s = lax.dot_general(qq[part], k_ref[start:start + size, :], NT_DIMS, preferred_element_type=F32)
            m_new = jnp.maximum(m[part], jnp.max(s, axis=-1, keepdims=True))
            p = jnp.exp2(s - m_new).astype(BF16)
            acc[part] = acc[part] * jnp.exp2(m[part] - m_new) + jnp.dot(
                p, v_scr[start:start + size, :], preferred_element_type=F32)
            m[part] = m_new
    for part in range(n_parts):
        on = acc[part][:, :LANES] * (1.0 / acc[part][:, LANES:])
        o = on[:rows] - lam_ref[0] * on[rows:]
        o_ref[part * rows:(part + 1) * rows, :] = (_rms(o) * nw_ref[...] * post_scale).astype(BF16)


def _attn_call(lam, q_src, qk_lat, vg_lat, qk_ctx, vg_ctx, nw, *, batch, tq, lam_init, with_lat):
    n_q = q_src.shape[0]
    t_q = n_q // batch
    t_lat = qk_lat.shape[0] // batch if with_lat else 0
    t_ctx = qk_ctx.shape[0] // batch
    nq_tiles = t_q // tq
    h = ATT_HEADS
    in_specs = [
        pl.BlockSpec(memory_space=pltpu.SMEM),
        pl.BlockSpec((tq, LANES), lambda b, hh, i: (b * nq_tiles + i, hh)),
    ]
    args = [lam, q_src]
    if with_lat:
        in_specs += [
            pl.BlockSpec((t_lat, LANES), lambda b, hh, i: (b, h + hh)),
            pl.BlockSpec((t_lat, LANES), lambda b, hh, i: (b, hh)),
        ]
        args += [qk_lat, vg_lat]
    in_specs += [
        pl.BlockSpec((t_ctx, LANES), lambda b, hh, i: (b, h + hh)),
        pl.BlockSpec((t_ctx, LANES), lambda b, hh, i: (b, hh)),
        pl.BlockSpec((1, LANES), lambda b, hh, i: (0, 0)),
    ]
    args += [qk_ctx, vg_ctx, nw]
    return pl.pallas_call(
        functools.partial(_attn_kernel, post_scale=1.0 - lam_init, with_lat=with_lat),
        grid=(batch, h, nq_tiles),
        in_specs=in_specs,
        out_specs=pl.BlockSpec((tq, LANES), lambda b, hh, i: (b * nq_tiles + i, hh)),
        out_shape=jax.ShapeDtypeStruct((n_q, h * LANES), BF16),
        scratch_shapes=[pltpu.VMEM((t_ctx, 2 * LANES), BF16)]
        + ([pltpu.VMEM((t_lat, 2 * LANES), BF16)] if with_lat else []),
        compiler_params=_cparams("arbitrary", "arbitrary", "arbitrary"),
        name="attn_lat" if with_lat else "attn_ctx",
    )(*args)


def _gla_constants():
    n = GLA_BLOCK
    idx = np.arange(n)
    same_chunk = (idx[:, None] // GLA_CHUNK) == (idx[None, :] // GLA_CHUNK)
    stacks, masks = [], []
    for reverse in (False, True):
        order = (idx[None, :] >= idx[:, None]) if reverse else (idx[None, :] <= idx[:, None])
        cum = (same_chunk & order).astype(np.int32)
        groups, lvl_masks = [cum], []
        for m in GLA_LEVELS:
            ref = (idx // (2 * m)) * (2 * m) + (m if reverse else m - 1)
            groups.append(np.abs(cum - cum[ref]))
            upper = (idx % (2 * m)) >= m
            same = (idx[:, None] // (2 * m)) == (idx[None, :] // (2 * m))
            q_side, k_side = (~upper, upper) if reverse else (upper, ~upper)
            lvl_masks.append(same & q_side[:, None] & k_side[None, :])
        end = (idx // GLA_CHUNK) * GLA_CHUNK + (0 if reverse else GLA_CHUNK - 1)
        groups.append(np.abs(cum[end] - cum))
        stacks.append(np.concatenate(groups, axis=0))
        masks.append(np.stack(lvl_masks))
    return jnp.asarray(np.stack(stacks), BF16), jnp.asarray(np.stack(masks), F32)


def _gla_prepare(q, v, lf, stack, *, reverse):
    n = q.shape[0]
    qf = q.astype(F32)
    k = 1.0 - jnp.exp(lf)

    hi = lf.astype(BF16)
    mid = (lf - hi.astype(F32)).astype(BF16)
    expo = jnp.dot(stack, jnp.concatenate([hi, mid], axis=1), preferred_element_type=F32)

    def exponent(group):
        blk = expo[group * n:(group + 1) * n]
        return blk[:, :LANES] + blk[:, LANES:]

    scale = [jnp.exp(exponent(1 + li)) for li in range(len(GLA_LEVELS))]
    qs = jnp.concatenate([(qf * e).astype(BF16) for e in scale], axis=1)
    ks = jnp.concatenate([(k * e).astype(BF16) for e in scale], axis=1)
    c = exponent(0)
    io = jnp.concatenate([(qf * jnp.exp(c)).astype(BF16),
                          (k * jnp.exp(exponent(1 + len(GLA_LEVELS)))).astype(BF16)], axis=1)
    diag = jnp.sum(qf * k, axis=-1, keepdims=True) * v.astype(F32)
    last = 0 if reverse else GLA_CHUNK - 1
    n_chunks = n // GLA_CHUNK
    decay = jnp.concatenate([jnp.exp(c[ci * GLA_CHUNK + last:ci * GLA_CHUNK + last + 1, :])
                             for ci in range(n_chunks)] + [jnp.ones((8 - n_chunks, LANES), F32)], axis=0)
    return qs, ks, io, diag, decay


def _gla_scan(qs, ks, io, diag, decay, v, st, masks, *, reverse):
    n = v.shape[0]
    scores = jnp.zeros((n, n), F32)
    for li in range(len(GLA_LEVELS)):
        cols = slice(li * LANES, (li + 1) * LANES)
        s_l = lax.dot_general(qs[:, cols], ks[:, cols], NT_DIMS, preferred_element_type=F32)
        scores = scores + s_l * masks[li]
    o = jnp.dot(scores.astype(BF16), v, preferred_element_type=F32) + diag

    n_chunks = n // GLA_CHUNK
    outs = [None] * n_chunks
    order = range(n_chunks - 1, -1, -1) if reverse else range(n_chunks)
    for ci in order:
        r = slice(ci * GLA_CHUNK, (ci + 1) * GLA_CHUNK)
        outs[ci] = lax.dot_general(io[r, :LANES], st.astype(BF16), NT_DIMS, preferred_element_type=F32)
        upd = lax.dot_general(v[r], io[r, LANES:], TN_DIMS, preferred_element_type=F32)
        st = st * decay[ci:ci + 1, :] + upd
    return o + jnp.concatenate(outs, axis=0), st


def _gla_kernel(ql_ref, il_ref, ffl_ref, fbl_ref, gl_ref, qc_ref, ic_ref, ffc_ref, fbc_ref, gc_ref,
                nw_ref, stack_ref, mask_ref, ol_ref, oc_ref,
                qs_scr, ks_scr, io_scr, dg_scr, dc_scr, of_scr, ob_scr):
    n_blocks = ql_ref.shape[0] // GLA_BLOCK
    nw = nw_ref[...]

    def finish(o, g):
        return (_rms(o) * nw * g.astype(F32)).astype(BF16)

    def rows(i):
        return pl.ds(pl.multiple_of(i * GLA_BLOCK, GLA_BLOCK), GLA_BLOCK)

    def prepare(slot, q, v, lf_f, lf_b):
        for d, lf in enumerate((lf_f, lf_b)):
            qs, ks, io, diag, decay = _gla_prepare(q, v, lf, stack_ref[d], reverse=bool(d))
            qs_scr[d, slot] = qs
            ks_scr[d, slot] = ks
            io_scr[d, slot] = io
            dg_scr[d, slot] = diag
            dc_scr[d, slot] = decay

    def scan(d, slot, v, st):
        return _gla_scan(qs_scr[d, slot], ks_scr[d, slot], io_scr[d, slot], dg_scr[d, slot], dc_scr[d, slot],
                         v, st, mask_ref[d], reverse=bool(d))

    prepare(0, qc_ref[...], ic_ref[...], ffc_ref[...], fbc_ref[...])

    def prep(i, carry):
        r = rows(i)
        prepare(i + 1, ql_ref[r, :], il_ref[r, :], ffl_ref[r, :], fbl_ref[r, :])
        return carry

    unroll = min(GLA_UNROLL, n_blocks)
    lax.fori_loop(0, n_blocks, prep, 0, unroll=min(2 * GLA_UNROLL, n_blocks))

    zero_state = jnp.zeros((LANES, LANES), F32)
    o_cf, st_f = scan(0, 0, ic_ref[...], zero_state)
    o_cb, st_b = scan(1, 0, ic_ref[...], zero_state)
    oc_ref[...] = finish(o_cf + o_cb, gc_ref[...])

    def step(i, carry):
        st_f, st_b = carry
        j = n_blocks - 1 - i
        o_f, st_f = scan(0, i + 1, il_ref[rows(i), :], st_f)
        o_b, st_b = scan(1, j + 1, il_ref[rows(j), :], st_b)
        of_scr[rows(i), :] = o_f
        ob_scr[rows(j), :] = o_b
        return st_f, st_b

    lax.fori_loop(0, n_blocks, step, (st_f, st_b), unroll=min(2 * GLA_UNROLL, n_blocks))

    def emit(i, carry):
        r = rows(i)
        ol_ref[r, :] = finish(of_scr[r, :] + ob_scr[r, :], gl_ref[r, :])
        return carry

    lax.fori_loop(0, n_blocks, emit, 0, unroll=unroll)


def _gla_call(qi_lat, lf_lat, vg_lat, qi_ctx, lf_ctx, vg_ctx, nw, *, batch):
    t_lat = qi_lat.shape[0] // batch
    t_ctx = qi_ctx.shape[0] // batch
    assert t_ctx == GLA_BLOCK and t_lat % GLA_BLOCK == 0
    h = REC_HEADS
    stack, masks = _gla_constants()
    n_slots = 1 + t_lat // GLA_BLOCK
    n_lvl = len(GLA_LEVELS)

    def col(off):
        return lambda b, hh: (b, off + hh)

    def seq(t):
        return [pl.BlockSpec((t, LANES), col(0)), pl.BlockSpec((t, LANES), col(h)),
                pl.BlockSpec((t, LANES), col(0)), pl.BlockSpec((t, LANES), col(h)),
                pl.BlockSpec((t, LANES), col(h))]

    return pl.pallas_call(
        _gla_kernel,
        grid=(batch, h),
        in_specs=seq(t_lat) + seq(t_ctx) + [
            pl.BlockSpec((1, LANES), lambda b, hh: (0, 0)),
            pl.BlockSpec(stack.shape, lambda b, hh: (0, 0, 0)),
            pl.BlockSpec(masks.shape, lambda b, hh: (0, 0, 0, 0)),
        ],
        out_specs=[pl.BlockSpec((t_lat, LANES), col(0)), pl.BlockSpec((t_ctx, LANES), col(0))],
        out_shape=[jax.ShapeDtypeStruct((batch * t_lat, h * LANES), BF16),
                   jax.ShapeDtypeStruct((batch * t_ctx, h * LANES), BF16)],
        scratch_shapes=[
            pltpu.VMEM((2, n_slots, GLA_BLOCK, n_lvl * LANES), BF16),
            pltpu.VMEM((2, n_slots, GLA_BLOCK, n_lvl * LANES), BF16),
            pltpu.VMEM((2, n_slots, GLA_BLOCK, 2 * LANES), BF16),
            pltpu.VMEM((2, n_slots, GLA_BLOCK, LANES), F32),
            pltpu.VMEM((2, n_slots, 8, LANES), F32),
            pltpu.VMEM((t_lat, LANES), F32), pltpu.VMEM((t_lat, LANES), F32),
        ],
        compiler_params=_cparams("arbitrary", "arbitrary"),
        name="gla",
    )(qi_lat, qi_lat, lf_lat, lf_lat, vg_lat, qi_ctx, qi_ctx, lf_ctx, lf_ctx, vg_ctx, nw, stack, masks)


def _top2_gates(logits):
    lane = lax.broadcasted_iota(jnp.int32, logits.shape, 1).astype(F32)
    big = float(LANES)
    m1 = jnp.max(logits, axis=-1, keepdims=True)
    i1 = jnp.min(jnp.where(logits == m1, lane, big), axis=-1, keepdims=True)
    rest = jnp.where(lane == i1, -jnp.inf, logits)
    m2 = jnp.max(rest, axis=-1, keepdims=True)
    i2 = jnp.min(jnp.where(rest == m2, lane, big), axis=-1, keepdims=True)
    e = jnp.exp(m2 - m1)
    w1 = 1.0 / (1.0 + e)
    return jnp.where(lane == i1, w1, 0.0) + jnp.where(lane == i2, e * w1, 0.0)


def _outproj_kernel(ya_ref, yr_ref, g_ref, x_ref, mod_ref, wua_ref, wur_ref, wo_ref, nw2_ref, *refs,
                    router, tiles_per_block):
    if router:
        rw_ref, tri_ref, xo_ref, h2a_ref, h2b_ref, route_ref, cnt_ref, carry_scr = refs
    else:
        xo_ref, h2_ref = refs
    d = x_ref.shape[1]
    ua = jnp.dot(ya_ref[...], wua_ref[...], preferred_element_type=F32)
    ur = jnp.dot(yr_ref[...], wur_ref[...], preferred_element_type=F32)
    u = g_ref[:, :d].astype(F32) * ua + g_ref[:, d:].astype(F32) * ur
    y = jnp.dot(u.astype(BF16), wo_ref[...], preferred_element_type=F32)
    xn = x_ref[...] + mod_ref[2:3, :] * y
    xo_ref[...] = xn
    h2 = (_rms(xn) * nw2_ref[...]) * (1.0 + mod_ref[4:5, :]) + mod_ref[3:4, :]
    h2_hi = h2.astype(BF16)
    if not router:
        h2_ref[...] = h2_hi
    if router:
        h2a_ref[...], h2b_ref[...] = _pack_row(h2)
        h2_lo = (h2 - h2_hi.astype(F32)).astype(BF16)
        logits = (jnp.dot(h2_hi, rw_ref[0], preferred_element_type=F32)
                  + jnp.dot(h2_lo, rw_ref[0], preferred_element_type=F32)
                  + jnp.dot(h2_hi, rw_ref[1], preferred_element_type=F32))
        lane = lax.broadcasted_iota(jnp.int32, logits.shape, 1)
        gates = _top2_gates(jnp.where(lane < N_EXPERTS, logits, -jnp.inf))

        @pl.when(pl.program_id(0) % tiles_per_block == 0)
        def _():
            carry_scr[...] = jnp.zeros_like(carry_scr)

        sel = gates > 0.0
        sel_f = jnp.where(sel, 1.0, 0.0)
        rank = jnp.dot(tri_ref[...], sel_f.astype(BF16), preferred_element_type=F32) + carry_scr[...]
        total = carry_scr[...] + jnp.sum(sel_f, axis=0, keepdims=True)
        carry_scr[...] = total
        cnt_ref[...] = total

        lanef = lane.astype(F32)
        e_a = jnp.min(jnp.where(sel, lanef, float(LANES)), axis=-1, keepdims=True)
        e_b = jnp.max(jnp.where(sel, lanef, -1.0), axis=-1, keepdims=True)
        is_a = lanef == e_a
        is_b = lanef == e_b

        def pick(mask, v):
            return jnp.sum(jnp.where(mask, v, 0.0), axis=-1, keepdims=True)

        fields = (pick(is_a, rank), pick(is_b, rank), e_a, e_b, pick(is_a, gates),
                  jnp.where(e_b == e_a, 0.0, pick(is_b, gates)))
        route = jnp.zeros_like(gates)
        for k, v in enumerate(fields):
            route = jnp.where(lane == k, v, route)
        route_ref[...] = route


def _outproj_call(ya, yr, g, x, mod, wua, wur, wo, nw2, rw, *, tm, rows_per_mod, route_block=None):
    n, d = x.shape
    assert n % tm == 0, (n, tm)
    router = rw is not None
    row = lambda i: (i, 0)
    const = lambda i: (0, 0)
    tiles_per_block = route_block // tm if router else 1
    if rows_per_mod is None:
        mod_map = lambda i: (mod.shape[0] - 1, 0, 0)
    else:
        mod_map = lambda i: (i // (rows_per_mod // tm), 0, 0)
    in_specs = [
        pl.BlockSpec((tm, ya.shape[1]), row),
        pl.BlockSpec((tm, yr.shape[1]), row),
        pl.BlockSpec((tm, 2 * d), row),
        pl.BlockSpec((tm, d), row),
        pl.BlockSpec((None, 6, d), mod_map),
        pl.BlockSpec(wua.shape, const),
        pl.BlockSpec(wur.shape, const),
        pl.BlockSpec(wo.shape, const),
        pl.BlockSpec((1, d), const),
    ]
    args = [ya, yr, g, x, mod, wua, wur, wo, nw2]
    out_specs = [pl.BlockSpec((tm, d), row)]
    out_shape = [jax.ShapeDtypeStruct((n, d), F32)]
    scratch = []
    if router:
        tri = jnp.asarray(np.tril(np.ones((tm, tm), np.float32), -1), BF16)
        in_specs += [pl.BlockSpec(rw.shape, lambda i: (0, 0, 0)), pl.BlockSpec(tri.shape, const)]
        args += [rw, tri]
        out_specs += [pl.BlockSpec((tm, d // 4), row), pl.BlockSpec((tm, d // 4), row),
                      pl.BlockSpec((tm, LANES), row),
                      pl.BlockSpec((None, 1, LANES), lambda i: (i // tiles_per_block, 0, 0))]
        out_shape += [jax.ShapeDtypeStruct((n, d // 4), jnp.uint32), jax.ShapeDtypeStruct((n, d // 4), jnp.uint32),
                      jax.ShapeDtypeStruct((n, LANES), F32),
                      jax.ShapeDtypeStruct((n // route_block, 1, LANES), F32)]
        scratch = [pltpu.VMEM((1, LANES), F32)]
    else:
        out_specs.append(pl.BlockSpec((tm, d), row))
        out_shape.append(jax.ShapeDtypeStruct((n, d), BF16))
    return pl.pallas_call(
        functools.partial(_outproj_kernel, router=router, tiles_per_block=tiles_per_block),
        grid=(n // tm,),
        in_specs=in_specs,
        out_specs=out_specs,
        out_shape=out_shape,
        scratch_shapes=scratch,
        compiler_params=_cparams("arbitrary"),
        name="outproj_router" if router else "outproj",
    )(*args)


def _swiglu_chunk(xs, w1_ref, w3_ref, w2_ref):
    a = jnp.dot(xs, w1_ref[...].astype(BF16), preferred_element_type=F32)
    b = jnp.dot(xs, w3_ref[...].astype(BF16), preferred_element_type=F32)
    return jnp.dot((_silu(a) * b).astype(BF16), w2_ref[...].astype(BF16), preferred_element_type=F32)


def _ffn_kernel(h_ref, x_ref, mod_ref, w1_ref, w3_ref, w2_ref, o_ref, *, tf):
    h = h_ref[...]
    ff = w1_ref.shape[1]
    acc = jnp.zeros(o_ref.shape, F32)
    for lo in range(0, ff, tf):
        hi = min(lo + tf, ff)
        a = jnp.dot(h, w1_ref[:, lo:hi], preferred_element_type=F32)
        b = jnp.dot(h, w3_ref[:, lo:hi], preferred_element_type=F32)
        acc = acc + jnp.dot((_silu(a) * b).astype(BF16), w2_ref[lo:hi, :], preferred_element_type=F32)
    o_ref[...] = x_ref[...] + mod_ref[5:6, :] * acc


def _ffn_call(h, x, mod, w1, w3, w2, *, tm, tf, rows_per_mod):
    n, d = x.shape
    assert n % tm == 0, (n, tm)
    row = lambda i: (i, 0)
    const = lambda i: (0, 0)
    if rows_per_mod is None:
        mod_map = lambda i: (mod.shape[0] - 1, 0, 0)
    else:
        mod_map = lambda i: (i // (rows_per_mod // tm), 0, 0)
    return pl.pallas_call(
        functools.partial(_ffn_kernel, tf=tf),
        grid=(n // tm,),
        in_specs=[
            pl.BlockSpec((tm, d), row), pl.BlockSpec((tm, d), row), pl.BlockSpec((None, 6, d), mod_map),
            pl.BlockSpec(w1.shape, const), pl.BlockSpec(w3.shape, const), pl.BlockSpec(w2.shape, const),
        ],
        out_specs=pl.BlockSpec((tm, d), row),
        out_shape=jax.ShapeDtypeStruct((n, d), F32),
        compiler_params=_cparams("arbitrary"),
        name="ffn",
    )(h, x, mod, w1, w3, w2)


def _pack_halves(x):
    n = x.shape[1] // 2
    lo = lax.bitcast_convert_type(x[:, :n].astype(BF16).astype(F32), jnp.uint32)
    hi = lax.bitcast_convert_type(x[:, n:].astype(BF16).astype(F32), jnp.uint32)
    return (hi & jnp.uint32(0xFFFF0000)) | (lo >> 16)


def _unpack_halves(w):
    lo = lax.bitcast_convert_type(w << 16, F32)
    hi = lax.bitcast_convert_type(w & jnp.uint32(0xFFFF0000), F32)
    return jnp.concatenate([lo, hi], axis=1)


def _sc_gather_rows(tables, idxs, *, window):
    n_jobs = len(tables)
    mesh = plsc.VectorSubcoreMesh(core_axis_name="c", subcore_axis_name="s")
    out_type = [jax.ShapeDtypeStruct((idx.shape[0], t.shape[1]), t.dtype) for t, idx in zip(tables, idxs)]
    for idx in idxs:
        assert idx.shape[0] % window == 0, (idx.shape, window)

    @pl.kernel(out_type=out_type, mesh=mesh)
    def gather(*refs):
        for j in range(n_jobs):
            table_hbm, idx_hbm, out_hbm = refs[j], refs[n_jobs + j], refs[2 * n_jobs + j]

            def body(idx_vmem, out_vmem, table_hbm=table_hbm):
                pltpu.sync_copy(table_hbm.at[idx_vmem.at[0]], out_vmem)

            pltpu.emit_pipeline(
                body,
                grid=(idx_hbm.shape[1] // window,),
                in_specs=[pl.BlockSpec((1, window), lambda i: (0, i))],
                out_specs=[pl.BlockSpec((window, table_hbm.shape[1]), lambda i: (i, 0))],
                core_axis_name=("c", "s"),
                dimension_semantics=(pltpu.PARALLEL,),
            )(idx_hbm, out_hbm)

    return gather(*tables, *[idx.reshape(1, -1) for idx in idxs])


def _packed_row(pair):
    return jnp.concatenate([_unpack_halves(pair[0][...]), _unpack_halves(pair[1][...])], axis=1)


def _pack_row(x):
    half = x.shape[1] // 2
    return _pack_halves(x[:, :half]), _pack_halves(x[:, half:])


def _experts_kernel(tile_expert_ref, n_active_ref, xa_ref, xb_ref, w1_ref, w3_ref, w2_ref, oa_ref, ob_ref,
                    x_scr, acc_ref):
    i, f = pl.program_id(0), pl.program_id(1)

    @pl.when(i < n_active_ref[0])
    def _():
        @pl.when(f == 0)
        def _():
            x_scr[...] = _packed_row((xa_ref, xb_ref)).astype(BF16)
            acc_ref[...] = jnp.zeros_like(acc_ref)

        acc_ref[...] += _swiglu_chunk(x_scr[...], w1_ref, w3_ref, w2_ref)

        @pl.when(f == pl.num_programs(1) - 1)
        def _():
            oa_ref[...], ob_ref[...] = _pack_row(acc_ref[...])


def _experts_call(tile_expert, n_active, xa, xb, w1, w3, w2, *, tm, tf):
    n, dp = xa.shape
    assert n % tm == 0, (n, tm)
    _, d, ff = w1.shape
    rows = pl.BlockSpec((tm, dp), lambda i, f, te, na: (i, 0))
    grid_spec = pltpu.PrefetchScalarGridSpec(
        num_scalar_prefetch=2,
        grid=(n // tm, ff // tf),
        in_specs=[
            rows, rows,
            pl.BlockSpec((None, d, tf), lambda i, f, te, na: (te[i], 0, f)),
            pl.BlockSpec((None, d, tf), lambda i, f, te, na: (te[i], 0, f)),
            pl.BlockSpec((None, tf, d), lambda i, f, te, na: (te[i], f, 0)),
        ],
        out_specs=[rows, rows],
        scratch_shapes=[pltpu.VMEM((tm, d), BF16), pltpu.VMEM((tm, d), F32)],
    )
    return pl.pallas_call(
        _experts_kernel,
        grid_spec=grid_spec,
        out_shape=[jax.ShapeDtypeStruct((n, dp), jnp.uint32)] * 2,
        compiler_params=_cparams("arbitrary", "arbitrary"),
        name="experts",
    )(tile_expert, n_active, xa, xb, w1, w3, w2)


def _combine_kernel(x_ref, yaa_ref, yab_ref, yba_ref, ybb_ref, route_ref, mod_ref, fnw_ref, o_ref):
    y = (route_ref[:, 4:5] * _packed_row((yaa_ref, yab_ref))
         + route_ref[:, 5:6] * _packed_row((yba_ref, ybb_ref)))
    o_ref[...] = _rms(x_ref[...] + mod_ref[5:6, :] * y) * fnw_ref[...]


def _combine_call(x, ys, route, mod, fnw, *, tm, rows_per_mod):
    n, d = x.shape
    assert n % tm == 0, (n, tm)
    row = lambda i: (i, 0)
    return pl.pallas_call(
        _combine_kernel,
        grid=(n // tm,),
        in_specs=[pl.BlockSpec((tm, d), row)] + [pl.BlockSpec((tm, d // 4), row)] * 4 + [
            pl.BlockSpec((tm, LANES), row),
            pl.BlockSpec((None, 6, d), lambda i: (i // (rows_per_mod // tm), 0, 0)),
            pl.BlockSpec((1, d), lambda i: (0, 0))],
        out_specs=pl.BlockSpec((tm, d), row),
        out_shape=jax.ShapeDtypeStruct((n, d), F32),
        compiler_params=_cparams("arbitrary"),
        name="combine",
    )(x, *ys, route, mod, fnw)


def _dispatch_plan(route, counts, *, tm):
    n = route.shape[0]
    n_e = counts.shape[0]
    n_slots = TOP_K * n + n_e * tm
    seg = (counts + tm - 1) // tm * tm
    ends = jnp.cumsum(seg)
    offs = ends - seg
    rec = route[:, :4].astype(jnp.int32)
    slot_a = jnp.take(offs, rec[:, 2]) + rec[:, 0]
    slot_b = jnp.take(offs, rec[:, 3]) + rec[:, 1]
    n_tiles = n_slots // tm
    n_active = (ends[-1] // tm).astype(jnp.int32)
    tile_start = jnp.arange(n_tiles, dtype=jnp.int32) * tm
    tile_start = jnp.minimum(tile_start, jnp.maximum(ends[-1] - tm, 0))
    tile_expert = jnp.minimum(jnp.sum(tile_start[:, None] >= ends[None, :], axis=1).astype(jnp.int32), n_e - 1)

    tok = jnp.arange(n, dtype=jnp.int32)
    key_b = jnp.where(rec[:, 3] == rec[:, 2], n_slots + tok, slot_b)
    _, by_slot = lax.sort_key_val(jnp.concatenate([slot_a, key_b]), jnp.concatenate([tok, tok]))
    e_of_slot = jnp.repeat(tile_expert, tm)
    packed_before = jnp.cumsum(counts) - counts
    src = jnp.arange(n_slots, dtype=jnp.int32) - jnp.take(offs, e_of_slot) + jnp.take(packed_before, e_of_slot)
    token_of_slot = jnp.take(by_slot, jnp.clip(src, 0, TOP_K * n - 1))
    return token_of_slot, slot_a, slot_b, tile_expert, n_active.reshape(1)


def _rope_tables(t_lat):
    rows = t_lat // GRID_W
    row = jnp.repeat(jnp.arange(rows, dtype=F32), GRID_W)
    col = jnp.tile(jnp.arange(GRID_W, dtype=F32), rows)
    n_freq = ATT_QK_DIM // 4
    inv_freq = ROPE_THETA ** (-jnp.arange(n_freq, dtype=F32) / n_freq)
    ang = jnp.concatenate([row[:, None] * inv_freq, col[:, None] * inv_freq], axis=-1)
    cos, sin = jnp.cos(ang), jnp.sin(ang)
    return (jnp.concatenate([cos, cos, cos, cos], axis=-1), jnp.concatenate([-sin, -sin, sin, sin], axis=-1))


def _layer_lower_bounds(lb_param):
    cs = jnp.cumsum(jax.nn.softmax(lb_param.astype(F32), axis=0), axis=0)
    return cs - cs[0:1]


def _win_columns(w_in_l):
    c = [w_in_l[:, i * 512:(i + 1) * 512] for i in range(8)]
    half = ATT_QK_DIM // 2

    def pair_halves(w):
        w = w.reshape(w.shape[0], ATT_HEADS, 2, 2, half)
        return w.transpose(0, 1, 3, 2, 4).reshape(w.shape[0], -1)

    return jnp.concatenate([pair_halves(c[0]), pair_halves(c[1]), c[2], c[7], c[3], c[4], c[5], c[6],
                            w_in_l[:, 4096:]], axis=1).astype(BF16)


def _pad_ff(w, axis, mult):
    ff = w.shape[axis]
    pad = (-ff) % mult
    if pad == 0:
        return w
    widths = [(0, 0)] * w.ndim
    widths[axis] = (0, pad)
    return jnp.pad(w, widths)


def kernel(x, c, ctx, c_ctx, w_ada, b_ada, norm_mix_w, norm_ffn_w, w_in, lambda_q1, lambda_k1, lambda_q2,
           lambda_k2, att_norm_w, rec_norm_w, lb_fwd, lb_bwd, w_up_att, w_up_rec, w_out, ffn_w1, ffn_w3,
           ffn_w2, router_w, moe_w1, moe_w3, moe_w2, final_norm_w):
    batch, t_lat, d = x.shape
    t_ctx = ctx.shape[1]
    depth = w_ada.shape[0]
    n_lat, n_ctx = batch * t_lat, batch * t_ctx
    tm = 512
    tm_in = 512
    tq = min(1024, t_lat)

    xl = x.reshape(n_lat, d)
    xc = ctx.reshape(n_ctx, d)

    pad_rows = (-(batch + 1)) % 8
    cc = jnp.concatenate([c, jnp.zeros((pad_rows, d), F32), c_ctx[None, :]], axis=0)
    mod_all = _mod_call(cc, w_ada, b_ada).reshape(depth, cc.shape[0], 6, d)

    cos, sin = _rope_tables(t_lat)
    lbs_f = _layer_lower_bounds(lb_fwd)
    lbs_b = _layer_lower_bounds(lb_bwd)

    for l in range(depth):
        last = l == depth - 1
        mod = mod_all[l]
        lam_init = 0.8 - 0.6 * math.exp(-0.3 * l)
        lam = (jnp.exp(jnp.sum(lambda_q1[l] * lambda_k1[l])) - jnp.exp(jnp.sum(lambda_q2[l] * lambda_k2[l]))
               + lam_init).reshape(1).astype(F32)
        lb = jnp.concatenate([lbs_f[l], lbs_b[l]])[None, :]
        w_l = _win_columns(w_in[l])
        nw = norm_mix_w[l][None, :]

        pl_lat = _inproj_call(xl, mod, nw, w_l, cos, sin, lb, tm=tm_in, rows_per_mod=t_lat, rope=True)
        pl_ctx = _inproj_call(xc, mod, nw, w_l, cos, sin, lb, tm=min(tm_in, n_ctx), rows_per_mod=None,
                              rope=False)
        qk_l, vg_l, qi_l, lf_l, g_l = pl_lat
        qk_c, vg_c, qi_c, lf_c, g_c = pl_ctx

        anw = att_norm_w[l][None, :]
        ya_l = _attn_call(lam, qk_l, qk_l, vg_l, qk_c, vg_c, anw, batch=batch, tq=tq,
                          lam_init=lam_init, with_lat=True)
        yr_l, yr_c = _gla_call(qi_l, lf_l, vg_l, qi_c, lf_c, vg_c, rec_norm_w[l][None, :], batch=batch)

        wua = w_up_att[l].astype(BF16)
        wur = w_up_rec[l].astype(BF16)
        wo = w_out[l].astype(BF16)
        nw2 = norm_ffn_w[l][None, :]
        moe_layer = l % 2 == 1
        j = l // 2
        assert moe_layer == last
        rw = None
        if moe_layer:
            rw32 = jnp.pad(router_w[j], ((0, 0), (0, LANES - N_EXPERTS)))
            rw_hi = rw32.astype(BF16)
            rw = jnp.stack([rw_hi, (rw32 - rw_hi.astype(F32)).astype(BF16)])
        res = _outproj_call(ya_l, yr_l, g_l, xl, mod, wua, wur, wo, nw2, rw, tm=tm, rows_per_mod=t_lat,
                            route_block=n_lat)
        xl = res[0]
        if not last:
            ya_c = _attn_call(lam, qk_c, None, None, qk_c, vg_c, anw, batch=batch, tq=t_ctx,
                              lam_init=lam_init, with_lat=False)
            xc, h2_c = _outproj_call(ya_c, yr_c, g_c, xc, mod, wua, wur, wo, nw2, None, tm=tm, rows_per_mod=None)

        if moe_layer:
            h2a, h2b, route, cnt = res[1:]
            counts = cnt[0, 0, :N_EXPERTS].astype(jnp.int32)
            mt = min(MOE_TILE, n_lat)
            token_of_slot, slot_a, slot_b, tile_expert, n_active = _dispatch_plan(route, counts, tm=mt)
            xa, xb = _sc_gather_rows([h2a, h2b], [token_of_slot] * 2, window=SC_WINDOW)
            oa, ob = _experts_call(tile_expert, n_active, xa, xb, moe_w1[j], moe_w3[j], moe_w2[j], tm=mt, tf=512)
            ys = _sc_gather_rows([oa, ob, oa, ob], [slot_a, slot_a, slot_b, slot_b], window=SC_WINDOW)
            xl = _combine_call(xl, ys, route, mod, final_norm_w[None, :], tm=tm, rows_per_mod=t_lat)
        else:
            h2_l = res[1]
            w1 = _pad_ff(ffn_w1[j], 1, 2 * LANES).astype(BF16)
            w3 = _pad_ff(ffn_w3[j], 1, 2 * LANES).astype(BF16)
            w2 = _pad_ff(ffn_w2[j], 0, 2 * LANES).astype(BF16)
            tf = 512
            xl = _ffn_call(h2_l, xl, mod, w1, w3, w2, tm=tm, tf=tf, rows_per_mod=t_lat)
            xc = _ffn_call(h2_c, xc, mod, w1, w3, w2, tm=tm, tf=tf, rows_per_mod=None)

    return xl.reshape(batch, t_lat, d)
```

```python
import functools
import math

import numpy as np
import jax
import jax.numpy as jnp
from jax import lax
from jax.experimental import pallas as pl
from jax.experimental.pallas import tpu as pltpu
from jax.experimental.pallas import tpu_sc as plsc

F32 = jnp.float32
BF16 = jnp.bfloat16
HIGHEST = lax.Precision.HIGHEST

EPS = 1e-6
GRID_W = 64
ROPE_THETA = 10000.0
ATT_HEADS = 4
ATT_QK_DIM = 64
REC_HEADS = 4
N_EXPERTS = 8
TOP_K = 2
Q_SCALE = ATT_QK_DIM ** -0.5 * math.log2(math.e)

LANES = 128
GLA_BLOCK = 256
GLA_CHUNK = 256
GLA_UNROLL = 4
GLA_LEVELS = tuple(2 ** i for i in range(GLA_CHUNK.bit_length() - 1))
VMEM_LIMIT = 56 * 1024 * 1024
MOE_TILE = 1024
SC_WINDOW = 128
ATT_KEYS = 256
ATT_ROWS = 512

NT_DIMS = (((1,), (1,)), ((), ()))
TN_DIMS = (((0,), (0,)), ((), ()))


def _cparams(*sem):
    return pltpu.CompilerParams(dimension_semantics=sem, vmem_limit_bytes=VMEM_LIMIT)


def _silu(a):
    return a * jax.nn.sigmoid(a)


def _rms(x):
    return x * lax.rsqrt(jnp.mean(x * x, axis=-1, keepdims=True) + EPS)


def _mod_kernel(c_ref, w_ref, b_ref, o_ref):
    s = _silu(c_ref[...])
    o_ref[...] = jnp.dot(s, w_ref[...], precision=HIGHEST, preferred_element_type=F32) + b_ref[...]


def _mod_call(cc, w_ada, b_ada):
    depth, d, n = w_ada.shape
    rows = cc.shape[0]
    tn = 1536
    return pl.pallas_call(
        _mod_kernel,
        grid=(depth, n // tn),
        in_specs=[
            pl.BlockSpec((rows, d), lambda l, j: (0, 0)),
            pl.BlockSpec((None, d, tn), lambda l, j: (l, 0, j)),
            pl.BlockSpec((None, 1, tn), lambda l, j: (l, 0, j)),
        ],
        out_specs=pl.BlockSpec((None, rows, tn), lambda l, j: (l, 0, j)),
        out_shape=jax.ShapeDtypeStruct((depth, rows, n), F32),
        compiler_params=_cparams("arbitrary", "arbitrary"),
        name="mod",
    )(cc, w_ada, b_ada.reshape(depth, 1, n))


def _log_forget(z, lb):
    t = jnp.exp(-jnp.abs(z))
    num = jnp.where(z >= 0.0, 1.0 + lb * t, lb + t)
    return jnp.where(num > 0.0, jnp.log(num / (1.0 + t)), z)


def _inproj_kernel(x_ref, mod_ref, nw_ref, w_ref, cos_ref, sin_ref, lb_ref,
                   qk_ref, vg_ref, qi_ref, lf_ref, g_ref, *, rope):
    h = _rms(x_ref[...]) * nw_ref[...]
    h = (h * (1.0 + mod_ref[1:2, :]) + mod_ref[0:1, :]).astype(BF16)
    ts = qk_ref.shape[1]
    half = ts // 2

    def proj(s):
        return jnp.dot(h, w_ref[:, s * ts:(s + 1) * ts], preferred_element_type=F32)

    p = proj(0)
    if rope:
        cos = cos_ref[...]
        sin = sin_ref[...]
    for j in range(ts // LANES):
        blk = p[:, j * LANES:(j + 1) * LANES]
        if rope:
            blk = blk * cos + pltpu.roll(blk, LANES // 2, 1) * sin
        if j * LANES < half:
            blk = blk * Q_SCALE
        qk_ref[:, j * LANES:(j + 1) * LANES] = blk.astype(BF16)

    p = proj(1)
    vg_ref[:, :half] = p[:, :half].astype(BF16)
    vg_ref[:, half:] = _silu(p[:, half:]).astype(BF16)

    p = proj(2)
    qi_ref[:, :half] = _silu(p[:, :half]).astype(BF16)
    qi_ref[:, half:] = p[:, half:].astype(BF16)

    lf_ref[...] = _log_forget(proj(3), lb_ref[...])

    g_ref[:, :ts] = jax.nn.sigmoid(proj(4)).astype(BF16)
    g_ref[:, ts:] = jax.nn.sigmoid(proj(5)).astype(BF16)


def _inproj_call(x, mod, nw, w, cos, sin, lb, *, tm, rows_per_mod, rope):
    n, d = x.shape
    assert n % tm == 0, (n, tm)
    ts = 1024
    assert w.shape[1] == 6 * ts
    pos_tiles = cos.shape[0] // tm
    row = lambda i: (i, 0)
    const = lambda i: (0, 0)
    if rows_per_mod is None:
        mod_map = lambda i: (mod.shape[0] - 1, 0, 0)
    else:
        mod_map = lambda i: (i // (rows_per_mod // tm), 0, 0)
    outs = pl.pallas_call(
        functools.partial(_inproj_kernel, rope=rope),
        grid=(n // tm,),
        in_specs=[
            pl.BlockSpec((tm, d), row),
            pl.BlockSpec((None, 6, d), mod_map),
            pl.BlockSpec((1, d), const),
            pl.BlockSpec(w.shape, const),
            pl.BlockSpec((tm, LANES), lambda i: (i % pos_tiles, 0)),
            pl.BlockSpec((tm, LANES), lambda i: (i % pos_tiles, 0)),
            pl.BlockSpec((1, ts), const),
        ],
        out_specs=[
            pl.BlockSpec((tm, ts), row),
            pl.BlockSpec((tm, ts), row),
            pl.BlockSpec((tm, ts), row),
            pl.BlockSpec((tm, ts), row),
            pl.BlockSpec((tm, 2 * ts), row),
        ],
        out_shape=[
            jax.ShapeDtypeStruct((n, ts), BF16),
            jax.ShapeDtypeStruct((n, ts), BF16),
            jax.ShapeDtypeStruct((n, ts), BF16),
            jax.ShapeDtypeStruct((n, ts), F32),
            jax.ShapeDtypeStruct((n, 2 * ts), BF16),
        ],
        compiler_params=_cparams("arbitrary"),
        name="inproj_rope" if rope else "inproj",
    )(x, mod, nw, w, cos, sin, lb)
    return outs


def _attn_kernel(lam_ref, q_ref, *refs, post_scale, with_lat):
    if with_lat:
        kl_ref, vl_ref, kc_ref, vc_ref, nw_ref, o_ref, vce_scr, vle_scr = refs
    else:
        kc_ref, vc_ref, nw_ref, o_ref, vce_scr = refs

    @pl.when(pl.program_id(2) == 0)
    def _():
        vce_scr[:, :LANES] = vc_ref[...]
        vce_scr[:, LANES:] = jnp.ones(vc_ref.shape, BF16)
        if with_lat:
            vle_scr[:, :LANES] = vl_ref[...]
            vle_scr[:, LANES:] = jnp.ones(vl_ref.shape, BF16)

    n_parts = max(1, q_ref.shape[0] // ATT_ROWS)
    rows = q_ref.shape[0] // n_parts
    qq = []
    for part in range(n_parts):
        q = q_ref[part * rows:(part + 1) * rows, :]
        lane = lax.broadcasted_iota(jnp.int32, q.shape, 1)
        zero = jnp.zeros_like(q)
        sub1 = (lane % ATT_QK_DIM) < (ATT_QK_DIM // 2)
        qq.append(jnp.concatenate([jnp.where(sub1, q, zero), jnp.where(sub1, zero, q)], axis=0))

    blocks = [(kc_ref, vce_scr, 0, kc_ref.shape[0])]
    if with_lat:
        blocks += [(kl_ref, vle_scr, j, ATT_KEYS) for j in range(0, kl_ref.shape[0], ATT_KEYS)]
    m = [jnp.full((2 * rows, 1), -jnp.inf, F32)] * n_parts
    acc = [jnp.zeros((2 * rows, 2 * LANES), F32)] * n_parts
    for k_ref, v_scr, start, size in blocks:
        for part in range(n_parts):
            s = lax.dot_general(qq[part], k_ref[start:start + size, :], NT_DIMS, preferred_element_type=F32)
            m_new = jnp.maximum(m[part], jnp.max(s, axis=-1, keepdims=True))
            p = jnp.exp2(s - m_new).astype(BF16)
            acc[part] = acc[part] * jnp.exp2(m[part] - m_new) + jnp.dot(
                p, v_scr[start:start + size, :], preferred_element_type=F32)
            m[part] = m_new
    for part in range(n_parts):
        on = acc[part][:, :LANES] * (1.0 / acc[part][:, LANES:])
        o = on[:rows] - lam_ref[0] * on[rows:]
        o_ref[part * rows:(part + 1) * rows, :] = (_rms(o) * nw_ref[...] * post_scale).astype(BF16)


def _attn_call(lam, q_src, qk_lat, vg_lat, qk_ctx, vg_ctx, nw, *, batch, tq, lam_init, with_lat):
    n_q = q_src.shape[0]
    t_q = n_q // batch
    t_lat = qk_lat.shape[0] // batch if with_lat else 0
    t_ctx = qk_ctx.shape[0] // batch
    nq_tiles = t_q // tq
    h = ATT_HEADS
    in_specs = [
        pl.BlockSpec(memory_space=pltpu.SMEM),
        pl.BlockSpec((tq, LANES), lambda b, hh, i: (b * nq_tiles + i, hh)),
    ]
    args = [lam, q_src]
    if with_lat:
        in_specs += [
            pl.BlockSpec((t_lat, LANES), lambda b, hh, i: (b, h + hh)),
            pl.BlockSpec((t_lat, LANES), lambda b, hh, i: (b, hh)),
        ]
        args += [qk_lat, vg_lat]
    in_specs += [
        pl.BlockSpec((t_ctx, LANES), lambda b, hh, i: (b, h + hh)),
        pl.BlockSpec((t_ctx, LANES), lambda b, hh, i: (b, hh)),
        pl.BlockSpec((1, LANES), lambda b, hh, i: (0, 0)),
    ]
    args += [qk_ctx, vg_ctx, nw]
    return pl.pallas_call(
        functools.partial(_attn_kernel, post_scale=1.0 - lam_init, with_lat=with_lat),
        grid=(batch, h, nq_tiles),
        in_specs=in_specs,
        out_specs=pl.BlockSpec((tq, LANES), lambda b, hh, i: (b * nq_tiles + i, hh)),
        out_shape=jax.ShapeDtypeStruct((n_q, h * LANES), BF16),
        scratch_shapes=[pltpu.VMEM((t_ctx, 2 * LANES), BF16)]
        + ([pltpu.VMEM((t_lat, 2 * LANES), BF16)] if with_lat else []),
        compiler_params=_cparams("arbitrary", "arbitrary", "arbitrary"),
        name="attn_lat" if with_lat else "attn_ctx",
    )(*args)


def _gla_constants():
    n = GLA_BLOCK
    idx = np.arange(n)
    same_chunk = (idx[:, None] // GLA_CHUNK) == (idx[None, :] // GLA_CHUNK)
    stacks, masks = [], []
    for reverse in (False, True):
        order = (idx[None, :] >= idx[:, None]) if reverse else (idx[None, :] <= idx[:, None])
        cum = (same_chunk & order).astype(np.int32)
        groups, lvl_masks = [cum], []
        for m in GLA_LEVELS:
            ref = (idx // (2 * m)) * (2 * m) + (m if reverse else m - 1)
            groups.append(np.abs(cum - cum[ref]))
            upper = (idx % (2 * m)) >= m
            same = (idx[:, None] // (2 * m)) == (idx[None, :] // (2 * m))
            q_side, k_side = (~upper, upper) if reverse else (upper, ~upper)
            lvl_masks.append(same & q_side[:, None] & k_side[None, :])
        end = (idx // GLA_CHUNK) * GLA_CHUNK + (0 if reverse else GLA_CHUNK - 1)
        groups.append(np.abs(cum[end] - cum))
        stacks.append(np.concatenate(groups, axis=0))
        masks.append(np.stack(lvl_masks))
    return jnp.asarray(np.stack(stacks), BF16), jnp.asarray(np.stack(masks), F32)


def _gla_prepare(q, v, lf, stack, *, reverse):
    n = q.shape[0]
    qf = q.astype(F32)
    k = 1.0 - jnp.exp(lf)

    hi = lf.astype(BF16)
    mid = (lf - hi.astype(F32)).astype(BF16)
    expo = jnp.dot(stack, jnp.concatenate([hi, mid], axis=1), preferred_element_type=F32)

    def exponent(group):
        blk = expo[group * n:(group + 1) * n]
        return blk[:, :LANES] + blk[:, LANES:]

    scale = [jnp.exp(exponent(1 + li)) for li in range(len(GLA_LEVELS))]
    qs = jnp.concatenate([(qf * e).astype(BF16) for e in scale], axis=1)
    ks = jnp.concatenate([(k * e).astype(BF16) for e in scale], axis=1)
    c = exponent(0)
    io = jnp.concatenate([(qf * jnp.exp(c)).astype(BF16),
                          (k * jnp.exp(exponent(1 + len(GLA_LEVELS)))).astype(BF16)], axis=1)
    diag = jnp.sum(qf * k, axis=-1, keepdims=True) * v.astype(F32)
    last = 0 if reverse else GLA_CHUNK - 1
    n_chunks = n // GLA_CHUNK
    decay = jnp.concatenate([jnp.exp(c[ci * GLA_CHUNK + last:ci * GLA_CHUNK + last + 1, :])
                             for ci in range(n_chunks)] + [jnp.ones((8 - n_chunks, LANES), F32)], axis=0)
    return qs, ks, io, diag, decay


def _gla_scan(qs, ks, io, diag, decay, v, st, masks, *, reverse):
    n = v.shape[0]
    scores = jnp.zeros((n, n), F32)
    for li in range(len(GLA_LEVELS)):
        cols = slice(li * LANES, (li + 1) * LANES)
        s_l = lax.dot_general(qs[:, cols], ks[:, cols], NT_DIMS, preferred_element_type=F32)
        scores = scores + s_l * masks[li]
    o = jnp.dot(scores.astype(BF16), v, preferred_element_type=F32) + diag

    n_chunks = n // GLA_CHUNK
    outs = [None] * n_chunks
    order = range(n_chunks - 1, -1, -1) if reverse else range(n_chunks)
    for ci in order:
        r = slice(ci * GLA_CHUNK, (ci + 1) * GLA_CHUNK)
        outs[ci] = lax.dot_general(io[r, :LANES], st.astype(BF16), NT_DIMS, preferred_element_type=F32)
        upd = lax.dot_general(v[r], io[r, LANES:], TN_DIMS, preferred_element_type=F32)
        st = st * decay[ci:ci + 1, :] + upd
    return o + jnp.concatenate(outs, axis=0), st


def _gla_kernel(ql_ref, il_ref, ffl_ref, fbl_ref, gl_ref, qc_ref, ic_ref, ffc_ref, fbc_ref, gc_ref,
                nw_ref, stack_ref, mask_ref, ol_ref, oc_ref,
                qs_scr, ks_scr, io_scr, dg_scr, dc_scr, of_scr, ob_scr):
    n_blocks = ql_ref.shape[0] // GLA_BLOCK
    nw = nw_ref[...]

    def finish(o, g):
        return (_rms(o) * nw * g.astype(F32)).astype(BF16)

    def rows(i):
        return pl.ds(pl.multiple_of(i * GLA_BLOCK, GLA_BLOCK), GLA_BLOCK)

    def prepare(slot, q, v, lf_f, lf_b):
        for d, lf in enumerate((lf_f, lf_b)):
            qs, ks, io, diag, decay = _gla_prepare(q, v, lf, stack_ref[d], reverse=bool(d))
            qs_scr[d, slot] = qs
            ks_scr[d, slot] = ks
            io_scr[d, slot] = io
            dg_scr[d, slot] = diag
            dc_scr[d, slot] = decay

    def scan(d, slot, v, st):
        return _gla_scan(qs_scr[d, slot], ks_scr[d, slot], io_scr[d, slot], dg_scr[d, slot], dc_scr[d, slot],
                         v, st, mask_ref[d], reverse=bool(d))

    prepare(0, qc_ref[...], ic_ref[...], ffc_ref[...], fbc_ref[...])

    def prep(i, carry):
        r = rows(i)
        prepare(i + 1, ql_ref[r, :], il_ref[r, :], ffl_ref[r, :], fbl_ref[r, :])
        return carry

    unroll = min(GLA_UNROLL, n_blocks)
    lax.fori_loop(0, n_blocks, prep, 0, unroll=min(2 * GLA_UNROLL, n_blocks))

    zero_state = jnp.zeros((LANES, LANES), F32)
    o_cf, st_f = scan(0, 0, ic_ref[...], zero_state)
    o_cb, st_b = scan(1, 0, ic_ref[...], zero_state)
    oc_ref[...] = finish(o_cf + o_cb, gc_ref[...])

    def step(i, carry):
        st_f, st_b = carry
        j = n_blocks - 1 - i
        o_f, st_f = scan(0, i + 1, il_ref[rows(i), :], st_f)
        o_b, st_b = scan(1, j + 1, il_ref[rows(j), :], st_b)
        of_scr[rows(i), :] = o_f
        ob_scr[rows(j), :] = o_b
        return st_f, st_b

    lax.fori_loop(0, n_blocks, step, (st_f, st_b), unroll=min(2 * GLA_UNROLL, n_blocks))

    def emit(i, carry):
        r = rows(i)
        ol_ref[r, :] = finish(of_scr[r, :] + ob_scr[r, :], gl_ref[r, :])
        return carry

    lax.fori_loop(0, n_blocks, emit, 0, unroll=unroll)


def _gla_call(qi_lat, lf_lat, vg_lat, qi_ctx, lf_ctx, vg_ctx, nw, *, batch):
    t_lat = qi_lat.shape[0] // batch
    t_ctx = qi_ctx.shape[0] // batch
    assert t_ctx == GLA_BLOCK and t_lat % GLA_BLOCK == 0
    h = REC_HEADS
    stack, masks = _gla_constants()
    n_slots = 1 + t_lat // GLA_BLOCK
    n_lvl = len(GLA_LEVELS)

    def col(off):
        return lambda b, hh: (b, off + hh)

    def seq(t):
        return [pl.BlockSpec((t, LANES), col(0)), pl.BlockSpec((t, LANES), col(h)),
                pl.BlockSpec((t, LANES), col(0)), pl.BlockSpec((t, LANES), col(h)),
                pl.BlockSpec((t, LANES), col(h))]

    return pl.pallas_call(
        _gla_kernel,
        grid=(batch, h),
        in_specs=seq(t_lat) + seq(t_ctx) + [
            pl.BlockSpec((1, LANES), lambda b, hh: (0, 0)),
            pl.BlockSpec(stack.shape, lambda b, hh: (0, 0, 0)),
            pl.BlockSpec(masks.shape, lambda b, hh: (0, 0, 0, 0)),
        ],
        out_specs=[pl.BlockSpec((t_lat, LANES), col(0)), pl.BlockSpec((t_ctx, LANES), col(0))],
        out_shape=[jax.ShapeDtypeStruct((batch * t_lat, h * LANES), BF16),
                   jax.ShapeDtypeStruct((batch * t_ctx, h * LANES), BF16)],
        scratch_shapes=[
            pltpu.VMEM((2, n_slots, GLA_BLOCK, n_lvl * LANES), BF16),
            pltpu.VMEM((2, n_slots, GLA_BLOCK, n_lvl * LANES), BF16),
            pltpu.VMEM((2, n_slots, GLA_BLOCK, 2 * LANES), BF16),
            pltpu.VMEM((2, n_slots, GLA_BLOCK, LANES), F32),
            pltpu.VMEM((2, n_slots, 8, LANES), F32),
            pltpu.VMEM((t_lat, LANES), F32), pltpu.VMEM((t_lat, LANES), F32),
        ],
        compiler_params=_cparams("arbitrary", "arbitrary"),
        name="gla",
    )(qi_lat, qi_lat, lf_lat, lf_lat, vg_lat, qi_ctx, qi_ctx, lf_ctx, lf_ctx, vg_ctx, nw, stack, masks)


def _top2_gates(logits):
    lane = lax.broadcasted_iota(jnp.int32, logits.shape, 1).astype(F32)
    big = float(LANES)
    m1 = jnp.max(logits, axis=-1, keepdims=True)
    i1 = jnp.min(jnp.where(logits == m1, lane, big), axis=-1, keepdims=True)
    rest = jnp.where(lane == i1, -jnp.inf, logits)
    m2 = jnp.max(rest, axis=-1, keepdims=True)
    i2 = jnp.min(jnp.where(rest == m2, lane, big), axis=-1, keepdims=True)
    e = jnp.exp(m2 - m1)
    w1 = 1.0 / (1.0 + e)
    return jnp.where(lane == i1, w1, 0.0) + jnp.where(lane == i2, e * w1, 0.0)


def _outproj_kernel(ya_ref, yr_ref, g_ref, x_ref, mod_ref, wua_ref, wur_ref, wo_ref, nw2_ref, *refs,
                    router, tiles_per_block):
    if router:
        rw_ref, tri_ref, xo_ref, h2a_ref, h2b_ref, route_ref, cnt_ref, carry_scr = refs
    else:
        xo_ref, h2_ref = refs
    d = x_ref.shape[1]
    ua = jnp.dot(ya_ref[...], wua_ref[...], preferred_element_type=F32)
    ur = jnp.dot(yr_ref[...], wur_ref[...], preferred_element_type=F32)
    u = g_ref[:, :d].astype(F32) * ua + g_ref[:, d:].astype(F32) * ur
    y = jnp.dot(u.astype(BF16), wo_ref[...], preferred_element_type=F32)
    xn = x_ref[...] + mod_ref[2:3, :] * y
    xo_ref[...] = xn
    h2 = (_rms(xn) * nw2_ref[...]) * (1.0 + mod_ref[4:5, :]) + mod_ref[3:4, :]
    h2_hi = h2.astype(BF16)
    if not router:
        h2_ref[...] = h2_hi
    if router:
        h2a_ref[...], h2b_ref[...] = _pack_row(h2)
        h2_lo = (h2 - h2_hi.astype(F32)).astype(BF16)
        logits = (jnp.dot(h2_hi, rw_ref[0], preferred_element_type=F32)
                  + jnp.dot(h2_lo, rw_ref[0], preferred_element_type=F32)
                  + jnp.dot(h2_hi, rw_ref[1], preferred_element_type=F32))
        lane = lax.broadcasted_iota(jnp.int32, logits.shape, 1)
        gates = _top2_gates(jnp.where(lane < N_EXPERTS, logits, -jnp.inf))

        @pl.when(pl.program_id(0) % tiles_per_block == 0)
        def _():
            carry_scr[...] = jnp.zeros_like(carry_scr)

        sel = gates > 0.0
        sel_f = jnp.where(sel, 1.0, 0.0)
        rank = jnp.dot(tri_ref[...], sel_f.astype(BF16), preferred_element_type=F32) + carry_scr[...]
        total = carry_scr[...] + jnp.sum(sel_f, axis=0, keepdims=True)
        carry_scr[...] = total
        cnt_ref[...] = total

        lanef = lane.astype(F32)
        e_a = jnp.min(jnp.where(sel, lanef, float(LANES)), axis=-1, keepdims=True)
        e_b = jnp.max(jnp.where(sel, lanef, -1.0), axis=-1, keepdims=True)
        is_a = lanef == e_a
        is_b = lanef == e_b

        def pick(mask, v):
            return jnp.sum(jnp.where(mask, v, 0.0), axis=-1, keepdims=True)

        fields = (pick(is_a, rank), pick(is_b, rank), e_a, e_b, pick(is_a, gates),
                  jnp.where(e_b == e_a, 0.0, pick(is_b, gates)))
        route = jnp.zeros_like(gates)
        for k, v in enumerate(fields):
            route = jnp.where(lane == k, v, route)
        route_ref[...] = route


def _outproj_call(ya, yr, g, x, mod, wua, wur, wo, nw2, rw, *, tm, rows_per_mod, route_block=None):
    n, d = x.shape
    assert n % tm == 0, (n, tm)
    router = rw is not None
    row = lambda i: (i, 0)
    const = lambda i: (0, 0)
    tiles_per_block = route_block // tm if router else 1
    if rows_per_mod is None:
        mod_map = lambda i: (mod.shape[0] - 1, 0, 0)
    else:
        mod_map = lambda i: (i // (rows_per_mod // tm), 0, 0)
    in_specs = [
        pl.BlockSpec((tm, ya.shape[1]), row),
        pl.BlockSpec((tm, yr.shape[1]), row),
        pl.BlockSpec((tm, 2 * d), row),
        pl.BlockSpec((tm, d), row),
        pl.BlockSpec((None, 6, d), mod_map),
        pl.BlockSpec(wua.shape, const),
        pl.BlockSpec(wur.shape, const),
        pl.BlockSpec(wo.shape, const),
        pl.BlockSpec((1, d), const),
    ]
    args = [ya, yr, g, x, mod, wua, wur, wo, nw2]
    out_specs = [pl.BlockSpec((tm, d), row)]
    out_shape = [jax.ShapeDtypeStruct((n, d), F32)]
    scratch = []
    if router:
        tri = jnp.asarray(np.tril(np.ones((tm, tm), np.float32), -1), BF16)
        in_specs += [pl.BlockSpec(rw.shape, lambda i: (0, 0, 0)), pl.BlockSpec(tri.shape, const)]
        args += [rw, tri]
        out_specs += [pl.BlockSpec((tm, d // 4), row), pl.BlockSpec((tm, d // 4), row),
                      pl.BlockSpec((tm, LANES), row),
                      pl.BlockSpec((None, 1, LANES), lambda i: (i // tiles_per_block, 0, 0))]
        out_shape += [jax.ShapeDtypeStruct((n, d // 4), jnp.uint32), jax.ShapeDtypeStruct((n, d // 4), jnp.uint32),
                      jax.ShapeDtypeStruct((n, LANES), F32),
                      jax.ShapeDtypeStruct((n // route_block, 1, LANES), F32)]
        scratch = [pltpu.VMEM((1, LANES), F32)]
    else:
        out_specs.append(pl.BlockSpec((tm, d), row))
        out_shape.append(jax.ShapeDtypeStruct((n, d), BF16))
    return pl.pallas_call(
        functools.partial(_outproj_kernel, router=router, tiles_per_block=tiles_per_block),
        grid=(n // tm,),
        in_specs=in_specs,
        out_specs=out_specs,
        out_shape=out_shape,
        scratch_shapes=scratch,
        compiler_params=_cparams("arbitrary"),
        name="outproj_router" if router else "outproj",
    )(*args)


def _swiglu_chunk(xs, w1_ref, w3_ref, w2_ref):
    a = jnp.dot(xs, w1_ref[...].astype(BF16), preferred_element_type=F32)
    b = jnp.dot(xs, w3_ref[...].astype(BF16), preferred_element_type=F32)
    return jnp.dot((_silu(a) * b).astype(BF16), w2_ref[...].astype(BF16), preferred_element_type=F32)


def _ffn_kernel(h_ref, x_ref, mod_ref, w1_ref, w3_ref, w2_ref, o_ref, *, tf):
    h = h_ref[...]
    ff = w1_ref.shape[1]
    acc = jnp.zeros(o_ref.shape, F32)
    for lo in range(0, ff, tf):
        hi = min(lo + tf, ff)
        a = jnp.dot(h, w1_ref[:, lo:hi], preferred_element_type=F32)
        b = jnp.dot(h, w3_ref[:, lo:hi], preferred_element_type=F32)
        acc = acc + jnp.dot((_silu(a) * b).astype(BF16), w2_ref[lo:hi, :], preferred_element_type=F32)
    o_ref[...] = x_ref[...] + mod_ref[5:6, :] * acc


def _ffn_call(h, x, mod, w1, w3, w2, *, tm, tf, rows_per_mod):
    n, d = x.shape
    assert n % tm == 0, (n, tm)
    row = lambda i: (i, 0)
    const = lambda i: (0, 0)
    if rows_per_mod is None:
        mod_map = lambda i: (mod.shape[0] - 1, 0, 0)
    else:
        mod_map = lambda i: (i // (rows_per_mod // tm), 0, 0)
    return pl.pallas_call(
        functools.partial(_ffn_kernel, tf=tf),
        grid=(n // tm,),
        in_specs=[
            pl.BlockSpec((tm, d), row), pl.BlockSpec((tm, d), row), pl.BlockSpec((None, 6, d), mod_map),
            pl.BlockSpec(w1.shape, const), pl.BlockSpec(w3.shape, const), pl.BlockSpec(w2.shape, const),
        ],
        out_specs=pl.BlockSpec((tm, d), row),
        out_shape=jax.ShapeDtypeStruct((n, d), F32),
        compiler_params=_cparams("arbitrary"),
        name="ffn",
    )(h, x, mod, w1, w3, w2)


def _pack_halves(x):
    n = x.shape[1] // 2
    lo = lax.bitcast_convert_type(x[:, :n].astype(BF16).astype(F32), jnp.uint32)
    hi = lax.bitcast_convert_type(x[:, n:].astype(BF16).astype(F32), jnp.uint32)
    return (hi & jnp.uint32(0xFFFF0000)) | (lo >> 16)


def _unpack_halves(w):
    lo = lax.bitcast_convert_type(w << 16, F32)
    hi = lax.bitcast_convert_type(w & jnp.uint32(0xFFFF0000), F32)
    return jnp.concatenate([lo, hi], axis=1)


def _sc_gather_rows(tables, idxs, *, window):
    n_jobs = len(tables)
    mesh = plsc.VectorSubcoreMesh(core_axis_name="c", subcore_axis_name="s")
    out_type = [jax.ShapeDtypeStruct((idx.shape[0], t.shape[1]), t.dtype) for t, idx in zip(tables, idxs)]
    for idx in idxs:
        assert idx.shape[0] % window == 0, (idx.shape, window)

    @pl.kernel(out_type=out_type, mesh=mesh)
    def gather(*refs):
        for j in range(n_jobs):
            table_hbm, idx_hbm, out_hbm = refs[j], refs[n_jobs + j], refs[2 * n_jobs + j]

            def body(idx_vmem, out_vmem, table_hbm=table_hbm):
                pltpu.sync_copy(table_hbm.at[idx_vmem.at[0]], out_vmem)

            pltpu.emit_pipeline(
                body,
                grid=(idx_hbm.shape[1] // window,),
                in_specs=[pl.BlockSpec((1, window), lambda i: (0, i))],
                out_specs=[pl.BlockSpec((window, table_hbm.shape[1]), lambda i: (i, 0))],
                core_axis_name=("c", "s"),
                dimension_semantics=(pltpu.PARALLEL,),
            )(idx_hbm, out_hbm)

    return gather(*tables, *[idx.reshape(1, -1) for idx in idxs])


def _packed_row(pair):
    return jnp.concatenate([_unpack_halves(pair[0][...]), _unpack_halves(pair[1][...])], axis=1)


def _pack_row(x):
    half = x.shape[1] // 2
    return _pack_halves(x[:, :half]), _pack_halves(x[:, half:])


def _experts_kernel(tile_expert_ref, n_active_ref, xa_ref, xb_ref, w1_ref, w3_ref, w2_ref, oa_ref, ob_ref,
                    x_scr, acc_ref):
    i, f = pl.program_id(0), pl.program_id(1)

    @pl.when(i < n_active_ref[0])
    def _():
        @pl.when(f == 0)
        def _():
            x_scr[...] = _packed_row((xa_ref, xb_ref)).astype(BF16)
            acc_ref[...] = jnp.zeros_like(acc_ref)

        acc_ref[...] += _swiglu_chunk(x_scr[...], w1_ref, w3_ref, w2_ref)

        @pl.when(f == pl.num_programs(1) - 1)
        def _():
            oa_ref[...], ob_ref[...] = _pack_row(acc_ref[...])


def _experts_call(tile_expert, n_active, xa, xb, w1, w3, w2, *, tm, tf):
    n, dp = xa.shape
    assert n % tm == 0, (n, tm)
    _, d, ff = w1.shape
    rows = pl.BlockSpec((tm, dp), lambda i, f, te, na: (i, 0))
    grid_spec = pltpu.PrefetchScalarGridSpec(
        num_scalar_prefetch=2,
        grid=(n // tm, ff // tf),
        in_specs=[
            rows, rows,
            pl.BlockSpec((None, d, tf), lambda i, f, te, na: (te[i], 0, f)),
            pl.BlockSpec((None, d, tf), lambda i, f, te, na: (te[i], 0, f)),
            pl.BlockSpec((None, tf, d), lambda i, f, te, na: (te[i], f, 0)),
        ],
        out_specs=[rows, rows],
        scratch_shapes=[pltpu.VMEM((tm, d), BF16), pltpu.VMEM((tm, d), F32)],
    )
    return pl.pallas_call(
        _experts_kernel,
        grid_spec=grid_spec,
        out_shape=[jax.ShapeDtypeStruct((n, dp), jnp.uint32)] * 2,
        compiler_params=_cparams("arbitrary", "arbitrary"),
        name="experts",
    )(tile_expert, n_active, xa, xb, w1, w3, w2)


def _combine_kernel(x_ref, yaa_ref, yab_ref, yba_ref, ybb_ref, route_ref, mod_ref, fnw_ref, o_ref):
    y = (route_ref[:, 4:5] * _packed_row((yaa_ref, yab_ref))
         + route_ref[:, 5:6] * _packed_row((yba_ref, ybb_ref)))
    o_ref[...] = _rms(x_ref[...] + mod_ref[5:6, :] * y) * fnw_ref[...]


def _combine_call(x, ys, route, mod, fnw, *, tm, rows_per_mod):
    n, d = x.shape
    assert n % tm == 0, (n, tm)
    row = lambda i: (i, 0)
    return pl.pallas_call(
        _combine_kernel,
        grid=(n // tm,),
        in_specs=[pl.BlockSpec((tm, d), row)] + [pl.BlockSpec((tm, d // 4), row)] * 4 + [
            pl.BlockSpec((tm, LANES), row),
            pl.BlockSpec((None, 6, d), lambda i: (i // (rows_per_mod // tm), 0, 0)),
            pl.BlockSpec((1, d), lambda i: (0, 0))],
        out_specs=pl.BlockSpec((tm, d), row),
        out_shape=jax.ShapeDtypeStruct((n, d), F32),
        compiler_params=_cparams("arbitrary"),
        name="combine",
    )(x, *ys, route, mod, fnw)


def _dispatch_plan(route, counts, *, tm):
    n = route.shape[0]
    n_e = counts.shape[0]
    n_slots = TOP_K * n + n_e * tm
    seg = (counts + tm - 1) // tm * tm
    ends = jnp.cumsum(seg)
    offs = ends - seg
    rec = route[:, :4].astype(jnp.int32)
    slot_a = jnp.take(offs, rec[:, 2]) + rec[:, 0]
    slot_b = jnp.take(offs, rec[:, 3]) + rec[:, 1]
    n_tiles = n_slots // tm
    n_active = (ends[-1] // tm).astype(jnp.int32)
    tile_start = jnp.arange(n_tiles, dtype=jnp.int32) * tm
    tile_start = jnp.minimum(tile_start, jnp.maximum(ends[-1] - tm, 0))
    tile_expert = jnp.minimum(jnp.sum(tile_start[:, None] >= ends[None, :], axis=1).astype(jnp.int32), n_e - 1)

    tok = jnp.arange(n, dtype=jnp.int32)
    key_b = jnp.where(rec[:, 3] == rec[:, 2], n_slots + tok, slot_b)
    _, by_slot = lax.sort_key_val(jnp.concatenate([slot_a, key_b]), jnp.concatenate([tok, tok]))
    e_of_slot = jnp.repeat(tile_expert, tm)
    packed_before = jnp.cumsum(counts) - counts
    src = jnp.arange(n_slots, dtype=jnp.int32) - jnp.take(offs, e_of_slot) + jnp.take(packed_before, e_of_slot)
    token_of_slot = jnp.take(by_slot, jnp.clip(src, 0, TOP_K * n - 1))
    return token_of_slot, slot_a, slot_b, tile_expert, n_active.reshape(1)


def _rope_tables(t_lat):
    rows = t_lat // GRID_W
    row = jnp.repeat(jnp.arange(rows, dtype=F32), GRID_W)
    col = jnp.tile(jnp.arange(GRID_W, dtype=F32), rows)
    n_freq = ATT_QK_DIM // 4
    inv_freq = ROPE_THETA ** (-jnp.arange(n_freq, dtype=F32) / n_freq)
    ang = jnp.concatenate([row[:, None] * inv_freq, col[:, None] * inv_freq], axis=-1)
    cos, sin = jnp.cos(ang), jnp.sin(ang)
    return (jnp.concatenate([cos, cos, cos, cos], axis=-1), jnp.concatenate([-sin, -sin, sin, sin], axis=-1))


def _layer_lower_bounds(lb_param):
    cs = jnp.cumsum(jax.nn.softmax(lb_param.astype(F32), axis=0), axis=0)
    return cs - cs[0:1]


def _win_columns(w_in_l):
    c = [w_in_l[:, i * 512:(i + 1) * 512] for i in range(8)]
    half = ATT_QK_DIM // 2

    def pair_halves(w):
        w = w.reshape(w.shape[0], ATT_HEADS, 2, 2, half)
        return w.transpose(0, 1, 3, 2, 4).reshape(w.shape[0], -1)

    return jnp.concatenate([pair_halves(c[0]), pair_halves(c[1]), c[2], c[7], c[3], c[4], c[5], c[6],
                            w_in_l[:, 4096:]], axis=1).astype(BF16)


def _pad_ff(w, axis, mult):
    ff = w.shape[axis]
    pad = (-ff) % mult
    if pad == 0:
        return w
    widths = [(0, 0)] * w.ndim
    widths[axis] = (0, pad)
    return jnp.pad(w, widths)


def kernel(x, c, ctx, c_ctx, w_ada, b_ada, norm_mix_w, norm_ffn_w, w_in, lambda_q1, lambda_k1, lambda_q2,
           lambda_k2, att_norm_w, rec_norm_w, lb_fwd, lb_bwd, w_up_att, w_up_rec, w_out, ffn_w1, ffn_w3,
           ffn_w2, router_w, moe_w1, moe_w3, moe_w2, final_norm_w):
    batch, t_lat, d = x.shape
    t_ctx = ctx.shape[1]
    depth = w_ada.shape[0]
    n_lat, n_ctx = batch * t_lat, batch * t_ctx
    tm = 512
    tm_in = 512
    tq = min(2048, t_lat)

    xl = x.reshape(n_lat, d)
    xc = ctx.reshape(n_ctx, d)

    pad_rows = (-(batch + 1)) % 8
    cc = jnp.concatenate([c, jnp.zeros((pad_rows, d), F32), c_ctx[None, :]], axis=0)
    mod_all = _mod_call(cc, w_ada, b_ada).reshape(depth, cc.shape[0], 6, d)

    cos, sin = _rope_tables(t_lat)
    lbs_f = _layer_lower_bounds(lb_fwd)
    lbs_b = _layer_lower_bounds(lb_bwd)

    for l in range(depth):
        last = l == depth - 1
        mod = mod_all[l]
        lam_init = 0.8 - 0.6 * math.exp(-0.3 * l)
        lam = (jnp.exp(jnp.sum(lambda_q1[l] * lambda_k1[l])) - jnp.exp(jnp.sum(lambda_q2[l] * lambda_k2[l]))
               + lam_init).reshape(1).astype(F32)
        lb = jnp.concatenate([lbs_f[l], lbs_b[l]])[None, :]
        w_l = _win_columns(w_in[l])
        nw = norm_mix_w[l][None, :]

        pl_lat = _inproj_call(xl, mod, nw, w_l, cos, sin, lb, tm=tm_in, rows_per_mod=t_lat, rope=True)
        pl_ctx = _inproj_call(xc, mod, nw, w_l, cos, sin, lb, tm=min(tm_in, n_ctx), rows_per_mod=None,
                              rope=False)
        qk_l, vg_l, qi_l, lf_l, g_l = pl_lat
        qk_c, vg_c, qi_c, lf_c, g_c = pl_ctx

        anw = att_norm_w[l][None, :]
        ya_l = _attn_call(lam, qk_l, qk_l, vg_l, qk_c, vg_c, anw, batch=batch, tq=tq,
                          lam_init=lam_init, with_lat=True)
        yr_l, yr_c = _gla_call(qi_l, lf_l, vg_l, qi_c, lf_c, vg_c, rec_norm_w[l][None, :], batch=batch)

        wua = w_up_att[l].astype(BF16)
        wur = w_up_rec[l].astype(BF16)
        wo = w_out[l].astype(BF16)
        nw2 = norm_ffn_w[l][None, :]
        moe_layer = l % 2 == 1
        j = l // 2
        assert moe_layer == last
        rw = None
        if moe_layer:
            rw32 = jnp.pad(router_w[j], ((0, 0), (0, LANES - N_EXPERTS)))
            rw_hi = rw32.astype(BF16)
            rw = jnp.stack([rw_hi, (rw32 - rw_hi.astype(F32)).astype(BF16)])
        res = _outproj_call(ya_l, yr_l, g_l, xl, mod, wua, wur, wo, nw2, rw, tm=tm, rows_per_mod=t_lat,
                            route_block=n_lat)
        xl = res[0]
        if not last:
            ya_c = _attn_call(lam, qk_c, None, None, qk_c, vg_c, anw, batch=batch, tq=t_ctx,
                              lam_init=lam_init, with_lat=False)
            xc, h2_c = _outproj_call(ya_c, yr_c, g_c, xc, mod, wua, wur, wo, nw2, None, tm=tm, rows_per_mod=None)

        if moe_layer:
            h2a, h2b, route, cnt = res[1:]
            counts = cnt[0, 0, :N_EXPERTS].astype(jnp.int32)
            mt = min(MOE_TILE, n_lat)
            token_of_slot, slot_a, slot_b, tile_expert, n_active = _dispatch_plan(route, counts, tm=mt)
            xa, xb = _sc_gather_rows([h2a, h2b], [token_of_slot] * 2, window=SC_WINDOW)
            oa, ob = _experts_call(tile_expert, n_active, xa, xb, moe_w1[j], moe_w3[j], moe_w2[j], tm=mt, tf=512)
            ys = _sc_gather_rows([oa, ob, oa, ob], [slot_a, slot_a, slot_b, slot_b], window=SC_WINDOW)
            xl = _combine_call(xl, ys, route, mod, final_norm_w[None, :], tm=tm, rows_per_mod=t_lat)
        else:
            h2_l = res[1]
            w1 = _pad_ff(ffn_w1[j], 1, 2 * LANES).astype(BF16)
            w3 = _pad_ff(ffn_w3[j], 1, 2 * LANES).astype(BF16)
            w2 = _pad_ff(ffn_w2[j], 0, 2 * LANES).astype(BF16)
            tf = 512
            xl = _ffn_call(h2_l, xl, mod, w1, w3, w2, tm=tm, tf=tf, rows_per_mod=t_lat)
            xc = _ffn_call(h2_c, xc, mod, w1, w3, w2, tm=tm, tf=tf, rows_per_mod=None)

    return xl.reshape(batch, t_lat, d)
```
